```python
import numpy as np
import jax
import jax.numpy as jnp
from jax import lax

D_MODEL = 1024
BATCH = 8
SEQ = 2048
DEPTH = 2

HEAD_DIM = 64
BLK = 128
CONV_CH = D_MODEL // 4
CONV_K = 31
SWA_HEADS = 4
SWA_KV_HEADS = 2
SWA_WINDOW = 128
RET_HEADS = 4
RET_QK = 64
RET_V = 128
NSA_HEADS = 4
NSA_CMP_LEN = 32
NSA_CMP_STRIDE = 16
NSA_CMP_HIDDEN = 256
NSA_SLC_LEN = 64
NSA_TOPN = 8
NSA_WINDOW = 512
NSA_FORCE = 1e4
N_EXPERTS = 64
TOP_K = 6
N_GROUPS = 8
TOPK_GROUPS = 4
D_EXPERT = D_MODEL // 4
ROUTE_SCALE = 2.5
MOE_BLK = 128
N_BRANCH = 4
ALPHA = (2 * DEPTH) ** 0.25
BETA = (8 * DEPTH) ** -0.25
LN_EPS = 1e-5
NEG_INF = -1e30

IN_WIDTHS = (
    2 * CONV_CH,
    SWA_HEADS * HEAD_DIM, SWA_KV_HEADS * HEAD_DIM, SWA_KV_HEADS * HEAD_DIM,
    RET_HEADS * RET_QK, RET_HEADS * RET_QK, RET_HEADS * RET_V, RET_HEADS * RET_V,
    NSA_HEADS * HEAD_DIM,
    HEAD_DIM, HEAD_DIM, HEAD_DIM, HEAD_DIM, HEAD_DIM, HEAD_DIM,
    3 * NSA_HEADS,
    N_BRANCH * D_MODEL,
)

kernel_name = 'hybrid_conv_swa_retention_nsa_moe_deepnorm'


def layer_norm(x, g, b):
    xf = x.astype(jnp.float32)
    mu = jnp.mean(xf, -1, keepdims=True)
    var = jnp.mean(jnp.square(xf - mu), -1, keepdims=True)
    return ((xf - mu) * lax.rsqrt(var + LN_EPS) * g + b).astype(x.dtype)


def alibi_slopes(n):
    return jnp.asarray([2.0 ** (-8.0 * (i + 1) / n) for i in range(n)], jnp.float32)


def masked_softmax(s, mask, sink=None):
    s = jnp.where(mask, s, NEG_INF)
    m = jnp.max(s, axis=-1, keepdims=True)
    if sink is not None:
        m = jnp.maximum(m, sink)
    e = jnp.where(mask, jnp.exp(s - m), 0.0)
    den = jnp.sum(e, -1, keepdims=True)
    if sink is not None:
        den = den + jnp.exp(sink - m)
    return e / jnp.maximum(den, 1e-30)


def banded_attention(q, k, v, slopes, window, sink=None):
    b, hq, s, dh = q.shape
    hkv = k.shape[1]
    rep = hq // hkv
    nb = s // BLK
    nprev = window // BLK

    def spans(t):
        tb = jnp.pad(t.reshape(b, hkv, nb, BLK, dh), ((0, 0), (0, 0), (nprev, 0), (0, 0), (0, 0)))
        return jnp.concatenate([tb[:, :, w:w + nb] for w in range(nprev + 1)], axis=3)

    ks, vs = spans(k), spans(v)
    qb = q.reshape(b, hkv, rep, nb, BLK, dh)
    sc = jnp.einsum('bgrnqd,bgnkd->bgrnqk', qb, ks, preferred_element_type=jnp.float32) * (dh ** -0.5)
    qpos = jnp.arange(nb)[:, None] * BLK + jnp.arange(BLK)[None, :]
    kpos = (jnp.arange(nb)[:, None] - nprev) * BLK + jnp.arange((nprev + 1) * BLK)[None, :]
    dist = qpos[:, :, None] - kpos[:, None, :]
    mask = (dist >= 0) & (dist < window) & (kpos[:, None, :] >= 0)
    sc = sc - slopes.reshape(1, hkv, rep, 1, 1, 1) * dist.astype(jnp.float32)
    sink_b = None if sink is None else sink.astype(jnp.float32).reshape(1, hkv, rep, 1, 1, 1)
    p = masked_softmax(sc, mask, sink_b)
    o = jnp.einsum('bgrnqk,bgnkd->bgrnqd', p.astype(v.dtype), vs)
    return o.reshape(b, hq, s, dh)


def conformer_conv(u, dw, db, g, b):
    a, gate = jnp.split(u, 2, axis=-1)
    h = a * jax.nn.sigmoid(gate)
    c = h.shape[-1]
    hp = jnp.pad(h, ((0, 0), (CONV_K - 1, 0), (0, 0)))
    y = lax.conv_general_dilated(hp, dw[:, None, :], window_strides=(1,), padding='VALID',
                                 dimension_numbers=('NWC', 'WIO', 'NWC'), feature_group_count=c) + db
    return jax.nn.silu(layer_norm(y, g, b))


def retention(q, k, v, g, gn_g, gn_b):
    b, s, _ = q.shape
    nc = s // BLK
    f32 = jnp.float32

    def heads(t, d):
        return t.reshape(b, nc, BLK, RET_HEADS, d).transpose(0, 3, 1, 2, 4).astype(f32)

    qh = heads(q, RET_QK)
    kh = heads(k, RET_QK) * (RET_QK ** -0.5)
    vh = heads(v, RET_V)
    log_gamma = jnp.log1p(-(2.0 ** (-5.0 - jnp.arange(RET_HEADS, dtype=f32))))
    idx = jnp.arange(BLK, dtype=f32)
    diff = idx[:, None] - idx[None, :]
    dmat = jnp.where(diff >= 0, jnp.exp(log_gamma[:, None, None] * jnp.maximum(diff, 0.0)), 0.0)
    inner = jnp.einsum('bhnqd,bhnkd->bhnqk', qh, kh) * dmat[None, :, None]
    o_inner = jnp.einsum('bhnqk,bhnkd->bhnqd', inner, vh)
    zeta = jnp.exp(log_gamma[:, None] * (BLK - 1 - idx)[None, :])
    xi = jnp.exp(log_gamma[:, None] * (idx + 1.0)[None, :])
    chunk_decay = jnp.exp(log_gamma * BLK)
    kv = jnp.einsum('bhnsd,bhnse->bhnde', kh * zeta[None, :, None, :, None], vh)

    def step(state, kv_c):
        return chunk_decay[None, :, None, None] * state + kv_c, state

    init = jnp.zeros((b, RET_HEADS, RET_QK, RET_V), f32)
    _, prev = lax.scan(step, init, jnp.moveaxis(kv, 2, 0))
    prev = jnp.moveaxis(prev, 0, 2)
    o_cross = jnp.einsum('bhnqd,bhnde->bhnqe', qh, prev) * xi[None, :, None, :, None]
    o = o_inner + o_cross
    mu = jnp.mean(o, -1, keepdims=True)
    var = jnp.mean(jnp.square(o - mu), -1, keepdims=True)
    o = (o - mu) * lax.rsqrt(var + LN_EPS)
    o = o.transpose(0, 2, 3, 1, 4).reshape(b, s, RET_HEADS * RET_V) * gn_g + gn_b
    return (jax.nn.silu(g.astype(f32)) * o).astype(g.dtype)


def native_sparse_attention(q, kc, vc, ks, vs, kw, vw, gates, slopes,
                            pe_k, w1_k, w2_k, pe_v, w1_v, w2_v):
    b, s, _ = q.shape
    dh = HEAD_DIM
    f32 = jnp.float32
    qh = q.reshape(b, s, NSA_HEADS, dh).transpose(0, 2, 1, 3)
    tpos = jnp.arange(s)

    def compress(t, pe, w1, w2):
        tc = t.reshape(b, s // NSA_CMP_STRIDE, NSA_CMP_STRIDE, dh)
        blocks = jnp.concatenate([tc[:, :-1], tc[:, 1:]], axis=2) + pe
        flat = blocks.reshape(b, blocks.shape[1], NSA_CMP_LEN * dh)
        return jax.nn.gelu(flat @ w1, approximate=False) @ w2

    kcmp = compress(kc, pe_k, w1_k, w2_k)
    vcmp = compress(vc, pe_v, w1_v, w2_v)
    ncmp = kcmp.shape[1]
    cstart = jnp.arange(ncmp) * NSA_CMP_STRIDE
    cdist = tpos[:, None] - (cstart + NSA_CMP_LEN - 1)[None, :]
    sc = jnp.einsum('bhtd,bcd->bhtc', qh, kcmp, preferred_element_type=f32) * (dh ** -0.5)
    sc = sc - slopes[:, None, None] * cdist.astype(f32)
    p_cmp = masked_softmax(sc, cdist >= 0)
    o_cmp = jnp.einsum('bhtc,bcd->bhtd', p_cmp.astype(vcmp.dtype), vcmp)

    nslc = s // NSA_SLC_LEN
    sstart = jnp.arange(nslc) * NSA_SLC_LEN
    overlap = ((cstart[:, None] < sstart[None, :] + NSA_SLC_LEN)
               & (cstart[:, None] + NSA_CMP_LEN > sstart[None, :])).astype(f32)
    imp = jnp.einsum('bhtc,cj->btj', p_cmp, overlap)
    cur = tpos // NSA_SLC_LEN
    jj = jnp.arange(nslc)
    forced = (jj[None, :] == 0) | (jj[None, :] == cur[:, None]) | (jj[None, :] == cur[:, None] - 1)
    future = jj[None, :] > cur[:, None]
    imp = jnp.where(forced, NSA_FORCE, jnp.where(future, -1.0, imp))
    _, sel = lax.top_k(imp, min(NSA_TOPN, nslc))

    kb = ks.reshape(b, nslc, NSA_SLC_LEN, dh)
    vb = vs.reshape(b, nslc, NSA_SLC_LEN, dh)
    nq = s // BLK
    take = jax.vmap(lambda t, i: t[i])

    def sel_block(args):
        q_blk, idx_blk, t_blk = args
        k_sel = take(kb, idx_blk)
        v_sel = take(vb, idx_blk)
        kpos = idx_blk[..., None] * NSA_SLC_LEN + jnp.arange(NSA_SLC_LEN)
        dist = t_blk[None, :, None, None] - kpos
        sc_s = jnp.einsum('bhqd,bqnld->bhqnl', q_blk, k_sel, preferred_element_type=f32) * (dh ** -0.5)
        sc_s = sc_s - slopes[None, :, None, None, None] * dist[:, None].astype(f32)
        p = masked_softmax(sc_s.reshape(b, NSA_HEADS, BLK, -1), (dist >= 0).reshape(b, 1, BLK, -1))
        return jnp.einsum('bhqm,bqmd->bhqd', p.astype(v_sel.dtype), v_sel.reshape(b, BLK, -1, dh))

    xs = (jnp.moveaxis(qh.reshape(b, NSA_HEADS, nq, BLK, dh), 2, 0),
          jnp.moveaxis(sel.reshape(b, nq, BLK, -1), 1, 0),
          tpos.reshape(nq, BLK))
    o_slc = jnp.moveaxis(lax.map(sel_block, xs), 0, 2).reshape(b, NSA_HEADS, s, dh)

    o_win = banded_attention(qh, kw[:, None], vw[:, None], slopes, NSA_WINDOW)

    gt = jax.nn.sigmoid(gates.astype(f32)).reshape(b, s, 3, NSA_HEADS).transpose(2, 0, 3, 1)[..., None]
    o = gt[0] * o_cmp + gt[1] * o_slc + gt[2] * o_win
    return o.transpose(0, 2, 1, 3).reshape(b, s, NSA_HEADS * dh).astype(q.dtype)


def token_mixer(x, w_in, conv_dw, conv_db, conv_ln_g, conv_ln_b, swa_sink, ret_gn_g, ret_gn_b,
                nsa_pe_k, nsa_w1_k, nsa_w2_k, nsa_pe_v, nsa_w1_v, nsa_w2_v,
                p_conv, p_swa, p_ret, p_nsa, w_out, slopes_swa, slopes_nsa):
    b, s, d = x.shape
    u = x @ w_in
    (conv_in, swa_q, swa_k, swa_v, ret_q, ret_k, ret_v, ret_g,
     nsa_q, nsa_kc, nsa_vc, nsa_ks, nsa_vs, nsa_kw, nsa_vw, nsa_gate, merge_gate) = jnp.split(
        u, [int(c) for c in np.cumsum(IN_WIDTHS)[:-1]], axis=-1)

    def heads(t, n):
        return t.reshape(b, s, n, HEAD_DIM).transpose(0, 2, 1, 3)

    y_conv = conformer_conv(conv_in, conv_dw, conv_db, conv_ln_g, conv_ln_b) @ p_conv
    o_swa = banded_attention(heads(swa_q, SWA_HEADS), heads(swa_k, SWA_KV_HEADS),
                             heads(swa_v, SWA_KV_HEADS), slopes_swa, SWA_WINDOW, swa_sink)
    y_swa = o_swa.transpose(0, 2, 1, 3).reshape(b, s, SWA_HEADS * HEAD_DIM) @ p_swa
    y_ret = retention(ret_q, ret_k, ret_v, ret_g, ret_gn_g, ret_gn_b) @ p_ret
    y_nsa = native_sparse_attention(nsa_q, nsa_kc, nsa_vc, nsa_ks, nsa_vs, nsa_kw, nsa_vw, nsa_gate,
                                    slopes_nsa, nsa_pe_k, nsa_w1_k, nsa_w2_k,
                                    nsa_pe_v, nsa_w1_v, nsa_w2_v) @ p_nsa
    gts = jax.nn.sigmoid(merge_gate.reshape(b, s, N_BRANCH, d).astype(jnp.float32))
    merged = (gts[..., 0, :] * y_conv + gts[..., 1, :] * y_swa
              + gts[..., 2, :] * y_ret + gts[..., 3, :] * y_nsa).astype(x.dtype)
    return merged @ w_out


def moe_ffn(x, router_w, router_b, wg, wu, wd, sh_wg, sh_wu, sh_wd):
    b, s, d = x.shape
    n_tok = b * s
    xt = x.reshape(n_tok, d)
    scores = jax.nn.sigmoid(xt.astype(jnp.float32) @ router_w.astype(jnp.float32).T)
    biased = scores + router_b.astype(jnp.float32)
    grp_score = lax.top_k(biased.reshape(n_tok, N_GROUPS, N_EXPERTS // N_GROUPS), 2)[0].sum(-1)
    _, gidx = lax.top_k(grp_score, TOPK_GROUPS)
    gmask = jnp.any(gidx[:, :, None] == jnp.arange(N_GROUPS)[None, None, :], axis=1)
    emask = jnp.repeat(gmask, N_EXPERTS // N_GROUPS, axis=1)
    _, eidx = lax.top_k(jnp.where(emask, biased, NEG_INF), TOP_K)
    gate = jnp.take_along_axis(scores, eidx, axis=1)
    gate = gate / jnp.sum(gate, -1, keepdims=True) * ROUTE_SCALE

    n_asg = n_tok * TOP_K
    e_flat = eidx.reshape(-1)
    order = jnp.argsort(e_flat)
    e_sorted = e_flat[order]
    tok_sorted = (jnp.arange(n_asg, dtype=jnp.int32) // TOP_K)[order]
    gate_sorted = gate.reshape(-1)[order]
    counts = jnp.bincount(e_flat, length=N_EXPERTS)
    padded = (counts + MOE_BLK - 1) // MOE_BLK * MOE_BLK
    pend = jnp.cumsum(padded)
    dest = (pend - padded)[e_sorted] + jnp.arange(n_asg) - (jnp.cumsum(counts) - counts)[e_sorted]
    n_blocks = (n_asg + N_EXPERTS * (MOE_BLK - 1) + MOE_BLK - 1) // MOE_BLK
    rows = n_blocks * MOE_BLK
    row_tok = jnp.full((rows,), n_tok, jnp.int32).at[dest].set(tok_sorted)
    row_gate = jnp.zeros((rows,), jnp.float32).at[dest].set(gate_sorted)
    blk_exp = jnp.minimum(jnp.searchsorted(pend, jnp.arange(n_blocks) * MOE_BLK, side='right'),
                          N_EXPERTS - 1)
    xpad = jnp.concatenate([xt, jnp.zeros((1, d), xt.dtype)], axis=0)

    def expert_block(args):
        tok, e = args
        h = xpad[tok]
        return (jax.nn.silu(h @ wg[e]) * (h @ wu[e])) @ wd[e]

    y_rows = lax.map(expert_block, (row_tok.reshape(n_blocks, MOE_BLK), blk_exp)).reshape(rows, d)
    routed = jax.ops.segment_sum(y_rows * row_gate[:, None], row_tok, num_segments=n_tok + 1)[:n_tok]
    shared = (jax.nn.silu(xt @ sh_wg) * (xt @ sh_wu)) @ sh_wd
    return (routed + shared).reshape(b, s, d).astype(x.dtype)


def setup_inputs(seed: int = 0) -> dict:
    key = jax.random.key(seed)
    keys = iter(jax.random.split(key, 40))
    L = DEPTH
    D = D_MODEL

    def nrm(shape, scale):
        return jax.random.normal(next(keys), shape, jnp.float32) * scale

    return {
        'x': nrm((BATCH, SEQ, D), 1.0),
        'w_in': nrm((L, D, sum(IN_WIDTHS)), D ** -0.5),
        'conv_dw': nrm((L, CONV_K, CONV_CH), CONV_K ** -0.5),
        'conv_db': nrm((L, CONV_CH), 0.02),
        'conv_ln_g': 1.0 + nrm((L, CONV_CH), 0.02),
        'conv_ln_b': nrm((L, CONV_CH), 0.02),
        'swa_sink': nrm((L, SWA_HEADS), 0.5),
        'ret_gn_g': 1.0 + nrm((L, RET_HEADS * RET_V), 0.02),
        'ret_gn_b': nrm((L, RET_HEADS * RET_V), 0.02),
        'nsa_pe_k': nrm((L, NSA_CMP_LEN, HEAD_DIM), 0.02),
        'nsa_w1_k': nrm((L, NSA_CMP_LEN * HEAD_DIM, NSA_CMP_HIDDEN), (NSA_CMP_LEN * HEAD_DIM) ** -0.5),
        'nsa_w2_k': nrm((L, NSA_CMP_HIDDEN, HEAD_DIM), NSA_CMP_HIDDEN ** -0.5),
        'nsa_pe_v': nrm((L, NSA_CMP_LEN, HEAD_DIM), 0.02),
        'nsa_w1_v': nrm((L, NSA_CMP_LEN * HEAD_DIM, NSA_CMP_HIDDEN), (NSA_CMP_LEN * HEAD_DIM) ** -0.5),
        'nsa_w2_v': nrm((L, NSA_CMP_HIDDEN, HEAD_DIM), NSA_CMP_HIDDEN ** -0.5),
        'p_conv': nrm((L, CONV_CH, D), CONV_CH ** -0.5),
        'p_swa': nrm((L, SWA_HEADS * HEAD_DIM, D), (SWA_HEADS * HEAD_DIM) ** -0.5),
        'p_ret': nrm((L, RET_HEADS * RET_V, D), (RET_HEADS * RET_V) ** -0.5),
        'p_nsa': nrm((L, NSA_HEADS * HEAD_DIM, D), (NSA_HEADS * HEAD_DIM) ** -0.5),
        'w_out': nrm((L, D, D), BETA * D ** -0.5),
        'ln1_g': 1.0 + nrm((L, D), 0.02),
        'ln1_b': nrm((L, D), 0.02),
        'router_w': nrm((L, N_EXPERTS, D), D ** -0.5),
        'router_b': nrm((L, N_EXPERTS), 0.01),
        'moe_wg': nrm((L, N_EXPERTS, D, D_EXPERT), D ** -0.5),
        'moe_wu': nrm((L, N_EXPERTS, D, D_EXPERT), D ** -0.5),
        'moe_wd': nrm((L, N_EXPERTS, D_EXPERT, D), BETA * D_EXPERT ** -0.5),
        'sh_wg': nrm((L, D, D_EXPERT), D ** -0.5),
        'sh_wu': nrm((L, D, D_EXPERT), D ** -0.5),
        'sh_wd': nrm((L, D_EXPERT, D), BETA * D_EXPERT ** -0.5),
        'ln2_g': 1.0 + nrm((L, D), 0.02),
        'ln2_b': nrm((L, D), 0.02),
    }


def reference(x, w_in, conv_dw, conv_db, conv_ln_g, conv_ln_b, swa_sink, ret_gn_g, ret_gn_b,
              nsa_pe_k, nsa_w1_k, nsa_w2_k, nsa_pe_v, nsa_w1_v, nsa_w2_v,
              p_conv, p_swa, p_ret, p_nsa, w_out, ln1_g, ln1_b,
              router_w, router_b, moe_wg, moe_wu, moe_wd, sh_wg, sh_wu, sh_wd, ln2_g, ln2_b):
    slopes = alibi_slopes(SWA_HEADS + NSA_HEADS)
    slopes_swa, slopes_nsa = slopes[:SWA_HEADS], slopes[SWA_HEADS:]
    for l in range(DEPTH):
        mix = token_mixer(x, w_in[l], conv_dw[l], conv_db[l], conv_ln_g[l], conv_ln_b[l], swa_sink[l],
                          ret_gn_g[l], ret_gn_b[l], nsa_pe_k[l], nsa_w1_k[l], nsa_w2_k[l],
                          nsa_pe_v[l], nsa_w1_v[l], nsa_w2_v[l], p_conv[l], p_swa[l], p_ret[l], p_nsa[l],
                          w_out[l], slopes_swa, slopes_nsa)
        x = layer_norm(ALPHA * x + mix, ln1_g[l], ln1_b[l])
        ffn = moe_ffn(x, router_w[l], router_b[l], moe_wg[l], moe_wu[l], moe_wd[l],
                      sh_wg[l], sh_wu[l], sh_wd[l])
        x = layer_norm(ALPHA * x + ffn, ln2_g[l], ln2_b[l])
    return x
```

```python
import functools
import math

import jax
import jax.numpy as jnp
import numpy as np
from jax import lax
from jax.experimental import pallas as pl
from jax.experimental.pallas import tpu as pltpu

F32 = jnp.float32
BF16 = jnp.bfloat16

D_MODEL = 1024
DEPTH = 2
HEAD_DIM = 64
BLK = 128
CONV_CH = 256
CONV_K = 31
SWA_HEADS = 4
SWA_KV_HEADS = 2
SWA_WINDOW = 128
RET_HEADS = 4
RET_QK = 64
RET_V = 128
NSA_HEADS = 4
NSA_CMP_LEN = 32
NSA_CMP_STRIDE = 16
NSA_CMP_HIDDEN = 256
NSA_SLC_LEN = 64
NSA_TOPN = 8
NSA_WINDOW = 512
NSA_FORCE = 1e4
N_EXPERTS = 64
TOP_K = 6
N_GROUPS = 8
GROUP_SIZE = N_EXPERTS // N_GROUPS
TOPK_GROUPS = 4
D_EXPERT = 256
ROUTE_SCALE = 2.5
MOE_BLK = 128
N_BRANCH = 4
ALPHA = (2 * DEPTH) ** 0.25
LN_EPS = 1e-5
NEG_INF = -1e30
SCALE = HEAD_DIM ** -0.5

C_CONV = 0
C_SWA = 512
C_RETQK = 1024
C_RETV = 1536
C_RETG = 2048
C_NSAQ = 2560
C_NSAC = 2816
C_NSAS = 2944
C_NSAW = 3072
C_NSAG = 3200
N_MIX_COLS = 3212
UA = 3328

ALIBI = [2.0 ** (-8.0 * (i + 1) / 8) for i in range(8)]
SLOPES_SWA = ALIBI[:4]
SLOPES_NSA = ALIBI[4:]
RET_LOG_GAMMA = [math.log1p(-(2.0 ** (-5.0 - h))) for h in range(RET_HEADS)]

VMEM_LIMIT = 48 * 1024 * 1024
NT_DIMS = (((1,), (1,)), ((), ()))
TN_DIMS = (((0,), (0,)), ((), ()))


def _params(*sem):
    return pltpu.CompilerParams(dimension_semantics=sem, vmem_limit_bytes=VMEM_LIMIT)


def _dot(a, b):
    return jnp.dot(a, b, preferred_element_type=F32)


def _dot_nt(a, b):
    return lax.dot_general(a, b, NT_DIMS, preferred_element_type=F32)


def _layer_norm(v, g, b):
    mu = jnp.mean(v, axis=-1, keepdims=True)
    c = v - mu
    var = jnp.mean(c * c, axis=-1, keepdims=True)
    return c * lax.rsqrt(var + LN_EPS) * g + b


def _silu(v):
    return v * jax.nn.sigmoid(v)


def _pack_halves(v):
    n = v.shape[1] // 2
    r = v.astype(BF16).astype(F32)
    lo = lax.shift_right_logical(lax.bitcast_convert_type(r[:, :n], jnp.uint32), jnp.uint32(16))
    hi = lax.bitcast_convert_type(r[:, n:], jnp.uint32) & jnp.uint32(0xFFFF0000)
    return lo | hi


def _unpack_halves(w):
    lo = lax.bitcast_convert_type(lax.shift_left(w, jnp.uint32(16)), F32)
    hi = lax.bitcast_convert_type(w & jnp.uint32(0xFFFF0000), F32)
    return jnp.concatenate([lo, hi], axis=1)


INPROJ_TM = 512


def _inproj_kernel(x_ref, w_ref, o_ref):
    xb = x_ref[...].astype(BF16)
    for c in range(0, UA, 256):
        o_ref[:, c:c + 256] = _dot(xb, w_ref[:, c:c + 256]).astype(o_ref.dtype)


def _inproj(x2d, w_a):
    t = x2d.shape[0]
    return pl.pallas_call(
        _inproj_kernel,
        grid=(t // INPROJ_TM,),
        in_specs=[pl.BlockSpec((INPROJ_TM, D_MODEL), lambda i: (i, 0)),
                  pl.BlockSpec((D_MODEL, UA), lambda i: (0, 0))],
        out_specs=pl.BlockSpec((INPROJ_TM, UA), lambda i: (i, 0)),
        out_shape=jax.ShapeDtypeStruct((t, UA), BF16),
        compiler_params=_params("parallel"),
        name="inproj",
    )(x2d, w_a)


CONV_TS = 256
CONV_HALO = 32


def _conv_kernel(cur_ref, halo_ref, dw_ref, db_ref, g_ref, b_ref, o_ref, hc_ref):
    i = pl.program_id(1)

    def glu(u):
        u = u.astype(F32)
        return u[:, :CONV_CH] * jax.nn.sigmoid(u[:, CONV_CH:])

    hc_ref[0:CONV_HALO, :] = jnp.where(i > 0, glu(halo_ref[...]), 0.0)
    hc_ref[CONV_HALO:CONV_HALO + CONV_TS, :] = glu(cur_ref[...])
    acc = jnp.broadcast_to(db_ref[...], (CONV_TS, CONV_CH))
    off = CONV_HALO - (CONV_K - 1)
    for j in range(CONV_K):
        acc = acc + hc_ref[off + j:off + j + CONV_TS, :] * dw_ref[j:j + 1, :]
    o_ref[...] = _silu(_layer_norm(acc, g_ref[...], b_ref[...])).astype(o_ref.dtype)


def _conv(ua, dw, db, g, b, bsz, seq):
    t = bsz * seq
    ns = seq // CONV_TS
    per = CONV_TS // CONV_HALO
    dwp = jnp.concatenate([dw, jnp.zeros((32 - CONV_K, CONV_CH), F32)], axis=0)
    vec = pl.BlockSpec((1, CONV_CH), lambda bb, i: (0, 0))
    return pl.pallas_call(
        _conv_kernel,
        grid=(bsz, ns),
        in_specs=[pl.BlockSpec((CONV_TS, 512), lambda bb, i: (bb * ns + i, C_CONV // 512)),
                  pl.BlockSpec((CONV_HALO, 512),
                               lambda bb, i: (jnp.maximum((bb * ns + i) * per - 1, 0), C_CONV // 512)),
                  pl.BlockSpec((32, CONV_CH), lambda bb, i: (0, 0)), vec, vec, vec],
        out_specs=pl.BlockSpec((CONV_TS, CONV_CH), lambda bb, i: (bb * ns + i, 0)),
        out_shape=jax.ShapeDtypeStruct((t, CONV_CH), BF16),
        scratch_shapes=[pltpu.VMEM((CONV_HALO + CONV_TS, CONV_CH), F32)],
        compiler_params=_params("parallel", "parallel"),
        name="conformer_conv",
    )(ua, ua, dwp, db.reshape(1, -1), g.reshape(1, -1), b.reshape(1, -1))


def _swa_kernel(sink_ref, q_ref, kvc_ref, kvp_ref, o_ref):
    i = pl.program_id(1)
    q = q_ref[...]
    kvc = kvc_ref[...]
    kvp = kvp_ref[...]
    a = lax.broadcasted_iota(jnp.int32, (BLK, 2 * BLK), 0)
    c = lax.broadcasted_iota(jnp.int32, (BLK, 2 * BLK), 1)
    dist = BLK + a - c
    mask = (dist >= 0) & (dist < SWA_WINDOW) & ((c >= BLK) | (i > 0))
    distf = dist.astype(F32)
    rep = SWA_HEADS // SWA_KV_HEADS
    for h in range(SWA_HEADS):
        g = h // rep
        k = jnp.concatenate([kvp[:, g * HEAD_DIM:(g + 1) * HEAD_DIM],
                             kvc[:, g * HEAD_DIM:(g + 1) * HEAD_DIM]], axis=0)
        v = jnp.concatenate([kvp[:, BLK + g * HEAD_DIM:BLK + (g + 1) * HEAD_DIM],
                             kvc[:, BLK + g * HEAD_DIM:BLK + (g + 1) * HEAD_DIM]], axis=0)
        s = _dot_nt(q[:, h * HEAD_DIM:(h + 1) * HEAD_DIM], k) * SCALE - SLOPES_SWA[h] * distf
        s = jnp.where(mask, s, NEG_INF)
        sink = sink_ref[h]
        m = jnp.maximum(jnp.max(s, axis=-1, keepdims=True), sink)
        e = jnp.where(mask, jnp.exp(s - m), 0.0)
        den = jnp.sum(e, axis=-1, keepdims=True) + jnp.exp(sink - m)
        p = e / jnp.maximum(den, 1e-30)
        o_ref[:, h * HEAD_DIM:(h + 1) * HEAD_DIM] = _dot(p.astype(BF16), v).astype(o_ref.dtype)


def _swa(ua, sink, bsz, seq):
    t = bsz * seq
    nb = seq // BLK
    return pl.pallas_call(
        _swa_kernel,
        grid=(bsz, nb),
        in_specs=[pl.BlockSpec(memory_space=pltpu.SMEM),
                  pl.BlockSpec((BLK, 256), lambda bb, i: (bb * nb + i, C_SWA // 256)),
                  pl.BlockSpec((BLK, 256), lambda bb, i: (bb * nb + i, C_SWA // 256 + 1)),
                  pl.BlockSpec((BLK, 256), lambda bb, i: (jnp.maximum(bb * nb + i - 1, 0), C_SWA // 256 + 1))],
        out_specs=pl.BlockSpec((BLK, 256), lambda bb, i: (bb * nb + i, 0)),
        out_shape=jax.ShapeDtypeStruct((t, SWA_HEADS * HEAD_DIM), BF16),
        compiler_params=_params("parallel", "parallel"),
        name="swa",
    )(sink, ua, ua, ua)


def _ret_kernel(qk_ref, v_ref, g_ref, gng_ref, gnb_ref, o_ref, state_ref):
    n = pl.program_id(1)

    @pl.when(n == 0)
    def _():
        state_ref[...] = jnp.zeros_like(state_ref)

    qk = qk_ref[...]
    vv = v_ref[...]
    gate = g_ref[...].astype(F32)
    a = lax.broadcasted_iota(jnp.int32, (BLK, BLK), 0)
    c = lax.broadcasted_iota(jnp.int32, (BLK, BLK), 1)
    diff = (a - c).astype(F32)
    idx = lax.broadcasted_iota(jnp.int32, (BLK, 1), 0).astype(F32)
    hq = RET_HEADS * RET_QK
    for h in range(RET_HEADS):
        lg = RET_LOG_GAMMA[h]
        q = qk[:, h * RET_QK:(h + 1) * RET_QK]
        k = qk[:, hq + h * RET_QK:hq + (h + 1) * RET_QK].astype(F32) * (RET_QK ** -0.5)
        v = vv[:, h * RET_V:(h + 1) * RET_V]
        dmat = jnp.where(diff >= 0, jnp.exp(lg * jnp.maximum(diff, 0.0)), 0.0)
        inner = _dot_nt(q, k.astype(BF16)) * dmat
        o_inner = _dot(inner.astype(BF16), v)
        zeta = jnp.exp(lg * (BLK - 1 - idx))
        xi = jnp.exp(lg * (idx + 1.0))
        kv = lax.dot_general((k * zeta).astype(BF16), v, TN_DIMS, preferred_element_type=F32)
        prev = state_ref[h]
        o = o_inner + _dot(q, prev.astype(BF16)) * xi
        state_ref[h] = math.exp(lg * BLK) * prev + kv
        mu = jnp.mean(o, axis=-1, keepdims=True)
        cen = o - mu
        var = jnp.mean(cen * cen, axis=-1, keepdims=True)
        sl = slice(h * RET_V, (h + 1) * RET_V)
        on = cen * lax.rsqrt(var + LN_EPS) * gng_ref[:, sl] + gnb_ref[:, sl]
        o_ref[:, sl] = (_silu(gate[:, sl]) * on).astype(o_ref.dtype)


def _retention(ua, gn_g, gn_b, bsz, seq):
    t = bsz * seq
    nc = seq // BLK
    vec = pl.BlockSpec((1, RET_HEADS * RET_V), lambda bb, i: (0, 0))
    return pl.pallas_call(
        _ret_kernel,
        grid=(bsz, nc),
        in_specs=[pl.BlockSpec((BLK, 512), lambda bb, i: (bb * nc + i, C_RETQK // 512)),
                  pl.BlockSpec((BLK, 512), lambda bb, i: (bb * nc + i, C_RETV // 512)),
                  pl.BlockSpec((BLK, 512), lambda bb, i: (bb * nc + i, C_RETG // 512)),
                  vec, vec],
        out_specs=pl.BlockSpec((BLK, 512), lambda bb, i: (bb * nc + i, 0)),
        out_shape=jax.ShapeDtypeStruct((t, RET_HEADS * RET_V), BF16),
        scratch_shapes=[pltpu.VMEM((RET_HEADS, RET_QK, RET_V), F32)],
        compiler_params=_params("parallel", "arbitrary"),
        name="retention",
    )(ua, ua, ua, gn_g.reshape(1, -1), gn_b.reshape(1, -1))


def _compress_kernel(tk_ref, tv_ref, pek_ref, w1k_ref, w2k_ref, pev_ref, w1v_ref, w2v_ref, ok_ref, ov_ref):
    half = NSA_CMP_STRIDE * HEAD_DIM

    def one(t_ref, pe_ref, w1_ref, w2_ref, o_ref):
        tc = t_ref[0].astype(F32)
        ncmp = tc.shape[0]
        lo = (tc + pe_ref[:, :half]).astype(BF16)
        hi = (tc + pe_ref[:, half:]).astype(BF16)
        first = _dot(lo, w1_ref[:half, :])
        second = _dot(hi, w1_ref[half:, :])
        hid = first + pltpu.roll(second, ncmp - 1, axis=0)
        act = 0.5 * hid * (1.0 + lax.erf(hid * (2.0 ** -0.5)))
        o_ref[0] = _dot(act.astype(BF16), w2_ref[...]).astype(o_ref.dtype)

    one(tk_ref, pek_ref, w1k_ref, w2k_ref, ok_ref)
    one(tv_ref, pev_ref, w1v_ref, w2v_ref, ov_ref)


def _compress(tk, tv, pe_k, w1_k, w2_k, pe_v, w1_v, w2_v):
    bsz, ng, width = tk.shape
    tsp = pl.BlockSpec((1, ng, width), lambda bb: (bb, 0, 0))
    pes = pl.BlockSpec((1, width * 2), lambda bb: (0, 0))
    w1s = pl.BlockSpec((width * 2, NSA_CMP_HIDDEN), lambda bb: (0, 0))
    w2s = pl.BlockSpec((NSA_CMP_HIDDEN, HEAD_DIM), lambda bb: (0, 0))
    osp = pl.BlockSpec((1, ng, HEAD_DIM), lambda bb: (bb, 0, 0))
    return pl.pallas_call(
        _compress_kernel,
        grid=(bsz,),
        in_specs=[tsp, tsp, pes, w1s, w2s, pes, w1s, w2s],
        out_specs=[osp, osp],
        out_shape=[jax.ShapeDtypeStruct((bsz, ng, HEAD_DIM), BF16)] * 2,
        compiler_params=_params("parallel"),
        name="nsa_compress",
    )(tk, tv, pe_k.reshape(1, -1), w1_k, w2_k, pe_v.reshape(1, -1), w1_v, w2_v)


NSA_KC = 512
NSA_WSPAN = NSA_WINDOW + BLK


def _masked_softmax(s, mask):
    s = jnp.where(mask, s, NEG_INF)
    m = jnp.max(s, axis=-1, keepdims=True)
    e = jnp.where(mask, jnp.exp(s - m), 0.0)
    return e / jnp.maximum(jnp.sum(e, axis=-1, keepdims=True), 1e-30)


def _nsa_kernel(q_ref, kcmp_ref, vcmp_ref, ksvs_ref, kwvw_ref, gate_ref, o_ref, selk_ref):
    i = pl.program_id(1)
    seq = ksvs_ref.shape[0]
    ncmp = kcmp_ref.shape[1]
    nslc = seq // NSA_SLC_LEN
    q = q_ref[...]
    qh = [q[:, h * HEAD_DIM:(h + 1) * HEAD_DIM] for h in range(NSA_HEADS)]
    t0 = i * BLK
    tq = t0 + lax.broadcasted_iota(jnp.int32, (BLK, 1), 0)

    kcmp = kcmp_ref[0]
    vcmp = vcmp_ref[0]
    cend = lax.broadcasted_iota(jnp.int32, (1, ncmp), 1) * NSA_CMP_STRIDE + (NSA_CMP_LEN - 1)
    cdist = tq - cend
    cmask = cdist >= 0
    cdistf = cdist.astype(F32)
    o_cmp = []
    psum = jnp.zeros((BLK, ncmp), F32)
    for h in range(NSA_HEADS):
        p = _masked_softmax(_dot_nt(qh[h], kcmp) * SCALE - SLOPES_NSA[h] * cdistf, cmask)
        psum = psum + p
        o_cmp.append(_dot(p.astype(BF16), vcmp))

    cst = lax.broadcasted_iota(jnp.int32, (ncmp, nslc), 0) * NSA_CMP_STRIDE
    sst = lax.broadcasted_iota(jnp.int32, (ncmp, nslc), 1) * NSA_SLC_LEN
    overlap = jnp.where((cst < sst + NSA_SLC_LEN) & (cst + NSA_CMP_LEN > sst), 1.0, 0.0).astype(BF16)
    p_hi = psum.astype(BF16)
    p_lo = (psum - p_hi.astype(F32)).astype(BF16)
    imp = _dot(p_hi, overlap) + _dot(p_lo, overlap)
    jj = lax.broadcasted_iota(jnp.int32, (BLK, nslc), 1)
    cur = tq // NSA_SLC_LEN
    forced = (jj == 0) | (jj == cur) | (jj == cur - 1)
    imp = jnp.where(forced, NSA_FORCE, jnp.where(jj > cur, -1.0, imp))

    sel = jnp.zeros((BLK, nslc), F32)
    for _ in range(min(NSA_TOPN, nslc)):
        mx = jnp.max(imp, axis=-1, keepdims=True)
        first = jnp.min(jnp.where(imp == mx, jj, nslc), axis=-1, keepdims=True)
        hit = jj == first
        sel = jnp.where(hit, 1.0, sel)
        imp = jnp.where(hit, -3e38, imp)
    kblk = lax.broadcasted_iota(jnp.int32, (nslc, seq), 1) // NSA_SLC_LEN
    expand = jnp.where(kblk == lax.broadcasted_iota(jnp.int32, (nslc, seq), 0), 1.0, 0.0).astype(BF16)
    selk_ref[...] = _dot(sel.astype(BF16), expand)

    def slc_step(c, carry):
        ms, ls, accs = carry
        start = pl.multiple_of(c * NSA_KC, NSA_KC)
        kv = ksvs_ref[pl.ds(start, NSA_KC), :]
        ks = kv[:, :HEAD_DIM]
        vs = kv[:, HEAD_DIM:]
        kpos = start + lax.broadcasted_iota(jnp.int32, (1, NSA_KC), 1)
        dist = tq - kpos
        mask = (selk_ref[:, pl.ds(start, NSA_KC)] > 0.5) & (dist >= 0)
        distf = dist.astype(F32)
        nm, nl, na = [], [], []
        for h in range(NSA_HEADS):
            s = _dot_nt(qh[h], ks) * SCALE - SLOPES_NSA[h] * distf
            s = jnp.where(mask, s, NEG_INF)
            m_new = jnp.maximum(ms[h], jnp.max(s, axis=-1, keepdims=True))
            alpha = jnp.exp(ms[h] - m_new)
            p = jnp.where(mask, jnp.exp(s - m_new), 0.0)
            nm.append(m_new)
            nl.append(alpha * ls[h] + jnp.sum(p, axis=-1, keepdims=True))
            na.append(alpha * accs[h] + _dot(p.astype(BF16), vs))
        return tuple(nm), tuple(nl), tuple(na)

    init = (tuple(jnp.full((BLK, 1), NEG_INF, F32) for _ in range(NSA_HEADS)),
            tuple(jnp.zeros((BLK, 1), F32) for _ in range(NSA_HEADS)),
            tuple(jnp.zeros((BLK, HEAD_DIM), F32) for _ in range(NSA_HEADS)))
    n_steps = (t0 + BLK - 1) // NSA_KC + 1
    _, ls, accs = lax.fori_loop(0, n_steps, slc_step, init)
    o_slc = [accs[h] / jnp.maximum(ls[h], 1e-30) for h in range(NSA_HEADS)]

    wspan = min(NSA_WSPAN, seq)
    wstart = pl.multiple_of(jnp.maximum(t0 + BLK - wspan, 0), BLK)
    kvw = kwvw_ref[pl.ds(wstart, wspan), :]
    kw = kvw[:, :HEAD_DIM]
    vw = kvw[:, HEAD_DIM:]
    wdist = tq - (wstart + lax.broadcasted_iota(jnp.int32, (1, wspan), 1))
    wmask = (wdist >= 0) & (wdist < NSA_WINDOW)
    wdistf = wdist.astype(F32)
    gts = jax.nn.sigmoid(gate_ref[...].astype(F32))
    for h in range(NSA_HEADS):
        p = _masked_softmax(_dot_nt(qh[h], kw) * SCALE - SLOPES_NSA[h] * wdistf, wmask)
        o_win = _dot(p.astype(BF16), vw)
        o = (gts[:, h:h + 1] * o_cmp[h] + gts[:, NSA_HEADS + h:NSA_HEADS + h + 1] * o_slc[h]
             + gts[:, 2 * NSA_HEADS + h:2 * NSA_HEADS + h + 1] * o_win)
        o_ref[:, h * HEAD_DIM:(h + 1) * HEAD_DIM] = o.astype(o_ref.dtype)


def _nsa(ua, kcmp, vcmp, bsz, seq):
    t = bsz * seq
    nq = seq // BLK
    ncmp = kcmp.shape[1]
    return pl.pallas_call(
        _nsa_kernel,
        grid=(bsz, nq),
        in_specs=[pl.BlockSpec((BLK, 256), lambda bb, i: (bb * nq + i, C_NSAQ // 256)),
                  pl.BlockSpec((1, ncmp, HEAD_DIM), lambda bb, i: (bb, 0, 0)),
                  pl.BlockSpec((1, ncmp, HEAD_DIM), lambda bb, i: (bb, 0, 0)),
                  pl.BlockSpec((seq, 128), lambda bb, i: (bb, C_NSAS // 128)),
                  pl.BlockSpec((seq, 128), lambda bb, i: (bb, C_NSAW // 128)),
                  pl.BlockSpec((BLK, 128), lambda bb, i: (bb * nq + i, C_NSAG // 128))],
        out_specs=pl.BlockSpec((BLK, 256), lambda bb, i: (bb * nq + i, 0)),
        out_shape=jax.ShapeDtypeStruct((t, NSA_HEADS * HEAD_DIM), BF16),
        scratch_shapes=[pltpu.VMEM((BLK, seq), F32)],
        compiler_params=_params("parallel", "parallel"),
        name="nsa_attention",
    )(ua, kcmp, vcmp, ua, ua, ua)


MERGE_TM = 256


def _merge_kernel(x_ref, conv_ref, swa_ref, ret_ref, nsa_ref, wm_ref, pc_ref, ps_ref, pr_ref, pn_ref,
                  wo_ref, g_ref, b_ref, o_ref, op_ref):
    x = x_ref[...]
    xb = x.astype(BF16)
    merged = None
    for n, (a_ref, p_ref) in enumerate(((conv_ref, pc_ref), (swa_ref, ps_ref), (ret_ref, pr_ref), (nsa_ref, pn_ref))):
        gate = jax.nn.sigmoid(_dot(xb, wm_ref[:, n * D_MODEL:(n + 1) * D_MODEL]))
        term = gate * _dot(a_ref[...], p_ref[...])
        merged = term if merged is None else merged + term
    mix = _dot(merged.astype(BF16), wo_ref[...])
    y = _layer_norm(ALPHA * x + mix, g_ref[...], b_ref[...])
    o_ref[...] = y
    op_ref[...] = _pack_halves(y)


def _merge(x2d, conv_a, swa_o, ret_o, nsa_o, w_m, p_conv, p_swa, p_ret, p_nsa, w_out, g, b):
    t = x2d.shape[0]
    tm = MERGE_TM

    def rows(width):
        return pl.BlockSpec((tm, width), lambda i: (i, 0))

    def whole(arr):
        return pl.BlockSpec(arr.shape, lambda i: (0, 0))

    g2, b2 = g.reshape(1, -1), b.reshape(1, -1)
    return pl.pallas_call(
        _merge_kernel,
        grid=(t // tm,),
        in_specs=[rows(D_MODEL), rows(256), rows(256), rows(512), rows(256),
                  whole(w_m), whole(p_conv), whole(p_swa), whole(p_ret), whole(p_nsa), whole(w_out),
                  whole(g2), whole(b2)],
        out_specs=[rows(D_MODEL), rows(D_MODEL // 2)],
        out_shape=[jax.ShapeDtypeStruct((t, D_MODEL), F32), jax.ShapeDtypeStruct((t, D_MODEL // 2), jnp.uint32)],
        compiler_params=_params("parallel"),
        name="merge_outproj_ln",
    )(x2d, conv_a, swa_o, ret_o, nsa_o, w_m, p_conv, p_swa, p_ret, p_nsa, w_out, g2, b2)


ROUTER_TR = 512


def _first_argmax(v, iota, size):
    m = jnp.max(v, axis=0, keepdims=True)
    idx = jnp.min(jnp.where(v == m, iota, size), axis=0, keepdims=True)
    return m, idx


def _router_kernel(x_ref, w_ref, b_ref, eidx_ref, gate_ref, pos_ref, cnt_ref, carry_ref):
    i = pl.program_id(0)
    tr = x_ref.shape[0]

    @pl.when(i == 0)
    def _():
        carry_ref[...] = jnp.zeros_like(carry_ref)

    x = x_ref[...]
    w = w_ref[...]
    xh = x.astype(BF16)
    xl = (x - xh.astype(F32)).astype(BF16)
    wh = w.astype(BF16)
    wl = (w - wh.astype(F32)).astype(BF16)
    logits = _dot_nt(wh, xh) + _dot_nt(wh, xl) + _dot_nt(wl, xh)
    scores = jax.nn.sigmoid(logits)
    biased = scores + b_ref[...]

    iota_g = lax.broadcasted_iota(jnp.int32, (GROUP_SIZE, tr), 0)
    grp = []
    for g in range(N_GROUPS):
        vg = biased[g * GROUP_SIZE:(g + 1) * GROUP_SIZE, :]
        m1, i1 = _first_argmax(vg, iota_g, GROUP_SIZE)
        m2 = jnp.max(jnp.where(iota_g == i1, -jnp.inf, vg), axis=0, keepdims=True)
        grp.append(m1 + m2)
    work = jnp.concatenate(grp, axis=0)
    iota_n = lax.broadcasted_iota(jnp.int32, (N_GROUPS, tr), 0)
    gsel = jnp.zeros((N_GROUPS, tr), F32)
    for _ in range(TOPK_GROUPS):
        _, gi = _first_argmax(work, iota_n, N_GROUPS)
        hit = iota_n == gi
        gsel = jnp.where(hit, 1.0, gsel)
        work = jnp.where(hit, -jnp.inf, work)
    emask = jnp.concatenate([jnp.broadcast_to(gsel[g:g + 1, :], (GROUP_SIZE, tr)) for g in range(N_GROUPS)], axis=0)
    work = jnp.where(emask > 0.5, biased, NEG_INF)

    iota_e = lax.broadcasted_iota(jnp.int32, (N_EXPERTS, tr), 0)
    hits, eids, gates = [], [], []
    chosen = jnp.zeros((N_EXPERTS, tr), F32)
    for _ in range(TOP_K):
        _, ei = _first_argmax(work, iota_e, N_EXPERTS)
        hit = iota_e == ei
        hits.append(hit)
        eids.append(ei)
        gates.append(jnp.sum(jnp.where(hit, scores, 0.0), axis=0, keepdims=True))
        chosen = jnp.where(hit, 1.0, chosen)
        work = jnp.where(hit, -jnp.inf, work)
    gsum = gates[0]
    for gk in gates[1:]:
        gsum = gsum + gk

    r = lax.broadcasted_iota(jnp.int32, (tr, tr), 0)
    c = lax.broadcasted_iota(jnp.int32, (tr, tr), 1)
    upper = jnp.where(r < c, 1.0, 0.0).astype(BF16)
    before = _dot(chosen.astype(BF16), upper) + carry_ref[...]
    zero_i = jnp.zeros((8 - TOP_K, tr), jnp.int32)
    eidx_ref[...] = jnp.concatenate(eids + [zero_i], axis=0)
    pos_ref[...] = jnp.concatenate(
        [jnp.sum(jnp.where(h, before, 0.0), axis=0, keepdims=True).astype(jnp.int32) for h in hits] + [zero_i], axis=0)
    gate_ref[...] = jnp.concatenate([gk / gsum * ROUTE_SCALE for gk in gates] + [jnp.zeros((8 - TOP_K, tr), F32)], axis=0)
    carry_ref[...] = carry_ref[...] + jnp.sum(chosen, axis=1, keepdims=True)
    cnt_ref[...] = jnp.broadcast_to(carry_ref[...], cnt_ref.shape)


def _router(x1, router_w, router_b):
    t = x1.shape[0]
    tr = ROUTER_TR
    col = pl.BlockSpec((8, tr), lambda i: (0, i))
    return pl.pallas_call(
        _router_kernel,
        grid=(t // tr,),
        in_specs=[pl.BlockSpec((tr, D_MODEL), lambda i: (i, 0)),
                  pl.BlockSpec((N_EXPERTS, D_MODEL), lambda i: (0, 0)),
                  pl.BlockSpec((N_EXPERTS, 1), lambda i: (0, 0))],
        out_specs=[col, col, col, pl.BlockSpec((N_EXPERTS, 128), lambda i: (0, 0))],
        out_shape=[jax.ShapeDtypeStruct((8, t), jnp.int32), jax.ShapeDtypeStruct((8, t), F32),
                   jax.ShapeDtypeStruct((8, t), jnp.int32), jax.ShapeDtypeStruct((N_EXPERTS, 128), F32)],
        scratch_shapes=[pltpu.VMEM((N_EXPERTS, 1), F32)],
        compiler_params=_params("arbitrary"),
        name="moe_router",
    )(x1, router_w, router_b.reshape(-1, 1))


DISPATCH_TD = 256


def _dispatch_kernel(dest_ref, x_hbm, buf_hbm, o_hbm, sem):
    del buf_hbm
    i = pl.program_id(0)
    base = i * DISPATCH_TD

    def issue(t, carry):
        for k in range(TOP_K):
            pltpu.make_async_copy(x_hbm.at[pl.ds(base + t, 1)], o_hbm.at[pl.ds(dest_ref[k, t], 1)], sem).start()
        return carry

    lax.fori_loop(0, DISPATCH_TD, issue, 0)

    def drain(t, carry):
        for k in range(TOP_K):
            pltpu.make_async_copy(x_hbm.at[pl.ds(0, 1)], o_hbm.at[pl.ds(0, 1)], sem).wait()
        return carry

    lax.fori_loop(0, DISPATCH_TD, drain, 0)


def _dispatch(dest, x1p, rows):
    t, half = x1p.shape
    td = DISPATCH_TD
    return pl.pallas_call(
        _dispatch_kernel,
        grid=(t // td,),
        in_specs=[pl.BlockSpec((8, td), lambda i: (0, i), memory_space=pltpu.SMEM),
                  pl.BlockSpec(memory_space=pl.ANY),
                  pl.BlockSpec(memory_space=pl.ANY)],
        out_specs=pl.BlockSpec(memory_space=pl.ANY),
        out_shape=jax.ShapeDtypeStruct((rows, half), jnp.uint32),
        scratch_shapes=[pltpu.SemaphoreType.DMA],
        input_output_aliases={2: 0},
        compiler_params=_params("arbitrary"),
        name="moe_dispatch",
    )(dest, x1p, jnp.zeros((rows, half), jnp.uint32))


def _expert_kernel(be_ref, nu_ref, x_ref, wg_ref, wu_ref, wd_ref, o_ref):
    r = pl.program_id(0)

    @pl.when(r < nu_ref[0])
    def _():
        xb = _unpack_halves(x_ref[...]).astype(BF16)
        h = _silu(_dot(xb, wg_ref[0])) * _dot(xb, wu_ref[0])
        o_ref[...] = _pack_halves(_dot(h.astype(BF16), wd_ref[0]))

    @pl.when(r >= nu_ref[0])
    def _():
        o_ref[...] = jnp.zeros_like(o_ref)


def _experts(blk_exp, n_used, xs, wg, wu, wd):
    rows, half = xs.shape
    nblk = rows // MOE_BLK
    grid_spec = pltpu.PrefetchScalarGridSpec(
        num_scalar_prefetch=2,
        grid=(nblk,),
        in_specs=[pl.BlockSpec((MOE_BLK, half), lambda r, be, nu: (r, 0)),
                  pl.BlockSpec((1, D_MODEL, D_EXPERT), lambda r, be, nu: (be[r], 0, 0)),
                  pl.BlockSpec((1, D_MODEL, D_EXPERT), lambda r, be, nu: (be[r], 0, 0)),
                  pl.BlockSpec((1, D_EXPERT, D_MODEL), lambda r, be, nu: (be[r], 0, 0))],
        out_specs=pl.BlockSpec((MOE_BLK, half), lambda r, be, nu: (r, 0)),
    )
    return pl.pallas_call(
        _expert_kernel,
        grid_spec=grid_spec,
        out_shape=jax.ShapeDtypeStruct((rows, half), jnp.uint32),
        compiler_params=_params("arbitrary"),
        name="moe_experts",
    )(blk_exp, n_used, xs, wg, wu, wd)


COMBINE_TC = 256


def _combine_kernel(dest_ref, x_ref, gate_ref, y_hbm, swg_ref, swu_ref, swd_ref, g_ref, b_ref, o_ref, buf_ref, sem):
    tc = x_ref.shape[0]

    def issue(t, carry):
        for k in range(TOP_K):
            pltpu.make_async_copy(y_hbm.at[pl.ds(dest_ref[k, t], 1)], buf_ref.at[k, pl.ds(t, 1)], sem).start()
        return carry

    lax.fori_loop(0, tc, issue, 0)

    x = x_ref[...]
    xb = x.astype(BF16)
    h = _silu(_dot(xb, swg_ref[...])) * _dot(xb, swu_ref[...])
    acc = ALPHA * x + _dot(h.astype(BF16), swd_ref[...])

    def drain(t, carry):
        for k in range(TOP_K):
            pltpu.make_async_copy(y_hbm.at[pl.ds(0, 1)], buf_ref.at[k, pl.ds(0, 1)], sem).wait()
        return carry

    lax.fori_loop(0, tc, drain, 0)

    gates = gate_ref[...]
    for k in range(TOP_K):
        acc = acc + gates[:, k:k + 1] * _unpack_halves(buf_ref[k])
    o_ref[...] = _layer_norm(acc, g_ref[...], b_ref[...])


def _combine(dest, x1, gates_t, y_rows, swg, swu, swd, g, b):
    t = x1.shape[0]
    tc = COMBINE_TC
    half = y_rows.shape[1]

    def whole(arr):
        return pl.BlockSpec(arr.shape, lambda i: (0, 0))

    g2, b2 = g.reshape(1, -1), b.reshape(1, -1)
    return pl.pallas_call(
        _combine_kernel,
        grid=(t // tc,),
        in_specs=[pl.BlockSpec((8, tc), lambda i: (0, i), memory_space=pltpu.SMEM),
                  pl.BlockSpec((tc, D_MODEL), lambda i: (i, 0)),
                  pl.BlockSpec((tc, 8), lambda i: (i, 0)),
                  pl.BlockSpec(memory_space=pl.ANY),
                  whole(swg), whole(swu), whole(swd), whole(g2), whole(b2)],
        out_specs=pl.BlockSpec((tc, D_MODEL), lambda i: (i, 0)),
        out_shape=jax.ShapeDtypeStruct((t, D_MODEL), F32),
        scratch_shapes=[pltpu.VMEM((TOP_K, tc, half), jnp.uint32), pltpu.SemaphoreType.DMA],
        compiler_params=_params("arbitrary"),
        name="moe_combine_shared_ln",
    )(dest, x1, gates_t, y_rows, swg, swu, swd, g2, b2)


def _moe(x1, x1p, router_w, router_b, wg, wu, wd, swg, swu, swd, g, b):
    t = x1.shape[0]
    eidx, gates, pos, cnt = _router(x1, router_w, router_b)
    counts = cnt[:, 0].astype(jnp.int32)
    padded = (counts + MOE_BLK - 1) // MOE_BLK * MOE_BLK
    pend = jnp.cumsum(padded)
    pstart = pend - padded
    n_blocks = (t * TOP_K + N_EXPERTS * (MOE_BLK - 1) + MOE_BLK - 1) // MOE_BLK
    blk_exp = jnp.minimum(jnp.searchsorted(pend, jnp.arange(n_blocks, dtype=jnp.int32) * MOE_BLK, side='right'),
                          N_EXPERTS - 1).astype(jnp.int32)
    n_used = (pend[-1:] // MOE_BLK).astype(jnp.int32)
    dest = pstart[eidx] + pos
    xs = _dispatch(dest, x1p, n_blocks * MOE_BLK)
    y_rows = _experts(blk_exp, n_used, xs, wg, wu, wd)
    return _combine(dest, x1, gates.T, y_rows, swg, swu, swd, g, b)


def _layer(x2d, bsz, seq, w_in, conv_dw, conv_db, conv_ln_g, conv_ln_b, swa_sink, ret_gn_g, ret_gn_b,
           nsa_pe_k, nsa_w1_k, nsa_w2_k, nsa_pe_v, nsa_w1_v, nsa_w2_v, p_conv, p_swa, p_ret, p_nsa, w_out,
           ln1_g, ln1_b, router_w, router_b, moe_wg, moe_wu, moe_wd, sh_wg, sh_wu, sh_wd, ln2_g, ln2_b):
    bf = lambda a: a.astype(BF16)
    w_a = jnp.pad(bf(w_in[:, :N_MIX_COLS]), ((0, 0), (0, UA - N_MIX_COLS)))
    w_m = bf(w_in[:, N_MIX_COLS:])
    ua = _inproj(x2d, w_a)
    conv_a = _conv(ua, conv_dw, conv_db, conv_ln_g, conv_ln_b, bsz, seq)
    swa_o = _swa(ua, swa_sink, bsz, seq)
    ret_o = _retention(ua, ret_gn_g, ret_gn_b, bsz, seq)
    ng = seq // NSA_CMP_STRIDE
    tk = ua[:, C_NSAC:C_NSAC + HEAD_DIM].reshape(bsz, ng, NSA_CMP_STRIDE * HEAD_DIM)
    tv = ua[:, C_NSAC + HEAD_DIM:C_NSAC + 2 * HEAD_DIM].reshape(bsz, ng, NSA_CMP_STRIDE * HEAD_DIM)
    kcmp, vcmp = _compress(tk, tv, nsa_pe_k, bf(nsa_w1_k), bf(nsa_w2_k), nsa_pe_v, bf(nsa_w1_v), bf(nsa_w2_v))
    nsa_o = _nsa(ua, kcmp, vcmp, bsz, seq)
    x1, x1p = _merge(x2d, conv_a, swa_o, ret_o, nsa_o, w_m, bf(p_conv), bf(p_swa), bf(p_ret), bf(p_nsa),
                     bf(w_out), ln1_g, ln1_b)
    return _moe(x1, x1p, router_w, router_b, bf(moe_wg), bf(moe_wu), bf(moe_wd),
                bf(sh_wg), bf(sh_wu), bf(sh_wd), ln2_g, ln2_b)


def kernel(x, w_in, conv_dw, conv_db, conv_ln_g, conv_ln_b, swa_sink, ret_gn_g, ret_gn_b, nsa_pe_k, nsa_w1_k, nsa_w2_k, nsa_pe_v, nsa_w1_v, nsa_w2_v, p_conv, p_swa, p_ret, p_nsa, w_out, ln1_g, ln1_b, router_w, router_b, moe_wg, moe_wu, moe_wd, sh_wg, sh_wu, sh_wd, ln2_g, ln2_b):
    bsz, seq, d = x.shape
    params = (w_in, conv_dw, conv_db, conv_ln_g, conv_ln_b, swa_sink, ret_gn_g, ret_gn_b,
              nsa_pe_k, nsa_w1_k, nsa_w2_k, nsa_pe_v, nsa_w1_v, nsa_w2_v, p_conv, p_swa, p_ret, p_nsa, w_out,
              ln1_g, ln1_b, router_w, router_b, moe_wg, moe_wu, moe_wd, sh_wg, sh_wu, sh_wd, ln2_g, ln2_b)
    x2d = x.reshape(bsz * seq, d)
    for l in range(w_in.shape[0]):
        x2d = _layer(x2d, bsz, seq, *[p[l] for p in params])
    return x2d.reshape(bsz, seq, d)
```

```python
import functools
import math

import jax
import jax.numpy as jnp
import numpy as np
from jax import lax
from jax.experimental import pallas as pl
from jax.experimental.pallas import tpu as pltpu

F32 = jnp.float32
BF16 = jnp.bfloat16

D_MODEL = 1024
DEPTH = 2
HEAD_DIM = 64
BLK = 128
CONV_CH = 256
CONV_K = 31
SWA_HEADS = 4
SWA_KV_HEADS = 2
SWA_WINDOW = 128
RET_HEADS = 4
RET_QK = 64
RET_V = 128
NSA_HEADS = 4
NSA_CMP_LEN = 32
NSA_CMP_STRIDE = 16
NSA_CMP_HIDDEN = 256
NSA_SLC_LEN = 64
NSA_TOPN = 8
NSA_WINDOW = 512
NSA_FORCE = 1e4
N_EXPERTS = 64
TOP_K = 6
N_GROUPS = 8
GROUP_SIZE = N_EXPERTS // N_GROUPS
TOPK_GROUPS = 4
D_EXPERT = 256
ROUTE_SCALE = 2.5
MOE_BLK = 128
N_BRANCH = 4
ALPHA = (2 * DEPTH) ** 0.25
LN_EPS = 1e-5
NEG_INF = -1e30
SCALE = HEAD_DIM ** -0.5

C_CONV = 0
C_SWA = 512
C_RETQK = 1024
C_RETV = 1536
C_RETG = 2048
C_NSAQ = 2560
C_NSAC = 2816
C_NSAS = 2944
C_NSAW = 3072
C_NSAG = 3200
N_MIX_COLS = 3212
UA = 3328

ALIBI = [2.0 ** (-8.0 * (i + 1) / 8) for i in range(8)]
SLOPES_SWA = ALIBI[:4]
SLOPES_NSA = ALIBI[4:]
RET_LOG_GAMMA = [math.log1p(-(2.0 ** (-5.0 - h))) for h in range(RET_HEADS)]

VMEM_LIMIT = 48 * 1024 * 1024
NT_DIMS = (((1,), (1,)), ((), ()))
TN_DIMS = (((0,), (0,)), ((), ()))


def _params(*sem):
    return pltpu.CompilerParams(dimension_semantics=sem, vmem_limit_bytes=VMEM_LIMIT)


def _dot(a, b):
    return jnp.dot(a, b, preferred_element_type=F32)


def _dot_nt(a, b):
    return lax.dot_general(a, b, NT_DIMS, preferred_element_type=F32)


def _layer_norm(v, g, b):
    mu = jnp.mean(v, axis=-1, keepdims=True)
    c = v - mu
    var = jnp.mean(c * c, axis=-1, keepdims=True)
    return c * lax.rsqrt(var + LN_EPS) * g + b


def _silu(v):
    return v * jax.nn.sigmoid(v)


def _pack_halves(v):
    n = v.shape[1] // 2
    r = v.astype(BF16).astype(F32)
    lo = lax.shift_right_logical(lax.bitcast_convert_type(r[:, :n], jnp.uint32), jnp.uint32(16))
    hi = lax.bitcast_convert_type(r[:, n:], jnp.uint32) & jnp.uint32(0xFFFF0000)
    return lo | hi


def _unpack_halves(w):
    lo = lax.bitcast_convert_type(lax.shift_left(w, jnp.uint32(16)), F32)
    hi = lax.bitcast_convert_type(w & jnp.uint32(0xFFFF0000), F32)
    return jnp.concatenate([lo, hi], axis=1)


INPROJ_TM = 512


def _inproj_kernel(x_ref, w_ref, o_ref, oc_ref):
    xb = x_ref[...].astype(BF16)
    for c in range(0, UA, 256):
        o_ref[:, c:c + 256] = _dot(xb, w_ref[:, c:c + 256]).astype(o_ref.dtype)
    oc_ref[...] = _dot(xb, w_ref[:, C_NSAC:C_NSAC + 128])


def _inproj(x2d, w_a):
    t = x2d.shape[0]
    return pl.pallas_call(
        _inproj_kernel,
        grid=(t // INPROJ_TM,),
        in_specs=[pl.BlockSpec((INPROJ_TM, D_MODEL), lambda i: (i, 0)),
                  pl.BlockSpec((D_MODEL, UA), lambda i: (0, 0))],
        out_specs=[pl.BlockSpec((INPROJ_TM, UA), lambda i: (i, 0)),
                   pl.BlockSpec((INPROJ_TM, 128), lambda i: (i, 0))],
        out_shape=[jax.ShapeDtypeStruct((t, UA), BF16), jax.ShapeDtypeStruct((t, 128), F32)],
        compiler_params=_params("parallel"),
        name="inproj",
    )(x2d, w_a)


CONV_TS = 256
CONV_HALO = 32


def _conv_kernel(cur_ref, halo_ref, dw_ref, db_ref, g_ref, b_ref, o_ref, hc_ref):
    i = pl.program_id(1)

    def glu(u):
        u = u.astype(F32)
        return u[:, :CONV_CH] * jax.nn.sigmoid(u[:, CONV_CH:])

    hc_ref[0:CONV_HALO, :] = jnp.where(i > 0, glu(halo_ref[...]), 0.0)
    hc_ref[CONV_HALO:CONV_HALO + CONV_TS, :] = glu(cur_ref[...])
    acc = jnp.broadcast_to(db_ref[...], (CONV_TS, CONV_CH))
    off = CONV_HALO - (CONV_K - 1)
    for j in range(CONV_K):
        acc = acc + hc_ref[off + j:off + j + CONV_TS, :] * dw_ref[j:j + 1, :]
    o_ref[...] = _silu(_layer_norm(acc, g_ref[...], b_ref[...])).astype(o_ref.dtype)


def _conv(ua, dw, db, g, b, bsz, seq):
    t = bsz * seq
    ns = seq // CONV_TS
    per = CONV_TS // CONV_HALO
    dwp = jnp.concatenate([dw, jnp.zeros((32 - CONV_K, CONV_CH), F32)], axis=0)
    vec = pl.BlockSpec((1, CONV_CH), lambda bb, i: (0, 0))
    return pl.pallas_call(
        _conv_kernel,
        grid=(bsz, ns),
        in_specs=[pl.BlockSpec((CONV_TS, 512), lambda bb, i: (bb * ns + i, C_CONV // 512)),
                  pl.BlockSpec((CONV_HALO, 512),
                               lambda bb, i: (jnp.maximum((bb * ns + i) * per - 1, 0), C_CONV // 512)),
                  pl.BlockSpec((32, CONV_CH), lambda bb, i: (0, 0)), vec, vec, vec],
        out_specs=pl.BlockSpec((CONV_TS, CONV_CH), lambda bb, i: (bb * ns + i, 0)),
        out_shape=jax.ShapeDtypeStruct((t, CONV_CH), BF16),
        scratch_shapes=[pltpu.VMEM((CONV_HALO + CONV_TS, CONV_CH), F32)],
        compiler_params=_params("parallel", "parallel"),
        name="conformer_conv",
    )(ua, ua, dwp, db.reshape(1, -1), g.reshape(1, -1), b.reshape(1, -1))


def _swa_kernel(sink_ref, q_ref, kvc_ref, kvp_ref, o_ref):
    i = pl.program_id(1)
    q = q_ref[...]
    kvc = kvc_ref[...]
    kvp = kvp_ref[...]
    a = lax.broadcasted_iota(jnp.int32, (BLK, 2 * BLK), 0)
    c = lax.broadcasted_iota(jnp.int32, (BLK, 2 * BLK), 1)
    dist = BLK + a - c
    mask = (dist >= 0) & (dist < SWA_WINDOW) & ((c >= BLK) | (i > 0))
    distf = dist.astype(F32)
    rep = SWA_HEADS // SWA_KV_HEADS
    for h in range(SWA_HEADS):
        g = h // rep
        k = jnp.concatenate([kvp[:, g * HEAD_DIM:(g + 1) * HEAD_DIM],
                             kvc[:, g * HEAD_DIM:(g + 1) * HEAD_DIM]], axis=0)
        v = jnp.concatenate([kvp[:, BLK + g * HEAD_DIM:BLK + (g + 1) * HEAD_DIM],
                             kvc[:, BLK + g * HEAD_DIM:BLK + (g + 1) * HEAD_DIM]], axis=0)
        s = _dot_nt(q[:, h * HEAD_DIM:(h + 1) * HEAD_DIM], k) * SCALE - SLOPES_SWA[h] * distf
        s = jnp.where(mask, s, NEG_INF)
        sink = sink_ref[h]
        m = jnp.maximum(jnp.max(s, axis=-1, keepdims=True), sink)
        e = jnp.where(mask, jnp.exp(s - m), 0.0)
        den = jnp.sum(e, axis=-1, keepdims=True) + jnp.exp(sink - m)
        p = e / jnp.maximum(den, 1e-30)
        o_ref[:, h * HEAD_DIM:(h + 1) * HEAD_DIM] = _dot(p.astype(BF16), v).astype(o_ref.dtype)


def _swa(ua, sink, bsz, seq):
    t = bsz * seq
    nb = seq // BLK
    return pl.pallas_call(
        _swa_kernel,
        grid=(bsz, nb),
        in_specs=[pl.BlockSpec(memory_space=pltpu.SMEM),
                  pl.BlockSpec((BLK, 256), lambda bb, i: (bb * nb + i, C_SWA // 256)),
                  pl.BlockSpec((BLK, 256), lambda bb, i: (bb * nb + i, C_SWA // 256 + 1)),
                  pl.BlockSpec((BLK, 256), lambda bb, i: (jnp.maximum(bb * nb + i - 1, 0), C_SWA // 256 + 1))],
        out_specs=pl.BlockSpec((BLK, 256), lambda bb, i: (bb * nb + i, 0)),
        out_shape=jax.ShapeDtypeStruct((t, SWA_HEADS * HEAD_DIM), BF16),
        compiler_params=_params("parallel", "parallel"),
        name="swa",
    )(sink, ua, ua, ua)


def _ret_kernel(qk_ref, v_ref, g_ref, gng_ref, gnb_ref, o_ref, state_ref):
    n = pl.program_id(1)

    @pl.when(n == 0)
    def _():
        state_ref[...] = jnp.zeros_like(state_ref)

    qk = qk_ref[...]
    vv = v_ref[...]
    gate = g_ref[...].astype(F32)
    a = lax.broadcasted_iota(jnp.int32, (BLK, BLK), 0)
    c = lax.broadcasted_iota(jnp.int32, (BLK, BLK), 1)
    diff = (a - c).astype(F32)
    idx = lax.broadcasted_iota(jnp.int32, (BLK, 1), 0).astype(F32)
    hq = RET_HEADS * RET_QK
    for h in range(RET_HEADS):
        lg = RET_LOG_GAMMA[h]
        q = qk[:, h * RET_QK:(h + 1) * RET_QK]
        k = qk[:, hq + h * RET_QK:hq + (h + 1) * RET_QK].astype(F32) * (RET_QK ** -0.5)
        v = vv[:, h * RET_V:(h + 1) * RET_V]
        dmat = jnp.where(diff >= 0, jnp.exp(lg * jnp.maximum(diff, 0.0)), 0.0)
        inner = _dot_nt(q, k.astype(BF16)) * dmat
        o_inner = _dot(inner.astype(BF16), v)
        zeta = jnp.exp(lg * (BLK - 1 - idx))
        xi = jnp.exp(lg * (idx + 1.0))
        kv = lax.dot_general((k * zeta).astype(BF16), v, TN_DIMS, preferred_element_type=F32)
        prev = state_ref[h]
        o = o_inner + _dot(q, prev.astype(BF16)) * xi
        state_ref[h] = math.exp(lg * BLK) * prev + kv
        mu = jnp.mean(o, axis=-1, keepdims=True)
        cen = o - mu
        var = jnp.mean(cen * cen, axis=-1, keepdims=True)
        sl = slice(h * RET_V, (h + 1) * RET_V)
        on = cen * lax.rsqrt(var + LN_EPS) * gng_ref[:, sl] + gnb_ref[:, sl]
        o_ref[:, sl] = (_silu(gate[:, sl]) * on).astype(o_ref.dtype)


def _retention(ua, gn_g, gn_b, bsz, seq):
    t = bsz * seq
    nc = seq // BLK
    vec = pl.BlockSpec((1, RET_HEADS * RET_V), lambda bb, i: (0, 0))
    return pl.pallas_call(
        _ret_kernel,
        grid=(bsz, nc),
        in_specs=[pl.BlockSpec((BLK, 512), lambda bb, i: (bb * nc + i, C_RETQK // 512)),
                  pl.BlockSpec((BLK, 512), lambda bb, i: (bb * nc + i, C_RETV // 512)),
                  pl.BlockSpec((BLK, 512), lambda bb, i: (bb * nc + i, C_RETG // 512)),
                  vec, vec],
        out_specs=pl.BlockSpec((BLK, 512), lambda bb, i: (bb * nc + i, 0)),
        out_shape=jax.ShapeDtypeStruct((t, RET_HEADS * RET_V), BF16),
        scratch_shapes=[pltpu.VMEM((RET_HEADS, RET_QK, RET_V), F32)],
        compiler_params=_params("parallel", "arbitrary"),
        name="retention",
    )(ua, ua, ua, gn_g.reshape(1, -1), gn_b.reshape(1, -1))


def _compress_kernel(t_ref, pek_ref, w1k_ref, w2k_ref, pev_ref, w1v_ref, w2v_ref, ok_ref, ov_ref):
    st = NSA_CMP_STRIDE
    ng = t_ref.shape[0] // st
    branches = ((pek_ref, w1k_ref, w2k_ref, ok_ref), (pev_ref, w1v_ref, w2v_ref, ov_ref))
    first = [jnp.zeros((ng, NSA_CMP_HIDDEN), F32) for _ in branches]
    second = [jnp.zeros((ng, NSA_CMP_HIDDEN), F32) for _ in branches]
    for r in range(st):
        rows = t_ref[pl.ds(r, ng, stride=st), :]
        for n, (pe_ref, w1_ref, _, _) in enumerate(branches):
            v = rows[:, n * HEAD_DIM:(n + 1) * HEAD_DIM]
            lo = (v + pe_ref[r:r + 1, :]).astype(BF16)
            hi = (v + pe_ref[st + r:st + r + 1, :]).astype(BF16)
            first[n] = first[n] + _dot(lo, w1_ref[r * HEAD_DIM:(r + 1) * HEAD_DIM, :])
            second[n] = second[n] + _dot(hi, w1_ref[(st + r) * HEAD_DIM:(st + r + 1) * HEAD_DIM, :])
    for n, (_, _, w2_ref, o_ref) in enumerate(branches):
        hid = first[n] + pltpu.roll(second[n], ng - 1, axis=0)
        act = 0.5 * hid * (1.0 + lax.erf(hid * (2.0 ** -0.5)))
        o_ref[0] = _dot(act.astype(BF16), w2_ref[...]).astype(o_ref.dtype)


def _compress(kcvc, pe_k, w1_k, w2_k, pe_v, w1_v, w2_v, bsz, seq):
    ng = seq // NSA_CMP_STRIDE
    pes = pl.BlockSpec((NSA_CMP_LEN, HEAD_DIM), lambda bb: (0, 0))
    w1s = pl.BlockSpec((NSA_CMP_LEN * HEAD_DIM, NSA_CMP_HIDDEN), lambda bb: (0, 0))
    w2s = pl.BlockSpec((NSA_CMP_HIDDEN, HEAD_DIM), lambda bb: (0, 0))
    osp = pl.BlockSpec((1, ng, HEAD_DIM), lambda bb: (bb, 0, 0))
    return pl.pallas_call(
        _compress_kernel,
        grid=(bsz,),
        in_specs=[pl.BlockSpec((seq, 128), lambda bb: (bb, 0)), pes, w1s, w2s, pes, w1s, w2s],
        out_specs=[osp, osp],
        out_shape=[jax.ShapeDtypeStruct((bsz, ng, HEAD_DIM), BF16)] * 2,
        compiler_params=_params("parallel"),
        name="nsa_compress",
    )(kcvc, pe_k, w1_k, w2_k, pe_v, w1_v, w2_v)


NSA_KC = 512
NSA_WSPAN = NSA_WINDOW + BLK


def _masked_softmax(s, mask):
    s = jnp.where(mask, s, NEG_INF)
    m = jnp.max(s, axis=-1, keepdims=True)
    e = jnp.where(mask, jnp.exp(s - m), 0.0)
    return e / jnp.maximum(jnp.sum(e, axis=-1, keepdims=True), 1e-30)


def _nsa_kernel(q_ref, kcmp_ref, vcmp_ref, ksvs_ref, kwvw_ref, gate_ref, o_ref, selk_ref):
    i = pl.program_id(1)
    seq = ksvs_ref.shape[0]
    ncmp = kcmp_ref.shape[1]
    nslc = seq // NSA_SLC_LEN
    q = q_ref[...]
    qh = [q[:, h * HEAD_DIM:(h + 1) * HEAD_DIM] for h in range(NSA_HEADS)]
    t0 = i * BLK
    tq = t0 + lax.broadcasted_iota(jnp.int32, (BLK, 1), 0)

    kcmp = kcmp_ref[0]
    vcmp = vcmp_ref[0]
    cend = lax.broadcasted_iota(jnp.int32, (1, ncmp), 1) * NSA_CMP_STRIDE + (NSA_CMP_LEN - 1)
    cdist = tq - cend
    cmask = cdist >= 0
    cdistf = cdist.astype(F32)
    o_cmp = []
    psum = jnp.zeros((BLK, ncmp), F32)
    for h in range(NSA_HEADS):
        p = _masked_softmax(_dot_nt(qh[h], kcmp) * SCALE - SLOPES_NSA[h] * cdistf, cmask)
        psum = psum + p
        o_cmp.append(_dot(p.astype(BF16), vcmp))

    cst = lax.broadcasted_iota(jnp.int32, (ncmp, nslc), 0) * NSA_CMP_STRIDE
    sst = lax.broadcasted_iota(jnp.int32, (ncmp, nslc), 1) * NSA_SLC_LEN
    overlap = jnp.where((cst < sst + NSA_SLC_LEN) & (cst + NSA_CMP_LEN > sst), 1.0, 0.0).astype(BF16)
    p_hi = psum.astype(BF16)
    p_lo = (psum - p_hi.astype(F32)).astype(BF16)
    imp = _dot(p_hi, overlap) + _dot(p_lo, overlap)
    jj = lax.broadcasted_iota(jnp.int32, (BLK, nslc), 1)
    cur = tq // NSA_SLC_LEN
    forced = (jj == 0) | (jj == cur) | (jj == cur - 1)
    imp = jnp.where(forced, NSA_FORCE, jnp.where(jj > cur, -1.0, imp))

    sel = jnp.zeros((BLK, nslc), F32)
    for _ in range(min(NSA_TOPN, nslc)):
        mx = jnp.max(imp, axis=-1, keepdims=True)
        first = jnp.min(jnp.where(imp == mx, jj, nslc), axis=-1, keepdims=True)
        hit = jj == first
        sel = jnp.where(hit, 1.0, sel)
        imp = jnp.where(hit, -3e38, imp)
    kblk = lax.broadcasted_iota(jnp.int32, (nslc, seq), 1) // NSA_SLC_LEN
    expand = jnp.where(kblk == lax.broadcasted_iota(jnp.int32, (nslc, seq), 0), 1.0, 0.0).astype(BF16)
    selk_ref[...] = _dot(sel.astype(BF16), expand)

    def slc_step(c, carry):
        ms, ls, accs = carry
        start = pl.multiple_of(c * NSA_KC, NSA_KC)
        kv = ksvs_ref[pl.ds(start, NSA_KC), :]
        ks = kv[:, :HEAD_DIM]
        vs = kv[:, HEAD_DIM:]
        kpos = start + lax.broadcasted_iota(jnp.int32, (1, NSA_KC), 1)
        dist = tq - kpos
        mask = (selk_ref[:, pl.ds(start, NSA_KC)] > 0.5) & (dist >= 0)
        distf = dist.astype(F32)
        nm, nl, na = [], [], []
        for h in range(NSA_HEADS):
            s = _dot_nt(qh[h], ks) * SCALE - SLOPES_NSA[h] * distf
            s = jnp.where(mask, s, NEG_INF)
            m_new = jnp.maximum(ms[h], jnp.max(s, axis=-1, keepdims=True))
            alpha = jnp.exp(ms[h] - m_new)
            p = jnp.where(mask, jnp.exp(s - m_new), 0.0)
            nm.append(m_new)
            nl.append(alpha * ls[h] + jnp.sum(p, axis=-1, keepdims=True))
            na.append(alpha * accs[h] + _dot(p.astype(BF16), vs))
        return tuple(nm), tuple(nl), tuple(na)

    init = (tuple(jnp.full((BLK, 1), NEG_INF, F32) for _ in range(NSA_HEADS)),
            tuple(jnp.zeros((BLK, 1), F32) for _ in range(NSA_HEADS)),
            tuple(jnp.zeros((BLK, HEAD_DIM), F32) for _ in range(NSA_HEADS)))
    n_steps = (t0 + BLK - 1) // NSA_KC + 1
    _, ls, accs = lax.fori_loop(0, n_steps, slc_step, init)
    o_slc = [accs[h] / jnp.maximum(ls[h], 1e-30) for h in range(NSA_HEADS)]

    wspan = min(NSA_WSPAN, seq)
    wstart = pl.multiple_of(jnp.maximum(t0 + BLK - wspan, 0), BLK)
    kvw = kwvw_ref[pl.ds(wstart, wspan), :]
    kw = kvw[:, :HEAD_DIM]
    vw = kvw[:, HEAD_DIM:]
    wdist = tq - (wstart + lax.broadcasted_iota(jnp.int32, (1, wspan), 1))
    wmask = (wdist >= 0) & (wdist < NSA_WINDOW)
    wdistf = wdist.astype(F32)
    gts = jax.nn.sigmoid(gate_ref[...].astype(F32))
    for h in range(NSA_HEADS):
        p = _masked_softmax(_dot_nt(qh[h], kw) * SCALE - SLOPES_NSA[h] * wdistf, wmask)
        o_win = _dot(p.astype(BF16), vw)
        o = (gts[:, h:h + 1] * o_cmp[h] + gts[:, NSA_HEADS + h:NSA_HEADS + h + 1] * o_slc[h]
             + gts[:, 2 * NSA_HEADS + h:2 * NSA_HEADS + h + 1] * o_win)
        o_ref[:, h * HEAD_DIM:(h + 1) * HEAD_DIM] = o.astype(o_ref.dtype)


def _nsa(ua, kcmp, vcmp, bsz, seq):
    t = bsz * seq
    nq = seq // BLK
    ncmp = kcmp.shape[1]
    return pl.pallas_call(
        _nsa_kernel,
        grid=(bsz, nq),
        in_specs=[pl.BlockSpec((BLK, 256), lambda bb, i: (bb * nq + i, C_NSAQ // 256)),
                  pl.BlockSpec((1, ncmp, HEAD_DIM), lambda bb, i: (bb, 0, 0)),
                  pl.BlockSpec((1, ncmp, HEAD_DIM), lambda bb, i: (bb, 0, 0)),
                  pl.BlockSpec((seq, 128), lambda bb, i: (bb, C_NSAS // 128)),
                  pl.BlockSpec((seq, 128), lambda bb, i: (bb, C_NSAW // 128)),
                  pl.BlockSpec((BLK, 128), lambda bb, i: (bb * nq + i, C_NSAG // 128))],
        out_specs=pl.BlockSpec((BLK, 256), lambda bb, i: (bb * nq + i, 0)),
        out_shape=jax.ShapeDtypeStruct((t, NSA_HEADS * HEAD_DIM), BF16),
        scratch_shapes=[pltpu.VMEM((BLK, seq), F32)],
        compiler_params=_params("parallel", "parallel"),
        name="nsa_attention",
    )(ua, kcmp, vcmp, ua, ua, ua)


MERGE_TM = 256


def _merge_kernel(x_ref, conv_ref, swa_ref, ret_ref, nsa_ref, wm_ref, pc_ref, ps_ref, pr_ref, pn_ref,
                  wo_ref, g_ref, b_ref, o_ref, op_ref):
    x = x_ref[...]
    xb = x.astype(BF16)
    merged = None
    for n, (a_ref, p_ref) in enumerate(((conv_ref, pc_ref), (swa_ref, ps_ref), (ret_ref, pr_ref), (nsa_ref, pn_ref))):
        gate = jax.nn.sigmoid(_dot(xb, wm_ref[:, n * D_MODEL:(n + 1) * D_MODEL]))
        term = gate * _dot(a_ref[...], p_ref[...])
        merged = term if merged is None else merged + term
    mix = _dot(merged.astype(BF16), wo_ref[...])
    y = _layer_norm(ALPHA * x + mix, g_ref[...], b_ref[...])
    o_ref[...] = y
    op_ref[...] = _pack_halves(y)


def _merge(x2d, conv_a, swa_o, ret_o, nsa_o, w_m, p_conv, p_swa, p_ret, p_nsa, w_out, g, b):
    t = x2d.shape[0]
    tm = MERGE_TM

    def rows(width):
        return pl.BlockSpec((tm, width), lambda i: (i, 0))

    def whole(arr):
        return pl.BlockSpec(arr.shape, lambda i: (0, 0))

    g2, b2 = g.reshape(1, -1), b.reshape(1, -1)
    return pl.pallas_call(
        _merge_kernel,
        grid=(t // tm,),
        in_specs=[rows(D_MODEL), rows(256), rows(256), rows(512), rows(256),
                  whole(w_m), whole(p_conv), whole(p_swa), whole(p_ret), whole(p_nsa), whole(w_out),
                  whole(g2), whole(b2)],
        out_specs=[rows(D_MODEL), rows(D_MODEL // 2)],
        out_shape=[jax.ShapeDtypeStruct((t, D_MODEL), F32), jax.ShapeDtypeStruct((t, D_MODEL // 2), jnp.uint32)],
        compiler_params=_params("parallel"),
        name="merge_outproj_ln",
    )(x2d, conv_a, swa_o, ret_o, nsa_o, w_m, p_conv, p_swa, p_ret, p_nsa, w_out, g2, b2)


ROUTER_TR = 512


def _first_argmax(v, iota, size):
    m = jnp.max(v, axis=0, keepdims=True)
    idx = jnp.min(jnp.where(v == m, iota, size), axis=0, keepdims=True)
    return m, idx


def _router_kernel(x_ref, w_ref, b_ref, eidx_ref, gate_ref, pos_ref, cnt_ref, carry_ref):
    i = pl.program_id(0)
    tr = x_ref.shape[0]

    @pl.when(i == 0)
    def _():
        carry_ref[...] = jnp.zeros_like(carry_ref)

    x = x_ref[...]
    w = w_ref[...]
    xh = x.astype(BF16)
    xl = (x - xh.astype(F32)).astype(BF16)
    wh = w.astype(BF16)
    wl = (w - wh.astype(F32)).astype(BF16)
    logits = _dot_nt(wh, xh) + _dot_nt(wh, xl) + _dot_nt(wl, xh)
    scores = jax.nn.sigmoid(logits)
    biased = scores + b_ref[...]

    iota_g = lax.broadcasted_iota(jnp.int32, (GROUP_SIZE, tr), 0)
    grp = []
    for g in range(N_GROUPS):
        vg = biased[g * GROUP_SIZE:(g + 1) * GROUP_SIZE, :]
        m1, i1 = _first_argmax(vg, iota_g, GROUP_SIZE)
        m2 = jnp.max(jnp.where(iota_g == i1, -jnp.inf, vg), axis=0, keepdims=True)
        grp.append(m1 + m2)
    work = jnp.concatenate(grp, axis=0)
    iota_n = lax.broadcasted_iota(jnp.int32, (N_GROUPS, tr), 0)
    gsel = jnp.zeros((N_GROUPS, tr), F32)
    for _ in range(TOPK_GROUPS):
        _, gi = _first_argmax(work, iota_n, N_GROUPS)
        hit = iota_n == gi
        gsel = jnp.where(hit, 1.0, gsel)
        work = jnp.where(hit, -jnp.inf, work)
    emask = jnp.concatenate([jnp.broadcast_to(gsel[g:g + 1, :], (GROUP_SIZE, tr)) for g in range(N_GROUPS)], axis=0)
    work = jnp.where(emask > 0.5, biased, NEG_INF)

    iota_e = lax.broadcasted_iota(jnp.int32, (N_EXPERTS, tr), 0)
    hits, eids, gates = [], [], []
    chosen = jnp.zeros((N_EXPERTS, tr), F32)
    for _ in range(TOP_K):
        _, ei = _first_argmax(work, iota_e, N_EXPERTS)
        hit = iota_e == ei
        hits.append(hit)
        eids.append(ei)
        gates.append(jnp.sum(jnp.where(hit, scores, 0.0), axis=0, keepdims=True))
        chosen = jnp.where(hit, 1.0, chosen)
        work = jnp.where(hit, -jnp.inf, work)
    gsum = gates[0]
    for gk in gates[1:]:
        gsum = gsum + gk

    r = lax.broadcasted_iota(jnp.int32, (tr, tr), 0)
    c = lax.broadcasted_iota(jnp.int32, (tr, tr), 1)
    upper = jnp.where(r < c, 1.0, 0.0).astype(BF16)
    before = _dot(chosen.astype(BF16), upper) + carry_ref[...]
    zero_i = jnp.zeros((8 - TOP_K, tr), jnp.int32)
    eidx_ref[...] = jnp.concatenate(eids + [zero_i], axis=0)
    pos_ref[...] = jnp.concatenate(
        [jnp.sum(jnp.where(h, before, 0.0), axis=0, keepdims=True).astype(jnp.int32) for h in hits] + [zero_i], axis=0)
    gmat = jnp.concatenate([gk / gsum * ROUTE_SCALE for gk in gates] + [jnp.zeros((128 - TOP_K, tr), F32)], axis=0)
    g_hi = gmat.astype(BF16)
    g_lo = (gmat - g_hi.astype(F32)).astype(BF16)
    eye = jnp.where(r == c, 1.0, 0.0).astype(BF16)
    gate_ref[...] = _dot_nt(eye, g_hi) + _dot_nt(eye, g_lo)
    carry_ref[...] = carry_ref[...] + jnp.sum(chosen, axis=1, keepdims=True)
    cnt_ref[...] = jnp.broadcast_to(carry_ref[...], cnt_ref.shape)


def _router(x1, router_w, router_b):
    t = x1.shape[0]
    tr = ROUTER_TR
    col = pl.BlockSpec((8, tr), lambda i: (0, i))
    return pl.pallas_call(
        _router_kernel,
        grid=(t // tr,),
        in_specs=[pl.BlockSpec((tr, D_MODEL), lambda i: (i, 0)),
                  pl.BlockSpec((N_EXPERTS, D_MODEL), lambda i: (0, 0)),
                  pl.BlockSpec((N_EXPERTS, 1), lambda i: (0, 0))],
        out_specs=[col, pl.BlockSpec((tr, 128), lambda i: (i, 0)), col,
                   pl.BlockSpec((N_EXPERTS, 128), lambda i: (0, 0))],
        out_shape=[jax.ShapeDtypeStruct((8, t), jnp.int32), jax.ShapeDtypeStruct((t, 128), F32),
                   jax.ShapeDtypeStruct((8, t), jnp.int32), jax.ShapeDtypeStruct((N_EXPERTS, 128), F32)],
        scratch_shapes=[pltpu.VMEM((N_EXPERTS, 1), F32)],
        compiler_params=_params("arbitrary"),
        name="moe_router",
    )(x1, router_w, router_b.reshape(-1, 1))


DISPATCH_TD = 256


WAIT_UNROLL = 8


def _row_copy_loops(n_tok, row_copy):
    def issue(pstart_ref, eidx_ref, pos_ref):
        def body(t, carry):
            for k in range(TOP_K):
                row_copy(t, k, pstart_ref[eidx_ref[k, t]] + pos_ref[k, t]).start()
            return carry
        lax.fori_loop(0, n_tok, body, 0)

    def drain():
        def body(_, carry):
            for _ in range(WAIT_UNROLL):
                for k in range(TOP_K):
                    row_copy(0, k, 0).wait()
            return carry
        lax.fori_loop(0, n_tok // WAIT_UNROLL, body, 0)

    return issue, drain


def _dispatch_kernel(pstart_ref, eidx_ref, pos_ref, x_ref, buf_hbm, o_hbm, sem):
    del buf_hbm

    def row_copy(t, k, row):
        return pltpu.make_async_copy(x_ref.at[pl.ds(t, 1)], o_hbm.at[pl.ds(row, 1)], sem)

    issue, drain = _row_copy_loops(DISPATCH_TD, row_copy)
    issue(pstart_ref, eidx_ref, pos_ref)
    drain()


def _dispatch(pstart, eidx, pos, x1p, rows):
    t, half = x1p.shape
    td = DISPATCH_TD
    grid_spec = pltpu.PrefetchScalarGridSpec(
        num_scalar_prefetch=1,
        grid=(t // td,),
        in_specs=[pl.BlockSpec((8, td), lambda i, ps: (0, i), memory_space=pltpu.SMEM),
                  pl.BlockSpec((8, td), lambda i, ps: (0, i), memory_space=pltpu.SMEM),
                  pl.BlockSpec((td, half), lambda i, ps: (i, 0)),
                  pl.BlockSpec(memory_space=pl.ANY)],
        out_specs=pl.BlockSpec(memory_space=pl.ANY),
        scratch_shapes=[pltpu.SemaphoreType.DMA],
    )
    return pl.pallas_call(
        _dispatch_kernel,
        grid_spec=grid_spec,
        out_shape=jax.ShapeDtypeStruct((rows, half), jnp.uint32),
        input_output_aliases={4: 0},
        compiler_params=_params("arbitrary"),
        name="moe_dispatch",
    )(pstart, eidx, pos, x1p, jnp.zeros((rows, half), jnp.uint32))


def _expert_kernel(be_ref, nu_ref, x_ref, wg_ref, wu_ref, wd_ref, o_ref):
    r = pl.program_id(0)

    @pl.when(r < nu_ref[0])
    def _():
        xb = _unpack_halves(x_ref[...]).astype(BF16)
        h = _silu(_dot(xb, wg_ref[0])) * _dot(xb, wu_ref[0])
        o_ref[...] = _pack_halves(_dot(h.astype(BF16), wd_ref[0]))

    @pl.when(r >= nu_ref[0])
    def _():
        o_ref[...] = jnp.zeros_like(o_ref)


def _experts(blk_exp, n_used, xs, wg, wu, wd):
    rows, half = xs.shape
    nblk = rows // MOE_BLK
    grid_spec = pltpu.PrefetchScalarGridSpec(
        num_scalar_prefetch=2,
        grid=(nblk,),
        in_specs=[pl.BlockSpec((MOE_BLK, half), lambda r, be, nu: (r, 0)),
                  pl.BlockSpec((1, D_MODEL, D_EXPERT), lambda r, be, nu: (be[r], 0, 0)),
                  pl.BlockSpec((1, D_MODEL, D_EXPERT), lambda r, be, nu: (be[r], 0, 0)),
                  pl.BlockSpec((1, D_EXPERT, D_MODEL), lambda r, be, nu: (be[r], 0, 0))],
        out_specs=pl.BlockSpec((MOE_BLK, half), lambda r, be, nu: (r, 0)),
    )
    return pl.pallas_call(
        _expert_kernel,
        grid_spec=grid_spec,
        out_shape=jax.ShapeDtypeStruct((rows, half), jnp.uint32),
        compiler_params=_params("arbitrary"),
        name="moe_experts",
    )(blk_exp, n_used, xs, wg, wu, wd)


COMBINE_TC = 256


def _combine_kernel(pstart_ref, eidx_ref, pos_ref, x_ref, gate_ref, y_hbm, swg_ref, swu_ref, swd_ref, g_ref, b_ref,
                    o_ref, buf_ref, sem):
    def row_copy(t, k, row):
        return pltpu.make_async_copy(y_hbm.at[pl.ds(row, 1)], buf_ref.at[k, pl.ds(t, 1)], sem)

    issue, drain = _row_copy_loops(COMBINE_TC, row_copy)
    issue(pstart_ref, eidx_ref, pos_ref)

    x = x_ref[...]
    xb = x.astype(BF16)
    h = _silu(_dot(xb, swg_ref[...])) * _dot(xb, swu_ref[...])
    acc = ALPHA * x + _dot(h.astype(BF16), swd_ref[...])

    drain()

    gates = gate_ref[...]
    for k in range(TOP_K):
        acc = acc + gates[:, k:k + 1] * _unpack_halves(buf_ref[k])
    o_ref[...] = _layer_norm(acc, g_ref[...], b_ref[...])


def _combine(pstart, eidx, pos, x1, gates_t, y_rows, swg, swu, swd, g, b):
    t = x1.shape[0]
    tc = COMBINE_TC
    half = y_rows.shape[1]

    def whole(arr):
        return pl.BlockSpec(arr.shape, lambda i, ps: (0, 0))

    g2, b2 = g.reshape(1, -1), b.reshape(1, -1)
    grid_spec = pltpu.PrefetchScalarGridSpec(
        num_scalar_prefetch=1,
        grid=(t // tc,),
        in_specs=[pl.BlockSpec((8, tc), lambda i, ps: (0, i), memory_space=pltpu.SMEM),
                  pl.BlockSpec((8, tc), lambda i, ps: (0, i), memory_space=pltpu.SMEM),
                  pl.BlockSpec((tc, D_MODEL), lambda i, ps: (i, 0)),
                  pl.BlockSpec((tc, 128), lambda i, ps: (i, 0)),
                  pl.BlockSpec(memory_space=pl.ANY),
                  whole(swg), whole(swu), whole(swd), whole(g2), whole(b2)],
        out_specs=pl.BlockSpec((tc, D_MODEL), lambda i, ps: (i, 0)),
        scratch_shapes=[pltpu.VMEM((TOP_K, tc, half), jnp.uint32), pltpu.SemaphoreType.DMA],
    )
    return pl.pallas_call(
        _combine_kernel,
        grid_spec=grid_spec,
        out_shape=jax.ShapeDtypeStruct((t, D_MODEL), F32),
        compiler_params=_params("arbitrary"),
        name="moe_combine_shared_ln",
    )(pstart, eidx, pos, x1, gates_t, y_rows, swg, swu, swd, g2, b2)


def _moe(x1, x1p, router_w, router_b, wg, wu, wd, swg, swu, swd, g, b):
    t = x1.shape[0]
    eidx, gates, pos, cnt = _router(x1, router_w, router_b)
    counts = cnt[:, 0].astype(jnp.int32)
    padded = (counts + MOE_BLK - 1) // MOE_BLK * MOE_BLK
    pend = jnp.cumsum(padded)
    pstart = pend - padded
    n_blocks = (t * TOP_K + N_EXPERTS * (MOE_BLK - 1) + MOE_BLK - 1) // MOE_BLK
    blk_first = jnp.arange(n_blocks, dtype=jnp.int32) * MOE_BLK
    blk_exp = jnp.minimum(jnp.sum((pend[None, :] <= blk_first[:, None]).astype(jnp.int32), axis=1), N_EXPERTS - 1)
    n_used = (pend[-1:] // MOE_BLK).astype(jnp.int32)
    xs = _dispatch(pstart, eidx, pos, x1p, n_blocks * MOE_BLK)
    y_rows = _experts(blk_exp, n_used, xs, wg, wu, wd)
    return _combine(pstart, eidx, pos, x1, gates, y_rows, swg, swu, swd, g, b)


def _layer(x2d, bsz, seq, w_in, conv_dw, conv_db, conv_ln_g, conv_ln_b, swa_sink, ret_gn_g, ret_gn_b,
           nsa_pe_k, nsa_w1_k, nsa_w2_k, nsa_pe_v, nsa_w1_v, nsa_w2_v, p_conv, p_swa, p_ret, p_nsa, w_out,
           ln1_g, ln1_b, router_w, router_b, moe_wg, moe_wu, moe_wd, sh_wg, sh_wu, sh_wd, ln2_g, ln2_b):
    bf = lambda a: a.astype(BF16)
    w_a = jnp.pad(bf(w_in[:, :N_MIX_COLS]), ((0, 0), (0, UA - N_MIX_COLS)))
    w_m = bf(w_in[:, N_MIX_COLS:])
    ua, kcvc = _inproj(x2d, w_a)
    conv_a = _conv(ua, conv_dw, conv_db, conv_ln_g, conv_ln_b, bsz, seq)
    swa_o = _swa(ua, swa_sink, bsz, seq)
    ret_o = _retention(ua, ret_gn_g, ret_gn_b, bsz, seq)
    kcmp, vcmp = _compress(kcvc, nsa_pe_k, bf(nsa_w1_k), bf(nsa_w2_k), nsa_pe_v, bf(nsa_w1_v), bf(nsa_w2_v), bsz, seq)
    nsa_o = _nsa(ua, kcmp, vcmp, bsz, seq)
    x1, x1p = _merge(x2d, conv_a, swa_o, ret_o, nsa_o, w_m, bf(p_conv), bf(p_swa), bf(p_ret), bf(p_nsa),
                     bf(w_out), ln1_g, ln1_b)
    return _moe(x1, x1p, router_w, router_b, bf(moe_wg), bf(moe_wu), bf(moe_wd),
                bf(sh_wg), bf(sh_wu), bf(sh_wd), ln2_g, ln2_b)


def kernel(x, w_in, conv_dw, conv_db, conv_ln_g, conv_ln_b, swa_sink, ret_gn_g, ret_gn_b, nsa_pe_k, nsa_w1_k, nsa_w2_k, nsa_pe_v, nsa_w1_v, nsa_w2_v, p_conv, p_swa, p_ret, p_nsa, w_out, ln1_g, ln1_b, router_w, router_b, moe_wg, moe_wu, moe_wd, sh_wg, sh_wu, sh_wd, ln2_g, ln2_b):
    bsz, seq, d = x.shape
    params = (w_in, conv_dw, conv_db, conv_ln_g, conv_ln_b, swa_sink, ret_gn_g, ret_gn_b,
              nsa_pe_k, nsa_w1_k, nsa_w2_k, nsa_pe_v, nsa_w1_v, nsa_w2_v, p_conv, p_swa, p_ret, p_nsa, w_out,
              ln1_g, ln1_b, router_w, router_b, moe_wg, moe_wu, moe_wd, sh_wg, sh_wu, sh_wd, ln2_g, ln2_b)
    x2d = x.reshape(bsz * seq, d)
    for l in range(w_in.shape[0]):
        x2d = _layer(x2d, bsz, seq, *[p[l] for p in params])
    return x2d.reshape(bsz, seq, d)
```

```python
import functools
import math

import jax
import jax.numpy as jnp
import numpy as np
from jax import lax
from jax.experimental import pallas as pl
from jax.experimental.pallas import tpu as pltpu

F32 = jnp.float32
BF16 = jnp.bfloat16

D_MODEL = 1024
DEPTH = 2
HEAD_DIM = 64
BLK = 128
CONV_CH = 256
CONV_K = 31
SWA_HEADS = 4
SWA_KV_HEADS = 2
SWA_WINDOW = 128
RET_HEADS = 4
RET_QK = 64
RET_V = 128
NSA_HEADS = 4
NSA_CMP_LEN = 32
NSA_CMP_STRIDE = 16
NSA_CMP_HIDDEN = 256
NSA_SLC_LEN = 64
NSA_TOPN = 8
NSA_WINDOW = 512
NSA_FORCE = 1e4
N_EXPERTS = 64
TOP_K = 6
N_GROUPS = 8
GROUP_SIZE = N_EXPERTS // N_GROUPS
TOPK_GROUPS = 4
D_EXPERT = 256
ROUTE_SCALE = 2.5
MOE_BLK = 256
N_BRANCH = 4
ALPHA = (2 * DEPTH) ** 0.25
LN_EPS = 1e-5
NEG_INF = -1e30
SCALE = HEAD_DIM ** -0.5

C_CONV = 0
C_SWA = 512
C_RETQK = 1024
C_RETV = 1536
C_RETG = 2048
C_NSAQ = 2560
C_NSAC = 2816
C_NSAS = 2944
C_NSAW = 3072
C_NSAG = 3200
N_MIX_COLS = 3212
UA = 3328

ALIBI = [2.0 ** (-8.0 * (i + 1) / 8) for i in range(8)]
SLOPES_SWA = ALIBI[:4]
SLOPES_NSA = ALIBI[4:]
RET_LOG_GAMMA = [math.log1p(-(2.0 ** (-5.0 - h))) for h in range(RET_HEADS)]

VMEM_LIMIT = 48 * 1024 * 1024
NT_DIMS = (((1,), (1,)), ((), ()))
TN_DIMS = (((0,), (0,)), ((), ()))


def _params(*sem):
    return pltpu.CompilerParams(dimension_semantics=sem, vmem_limit_bytes=VMEM_LIMIT)


def _dot(a, b):
    return jnp.dot(a, b, preferred_element_type=F32)


def _dot_nt(a, b):
    return lax.dot_general(a, b, NT_DIMS, preferred_element_type=F32)


def _layer_norm(v, g, b):
    mu = jnp.mean(v, axis=-1, keepdims=True)
    c = v - mu
    var = jnp.mean(c * c, axis=-1, keepdims=True)
    return c * lax.rsqrt(var + LN_EPS) * g + b


def _silu(v):
    return v * jax.nn.sigmoid(v)


def _pack_halves(v):
    n = v.shape[1] // 2
    r = v.astype(BF16).astype(F32)
    lo = lax.shift_right_logical(lax.bitcast_convert_type(r[:, :n], jnp.uint32), jnp.uint32(16))
    hi = lax.bitcast_convert_type(r[:, n:], jnp.uint32) & jnp.uint32(0xFFFF0000)
    return lo | hi


def _unpack_halves(w):
    lo = lax.bitcast_convert_type(lax.shift_left(w, jnp.uint32(16)), F32)
    hi = lax.bitcast_convert_type(w & jnp.uint32(0xFFFF0000), F32)
    return jnp.concatenate([lo, hi], axis=1)


CAST_TM = 256
LANES = 128


def _cast_kernel(x_ref, o_ref):
    o_ref[...] = x_ref[...].astype(o_ref.dtype)


def _mix_weights(w_in, layer):
    return pl.pallas_call(
        _cast_kernel,
        grid=(D_MODEL // CAST_TM,),
        in_specs=[pl.BlockSpec((None, CAST_TM, UA), lambda i: (layer, i, 0))],
        out_specs=pl.BlockSpec((CAST_TM, UA), lambda i: (i, 0)),
        out_shape=jax.ShapeDtypeStruct((D_MODEL, UA), BF16),
        compiler_params=_params("parallel"),
        name="cast_mix_weights",
    )(w_in)


def _shifted_cast_kernel(a_ref, b_ref, o_ref):
    shift = N_MIX_COLS % LANES
    both = jnp.concatenate([a_ref[...], b_ref[...]], axis=1)
    o_ref[...] = both[:, shift:shift + LANES].astype(o_ref.dtype)


def _merge_gate_weights(w_in, layer):
    first = N_MIX_COLS // LANES
    width = N_BRANCH * D_MODEL
    return pl.pallas_call(
        _shifted_cast_kernel,
        grid=(width // LANES,),
        in_specs=[pl.BlockSpec((None, D_MODEL, LANES), lambda j: (layer, 0, first + j)),
                  pl.BlockSpec((None, D_MODEL, LANES), lambda j: (layer, 0, first + j + 1))],
        out_specs=pl.BlockSpec((D_MODEL, LANES), lambda j: (0, j)),
        out_shape=jax.ShapeDtypeStruct((D_MODEL, width), BF16),
        compiler_params=_params("parallel"),
        name="cast_merge_gate_weights",
    )(w_in, w_in)


INPROJ_TM = 512


def _inproj_kernel(x_ref, w_ref, o_ref, oc_ref):
    xb = x_ref[...].astype(BF16)
    for c in range(0, UA, 256):
        o_ref[:, c:c + 256] = _dot(xb, w_ref[:, c:c + 256]).astype(o_ref.dtype)
    oc_ref[...] = _dot(xb, w_ref[:, C_NSAC:C_NSAC + 128])


def _inproj(x2d, w_a):
    t = x2d.shape[0]
    return pl.pallas_call(
        _inproj_kernel,
        grid=(t // INPROJ_TM,),
        in_specs=[pl.BlockSpec((INPROJ_TM, D_MODEL), lambda i: (i, 0)),
                  pl.BlockSpec((D_MODEL, UA), lambda i: (0, 0))],
        out_specs=[pl.BlockSpec((INPROJ_TM, UA), lambda i: (i, 0)),
                   pl.BlockSpec((INPROJ_TM, 128), lambda i: (i, 0))],
        out_shape=[jax.ShapeDtypeStruct((t, UA), BF16), jax.ShapeDtypeStruct((t, 128), F32)],
        compiler_params=_params("parallel"),
        name="inproj",
    )(x2d, w_a)


CONV_TS = 256
CONV_HALO = 32


def _conv_kernel(cur_ref, halo_ref, dw_ref, db_ref, g_ref, b_ref, o_ref, hc_ref):
    i = pl.program_id(1)

    def glu(u):
        u = u.astype(F32)
        return u[:, :CONV_CH] * jax.nn.sigmoid(u[:, CONV_CH:])

    hc_ref[0:CONV_HALO, :] = jnp.where(i > 0, glu(halo_ref[...]), 0.0)
    hc_ref[CONV_HALO:CONV_HALO + CONV_TS, :] = glu(cur_ref[...])
    acc = jnp.broadcast_to(db_ref[...], (CONV_TS, CONV_CH))
    off = CONV_HALO - (CONV_K - 1)
    for j in range(CONV_K):
        acc = acc + hc_ref[off + j:off + j + CONV_TS, :] * dw_ref[j:j + 1, :]
    o_ref[...] = _silu(_layer_norm(acc, g_ref[...], b_ref[...])).astype(o_ref.dtype)


def _conv(ua, dw, db, g, b, bsz, seq):
    t = bsz * seq
    ns = seq // CONV_TS
    per = CONV_TS // CONV_HALO
    dwp = jnp.concatenate([dw, jnp.zeros((32 - CONV_K, CONV_CH), F32)], axis=0)
    vec = pl.BlockSpec((1, CONV_CH), lambda bb, i: (0, 0))
    return pl.pallas_call(
        _conv_kernel,
        grid=(bsz, ns),
        in_specs=[pl.BlockSpec((CONV_TS, 512), lambda bb, i: (bb * ns + i, C_CONV // 512)),
                  pl.BlockSpec((CONV_HALO, 512),
                               lambda bb, i: (jnp.maximum((bb * ns + i) * per - 1, 0), C_CONV // 512)),
                  pl.BlockSpec((32, CONV_CH), lambda bb, i: (0, 0)), vec, vec, vec],
        out_specs=pl.BlockSpec((CONV_TS, CONV_CH), lambda bb, i: (bb * ns + i, 0)),
        out_shape=jax.ShapeDtypeStruct((t, CONV_CH), BF16),
        scratch_shapes=[pltpu.VMEM((CONV_HALO + CONV_TS, CONV_CH), F32)],
        compiler_params=_params("parallel", "parallel"),
        name="conformer_conv",
    )(ua, ua, dwp, db.reshape(1, -1), g.reshape(1, -1), b.reshape(1, -1))


def _swa_kernel(sink_ref, q_ref, kvc_ref, kvp_ref, o_ref):
    i = pl.program_id(1)
    q = q_ref[...]
    kvc = kvc_ref[...]
    kvp = kvp_ref[...]
    a = lax.broadcasted_iota(jnp.int32, (BLK, 2 * BLK), 0)
    c = lax.broadcasted_iota(jnp.int32, (BLK, 2 * BLK), 1)
    dist = BLK + a - c
    mask = (dist >= 0) & (dist < SWA_WINDOW) & ((c >= BLK) | (i > 0))
    distf = dist.astype(F32)
    rep = SWA_HEADS // SWA_KV_HEADS
    for h in range(SWA_HEADS):
        g = h // rep
        k = jnp.concatenate([kvp[:, g * HEAD_DIM:(g + 1) * HEAD_DIM],
                             kvc[:, g * HEAD_DIM:(g + 1) * HEAD_DIM]], axis=0)
        v = jnp.concatenate([kvp[:, BLK + g * HEAD_DIM:BLK + (g + 1) * HEAD_DIM],
                             kvc[:, BLK + g * HEAD_DIM:BLK + (g + 1) * HEAD_DIM]], axis=0)
        s = _dot_nt(q[:, h * HEAD_DIM:(h + 1) * HEAD_DIM], k) * SCALE - SLOPES_SWA[h] * distf
        s = jnp.where(mask, s, NEG_INF)
        sink = sink_ref[h]
        m = jnp.maximum(jnp.max(s, axis=-1, keepdims=True), sink)
        e = jnp.where(mask, jnp.exp(s - m), 0.0)
        den = jnp.sum(e, axis=-1, keepdims=True) + jnp.exp(sink - m)
        p = e / jnp.maximum(den, 1e-30)
        o_ref[:, h * HEAD_DIM:(h + 1) * HEAD_DIM] = _dot(p.astype(BF16), v).astype(o_ref.dtype)


def _swa(ua, sink, bsz, seq):
    t = bsz * seq
    nb = seq // BLK
    return pl.pallas_call(
        _swa_kernel,
        grid=(bsz, nb),
        in_specs=[pl.BlockSpec(memory_space=pltpu.SMEM),
                  pl.BlockSpec((BLK, 256), lambda bb, i: (bb * nb + i, C_SWA // 256)),
                  pl.BlockSpec((BLK, 256), lambda bb, i: (bb * nb + i, C_SWA // 256 + 1)),
                  pl.BlockSpec((BLK, 256), lambda bb, i: (jnp.maximum(bb * nb + i - 1, 0), C_SWA // 256 + 1))],
        out_specs=pl.BlockSpec((BLK, 256), lambda bb, i: (bb * nb + i, 0)),
        out_shape=jax.ShapeDtypeStruct((t, SWA_HEADS * HEAD_DIM), BF16),
        compiler_params=_params("parallel", "parallel"),
        name="swa",
    )(sink, ua, ua, ua)


def _ret_kernel(qk_ref, v_ref, g_ref, gng_ref, gnb_ref, o_ref, state_ref):
    n = pl.program_id(1)

    @pl.when(n == 0)
    def _():
        state_ref[...] = jnp.zeros_like(state_ref)

    qk = qk_ref[...]
    vv = v_ref[...]
    gate = g_ref[...].astype(F32)
    a = lax.broadcasted_iota(jnp.int32, (BLK, BLK), 0)
    c = lax.broadcasted_iota(jnp.int32, (BLK, BLK), 1)
    diff = (a - c).astype(F32)
    idx = lax.broadcasted_iota(jnp.int32, (BLK, 1), 0).astype(F32)
    hq = RET_HEADS * RET_QK
    for h in range(RET_HEADS):
        lg = RET_LOG_GAMMA[h]
        q = qk[:, h * RET_QK:(h + 1) * RET_QK]
        k = qk[:, hq + h * RET_QK:hq + (h + 1) * RET_QK].astype(F32) * (RET_QK ** -0.5)
        v = vv[:, h * RET_V:(h + 1) * RET_V]
        dmat = jnp.where(diff >= 0, jnp.exp(lg * jnp.maximum(diff, 0.0)), 0.0)
        inner = _dot_nt(q, k.astype(BF16)) * dmat
        o_inner = _dot(inner.astype(BF16), v)
        zeta = jnp.exp(lg * (BLK - 1 - idx))
        xi = jnp.exp(lg * (idx + 1.0))
        kv = lax.dot_general((k * zeta).astype(BF16), v, TN_DIMS, preferred_element_type=F32)
        prev = state_ref[h]
        o = o_inner + _dot(q, prev.astype(BF16)) * xi
        state_ref[h] = math.exp(lg * BLK) * prev + kv
        mu = jnp.mean(o, axis=-1, keepdims=True)
        cen = o - mu
        var = jnp.mean(cen * cen, axis=-1, keepdims=True)
        sl = slice(h * RET_V, (h + 1) * RET_V)
        on = cen * lax.rsqrt(var + LN_EPS) * gng_ref[:, sl] + gnb_ref[:, sl]
        o_ref[:, sl] = (_silu(gate[:, sl]) * on).astype(o_ref.dtype)


def _retention(ua, gn_g, gn_b, bsz, seq):
    t = bsz * seq
    nc = seq // BLK
    vec = pl.BlockSpec((1, RET_HEADS * RET_V), lambda bb, i: (0, 0))
    return pl.pallas_call(
        _ret_kernel,
        grid=(bsz, nc),
        in_specs=[pl.BlockSpec((BLK, 512), lambda bb, i: (bb * nc + i, C_RETQK // 512)),
                  pl.BlockSpec((BLK, 512), lambda bb, i: (bb * nc + i, C_RETV // 512)),
                  pl.BlockSpec((BLK, 512), lambda bb, i: (bb * nc + i, C_RETG // 512)),
                  vec, vec],
        out_specs=pl.BlockSpec((BLK, 512), lambda bb, i: (bb * nc + i, 0)),
        out_shape=jax.ShapeDtypeStruct((t, RET_HEADS * RET_V), BF16),
        scratch_shapes=[pltpu.VMEM((RET_HEADS, RET_QK, RET_V), F32)],
        compiler_params=_params("parallel", "arbitrary"),
        name="retention",
    )(ua, ua, ua, gn_g.reshape(1, -1), gn_b.reshape(1, -1))


def _compress_kernel(t_ref, pek_ref, w1k_ref, w2k_ref, pev_ref, w1v_ref, w2v_ref, ok_ref, ov_ref):
    st = NSA_CMP_STRIDE
    ng = t_ref.shape[0] // st
    branches = ((pek_ref, w1k_ref, w2k_ref, ok_ref), (pev_ref, w1v_ref, w2v_ref, ov_ref))
    first = [jnp.zeros((ng, NSA_CMP_HIDDEN), F32) for _ in branches]
    second = [jnp.zeros((ng, NSA_CMP_HIDDEN), F32) for _ in branches]
    for r in range(st):
        rows = t_ref[pl.ds(r, ng, stride=st), :]
        for n, (pe_ref, w1_ref, _, _) in enumerate(branches):
            v = rows[:, n * HEAD_DIM:(n + 1) * HEAD_DIM]
            lo = (v + pe_ref[r:r + 1, :]).astype(BF16)
            hi = (v + pe_ref[st + r:st + r + 1, :]).astype(BF16)
            first[n] = first[n] + _dot(lo, w1_ref[r * HEAD_DIM:(r + 1) * HEAD_DIM, :])
            second[n] = second[n] + _dot(hi, w1_ref[(st + r) * HEAD_DIM:(st + r + 1) * HEAD_DIM, :])
    for n, (_, _, w2_ref, o_ref) in enumerate(branches):
        hid = first[n] + pltpu.roll(second[n], ng - 1, axis=0)
        act = 0.5 * hid * (1.0 + lax.erf(hid * (2.0 ** -0.5)))
        o_ref[0] = _dot(act.astype(BF16), w2_ref[...]).astype(o_ref.dtype)


def _compress(kcvc, pe_k, w1_k, w2_k, pe_v, w1_v, w2_v, bsz, seq):
    ng = seq // NSA_CMP_STRIDE
    pes = pl.BlockSpec((NSA_CMP_LEN, HEAD_DIM), lambda bb: (0, 0))
    w1s = pl.BlockSpec((NSA_CMP_LEN * HEAD_DIM, NSA_CMP_HIDDEN), lambda bb: (0, 0))
    w2s = pl.BlockSpec((NSA_CMP_HIDDEN, HEAD_DIM), lambda bb: (0, 0))
    osp = pl.BlockSpec((1, ng, HEAD_DIM), lambda bb: (bb, 0, 0))
    return pl.pallas_call(
        _compress_kernel,
        grid=(bsz,),
        in_specs=[pl.BlockSpec((seq, 128), lambda bb: (bb, 0)), pes, w1s, w2s, pes, w1s, w2s],
        out_specs=[osp, osp],
        out_shape=[jax.ShapeDtypeStruct((bsz, ng, HEAD_DIM), BF16)] * 2,
        compiler_params=_params("parallel"),
        name="nsa_compress",
    )(kcvc, pe_k, w1_k, w2_k, pe_v, w1_v, w2_v)


NSA_KC = 512
NSA_WSPAN = NSA_WINDOW + BLK


def _masked_softmax(s, mask):
    s = jnp.where(mask, s, NEG_INF)
    m = jnp.max(s, axis=-1, keepdims=True)
    e = jnp.where(mask, jnp.exp(s - m), 0.0)
    return e / jnp.maximum(jnp.sum(e, axis=-1, keepdims=True), 1e-30)


def _nsa_kernel_rowmajor(q_ref, kcmp_ref, vcmp_ref, ksvs_ref, kwvw_ref, gate_ref, o_ref, selk_ref):
    i = pl.program_id(1)
    seq = ksvs_ref.shape[0]
    ncmp = kcmp_ref.shape[1]
    nslc = seq // NSA_SLC_LEN
    q = q_ref[...]
    qh = [q[:, h * HEAD_DIM:(h + 1) * HEAD_DIM] for h in range(NSA_HEADS)]
    t0 = i * BLK
    tq = t0 + lax.broadcasted_iota(jnp.int32, (BLK, 1), 0)

    kcmp = kcmp_ref[0]
    vcmp = vcmp_ref[0]
    cend = lax.broadcasted_iota(jnp.int32, (1, ncmp), 1) * NSA_CMP_STRIDE + (NSA_CMP_LEN - 1)
    cdist = tq - cend
    cmask = cdist >= 0
    cdistf = cdist.astype(F32)
    o_cmp = []
    psum = jnp.zeros((BLK, ncmp), F32)
    for h in range(NSA_HEADS):
        p = _masked_softmax(_dot_nt(qh[h], kcmp) * SCALE - SLOPES_NSA[h] * cdistf, cmask)
        psum = psum + p
        o_cmp.append(_dot(p.astype(BF16), vcmp))

    cst = lax.broadcasted_iota(jnp.int32, (ncmp, nslc), 0) * NSA_CMP_STRIDE
    sst = lax.broadcasted_iota(jnp.int32, (ncmp, nslc), 1) * NSA_SLC_LEN
    overlap = jnp.where((cst < sst + NSA_SLC_LEN) & (cst + NSA_CMP_LEN > sst), 1.0, 0.0).astype(BF16)
    p_hi = psum.astype(BF16)
    p_lo = (psum - p_hi.astype(F32)).astype(BF16)
    imp = _dot(p_hi, overlap) + _dot(p_lo, overlap)
    jj = lax.broadcasted_iota(jnp.int32, (BLK, nslc), 1)
    cur = tq // NSA_SLC_LEN
    forced = (jj == 0) | (jj == cur) | (jj == cur - 1)
    imp = jnp.where(forced, NSA_FORCE, jnp.where(jj > cur, -1.0, imp))

    sel = jnp.zeros((BLK, nslc), F32)
    for _ in range(min(NSA_TOPN, nslc)):
        mx = jnp.max(imp, axis=-1, keepdims=True)
        first = jnp.min(jnp.where(imp == mx, jj, nslc), axis=-1, keepdims=True)
        hit = jj == first
        sel = jnp.where(hit, 1.0, sel)
        imp = jnp.where(hit, -3e38, imp)
    kblk = lax.broadcasted_iota(jnp.int32, (nslc, seq), 1) // NSA_SLC_LEN
    expand = jnp.where(kblk == lax.broadcasted_iota(jnp.int32, (nslc, seq), 0), 1.0, 0.0).astype(BF16)
    selk_ref[...] = _dot(sel.astype(BF16), expand)

    def slc_step(c, carry):
        ms, ls, accs = carry
        start = pl.multiple_of(c * NSA_KC, NSA_KC)
        kv = ksvs_ref[pl.ds(start, NSA_KC), :]
        ks = kv[:, :HEAD_DIM]
        vs = kv[:, HEAD_DIM:]
        kpos = start + lax.broadcasted_iota(jnp.int32, (1, NSA_KC), 1)
        dist = tq - kpos
        mask = (selk_ref[:, pl.ds(start, NSA_KC)] > 0.5) & (dist >= 0)
        distf = dist.astype(F32)
        nm, nl, na = [], [], []
        for h in range(NSA_HEADS):
            s = _dot_nt(qh[h], ks) * SCALE - SLOPES_NSA[h] * distf
            s = jnp.where(mask, s, NEG_INF)
            m_new = jnp.maximum(ms[h], jnp.max(s, axis=-1, keepdims=True))
            alpha = jnp.exp(ms[h] - m_new)
            p = jnp.where(mask, jnp.exp(s - m_new), 0.0)
            nm.append(m_new)
            nl.append(alpha * ls[h] + jnp.sum(p, axis=-1, keepdims=True))
            na.append(alpha * accs[h] + _dot(p.astype(BF16), vs))
        return tuple(nm), tuple(nl), tuple(na)

    init = (tuple(jnp.full((BLK, 1), NEG_INF, F32) for _ in range(NSA_HEADS)),
            tuple(jnp.zeros((BLK, 1), F32) for _ in range(NSA_HEADS)),
            tuple(jnp.zeros((BLK, HEAD_DIM), F32) for _ in range(NSA_HEADS)))
    n_steps = (t0 + BLK - 1) // NSA_KC + 1
    _, ls, accs = lax.fori_loop(0, n_steps, slc_step, init)
    o_slc = [accs[h] / jnp.maximum(ls[h], 1e-30) for h in range(NSA_HEADS)]

    wspan = min(NSA_WSPAN, seq)
    wstart = pl.multiple_of(jnp.maximum(t0 + BLK - wspan, 0), BLK)
    kvw = kwvw_ref[pl.ds(wstart, wspan), :]
    kw = kvw[:, :HEAD_DIM]
    vw = kvw[:, HEAD_DIM:]
    wdist = tq - (wstart + lax.broadcasted_iota(jnp.int32, (1, wspan), 1))
    wmask = (wdist >= 0) & (wdist < NSA_WINDOW)
    wdistf = wdist.astype(F32)
    gts = jax.nn.sigmoid(gate_ref[...].astype(F32))
    for h in range(NSA_HEADS):
        p = _masked_softmax(_dot_nt(qh[h], kw) * SCALE - SLOPES_NSA[h] * wdistf, wmask)
        o_win = _dot(p.astype(BF16), vw)
        o = (gts[:, h:h + 1] * o_cmp[h] + gts[:, NSA_HEADS + h:NSA_HEADS + h + 1] * o_slc[h]
             + gts[:, 2 * NSA_HEADS + h:2 * NSA_HEADS + h + 1] * o_win)
        o_ref[:, h * HEAD_DIM:(h + 1) * HEAD_DIM] = o.astype(o_ref.dtype)


def _nsa_rowmajor(ua, kcmp, vcmp, bsz, seq):
    t = bsz * seq
    nq = seq // BLK
    ncmp = kcmp.shape[1]
    return pl.pallas_call(
        _nsa_kernel_rowmajor,
        grid=(bsz, nq),
        in_specs=[pl.BlockSpec((BLK, 256), lambda bb, i: (bb * nq + i, C_NSAQ // 256)),
                  pl.BlockSpec((1, ncmp, HEAD_DIM), lambda bb, i: (bb, 0, 0)),
                  pl.BlockSpec((1, ncmp, HEAD_DIM), lambda bb, i: (bb, 0, 0)),
                  pl.BlockSpec((seq, 128), lambda bb, i: (bb, C_NSAS // 128)),
                  pl.BlockSpec((seq, 128), lambda bb, i: (bb, C_NSAW // 128)),
                  pl.BlockSpec((BLK, 128), lambda bb, i: (bb * nq + i, C_NSAG // 128))],
        out_specs=pl.BlockSpec((BLK, 256), lambda bb, i: (bb * nq + i, 0)),
        out_shape=jax.ShapeDtypeStruct((t, NSA_HEADS * HEAD_DIM), BF16),
        scratch_shapes=[pltpu.VMEM((BLK, seq), F32)],
        compiler_params=_params("parallel", "parallel"),
        name="nsa_attention",
    )(ua, kcmp, vcmp, ua, ua, ua)


NSA_PAD = 128
Q_LANES = NSA_HEADS * BLK


def _nsa_query_select():
    sel = np.zeros((NSA_HEADS * HEAD_DIM, NSA_HEADS * NSA_PAD), np.float32)
    for h in range(NSA_HEADS):
        for dd in range(HEAD_DIM):
            sel[h * HEAD_DIM + dd, h * NSA_PAD + dd] = SCALE
    return jnp.asarray(sel, BF16)


def _softmax_down(s):
    m = jnp.maximum(jnp.max(s, axis=0, keepdims=True), -1e29)
    e = jnp.exp(s - m)
    return e / jnp.maximum(jnp.sum(e, axis=0, keepdims=True), 1e-30)


def _nsa_kernel(q_ref, qsel_ref, kcmp_ref, vcmp_ref, ksvs_ref, kwvw_ref, gate_ref, o_ref,
                ksa_ref, kwa_ref, kca_ref, vst_ref, vwt_ref, vct_ref, ext_ref, bsel_ref):
    i = pl.program_id(1)
    seq = ksvs_ref.shape[0]
    ncmp = kcmp_ref.shape[1]
    nslc = seq // NSA_SLC_LEN
    lane = lax.broadcasted_iota(jnp.int32, (1, NSA_PAD), 1)

    @pl.when(i == 0)
    def _():
        def aug(k, pos):
            kf = jnp.concatenate([k.astype(F32), jnp.zeros((k.shape[0], NSA_PAD - HEAD_DIM), F32)], axis=1)
            hi = (pos // 128).astype(F32)
            lo = (pos % 128).astype(F32)
            return (kf + jnp.where(lane == HEAD_DIM, hi, jnp.where(lane == HEAD_DIM + 1, lo, 0.0))).astype(BF16)

        def value_t(v):
            vf = jnp.concatenate([v.astype(F32), jnp.zeros((v.shape[0], NSA_PAD - HEAD_DIM), F32)], axis=1)
            return vf.T.astype(BF16)

        kpos = lax.broadcasted_iota(jnp.int32, (seq, 1), 0)
        ksvs = ksvs_ref[...]
        kwvw = kwvw_ref[...]
        ksa_ref[...] = aug(ksvs[:, :HEAD_DIM], kpos)
        kwa_ref[...] = aug(kwvw[:, :HEAD_DIM], kpos)
        cend = lax.broadcasted_iota(jnp.int32, (ncmp, 1), 0) * NSA_CMP_STRIDE + (NSA_CMP_LEN - 1)
        kca_ref[...] = aug(kcmp_ref[0], cend)
        vst_ref[...] = value_t(ksvs[:, HEAD_DIM:])
        vwt_ref[...] = value_t(kwvw[:, HEAD_DIM:])
        vct_ref[...] = value_t(vcmp_ref[0])
        ext_ref[...] = jnp.where(kpos // NSA_SLC_LEN == lax.broadcasted_iota(jnp.int32, (seq, NSA_PAD), 1),
                                 1.0, 0.0).astype(BF16)

    t0 = i * BLK
    tq = t0 + lax.broadcasted_iota(jnp.int32, (1, BLK), 1)
    qw = _dot(q_ref[...], qsel_ref[...])
    qaug = []
    for h in range(NSA_HEADS):
        coef = jnp.where(lane == HEAD_DIM, 128.0 * SLOPES_NSA[h], jnp.where(lane == HEAD_DIM + 1, SLOPES_NSA[h], 0.0))
        qaug.append((qw[:, h * NSA_PAD:(h + 1) * NSA_PAD] + coef).astype(BF16))
    qaug = jnp.concatenate(qaug, axis=0)

    def tile4(b):
        return jnp.concatenate([b] * NSA_HEADS, axis=1)

    cend = lax.broadcasted_iota(jnp.int32, (ncmp, 1), 0) * NSA_CMP_STRIDE + (NSA_CMP_LEN - 1)
    p_cmp = _softmax_down(_dot_nt(kca_ref[...], qaug) + tile4(jnp.where(cend <= tq, 0.0, NEG_INF)))
    o_cmp = _dot(vct_ref[...], p_cmp.astype(BF16))
    psum = p_cmp[:, 0:BLK]
    for h in range(1, NSA_HEADS):
        psum = psum + p_cmp[:, h * BLK:(h + 1) * BLK]

    sst = lax.broadcasted_iota(jnp.int32, (nslc, ncmp), 0) * NSA_SLC_LEN
    cst = lax.broadcasted_iota(jnp.int32, (nslc, ncmp), 1) * NSA_CMP_STRIDE
    overlap = jnp.where((cst < sst + NSA_SLC_LEN) & (cst + NSA_CMP_LEN > sst), 1.0, 0.0).astype(BF16)
    p_hi = psum.astype(BF16)
    p_lo = (psum - p_hi.astype(F32)).astype(BF16)
    imp = _dot(overlap, p_hi) + _dot(overlap, p_lo)
    jj = lax.broadcasted_iota(jnp.int32, (nslc, BLK), 0)
    cur = tq // NSA_SLC_LEN
    forced = (jj == 0) | (jj == cur) | (jj == cur - 1)
    imp = jnp.where(forced, NSA_FORCE, jnp.where(jj > cur, -1.0, imp))

    sel = jnp.zeros((nslc, BLK), F32)
    for _ in range(min(NSA_TOPN, nslc)):
        mx = jnp.max(imp, axis=0, keepdims=True)
        first = jnp.min(jnp.where(imp == mx, jj, nslc), axis=0, keepdims=True)
        hit = jj == first
        sel = jnp.where(hit, 1.0, sel)
        imp = jnp.where(hit, -3e38, imp)
    sel = jnp.concatenate([sel, jnp.zeros((NSA_PAD - nslc, BLK), F32)], axis=0).astype(BF16)
    bsel_ref[...] = (_dot(ext_ref[...], sel) - 1.0) * 1e30

    def slc_step(c, carry):
        m, l, acc = carry
        start = pl.multiple_of(c * NSA_KC, NSA_KC)
        kpos = start + lax.broadcasted_iota(jnp.int32, (NSA_KC, 1), 0)
        bias = bsel_ref[pl.ds(start, NSA_KC), :] + jnp.where(kpos <= tq, 0.0, NEG_INF)
        s = _dot_nt(ksa_ref[pl.ds(start, NSA_KC), :], qaug) + tile4(bias)
        m_new = jnp.maximum(m, jnp.max(s, axis=0, keepdims=True))
        alpha = jnp.exp(m - m_new)
        p = jnp.exp(s - m_new)
        l_new = alpha * l + jnp.sum(p, axis=0, keepdims=True)
        acc_new = alpha * acc + _dot(vst_ref[:, pl.ds(start, NSA_KC)], p.astype(BF16))
        return m_new, l_new, acc_new

    init = (jnp.full((1, Q_LANES), -1e29, F32), jnp.zeros((1, Q_LANES), F32), jnp.zeros((NSA_PAD, Q_LANES), F32))
    _, l_slc, acc_slc = lax.fori_loop(0, (t0 + BLK - 1) // NSA_KC + 1, slc_step, init)
    o_slc = acc_slc / jnp.maximum(l_slc, 1e-30)

    wspan = min(NSA_WSPAN, seq)
    wstart = pl.multiple_of(jnp.maximum(t0 + BLK - wspan, 0), BLK)
    wdist = tq - (wstart + lax.broadcasted_iota(jnp.int32, (wspan, 1), 0))
    wbias = jnp.where((wdist >= 0) & (wdist < NSA_WINDOW), 0.0, NEG_INF)
    p_win = _softmax_down(_dot_nt(kwa_ref[pl.ds(wstart, wspan), :], qaug) + tile4(wbias))
    o_win = _dot(vwt_ref[:, pl.ds(wstart, wspan)], p_win.astype(BF16))

    gts = jax.nn.sigmoid(gate_ref[...].astype(F32)).T
    outs = []
    for h in range(NSA_HEADS):
        cols = slice(h * BLK, (h + 1) * BLK)
        outs.append(gts[h:h + 1, :] * o_cmp[:HEAD_DIM, cols]
                    + gts[NSA_HEADS + h:NSA_HEADS + h + 1, :] * o_slc[:HEAD_DIM, cols]
                    + gts[2 * NSA_HEADS + h:2 * NSA_HEADS + h + 1, :] * o_win[:HEAD_DIM, cols])
    o_ref[...] = jnp.concatenate(outs, axis=0).T.astype(o_ref.dtype)


def _nsa(ua, kcmp, vcmp, bsz, seq):
    t = bsz * seq
    nq = seq // BLK
    ncmp = kcmp.shape[1]
    qsel = _nsa_query_select()
    staged_k = pltpu.VMEM((seq, NSA_PAD), BF16)
    staged_v = pltpu.VMEM((NSA_PAD, seq), BF16)
    return pl.pallas_call(
        _nsa_kernel,
        grid=(bsz, nq),
        in_specs=[pl.BlockSpec((BLK, 256), lambda bb, i: (bb * nq + i, C_NSAQ // 256)),
                  pl.BlockSpec(qsel.shape, lambda bb, i: (0, 0)),
                  pl.BlockSpec((1, ncmp, HEAD_DIM), lambda bb, i: (bb, 0, 0)),
                  pl.BlockSpec((1, ncmp, HEAD_DIM), lambda bb, i: (bb, 0, 0)),
                  pl.BlockSpec((seq, 128), lambda bb, i: (bb, C_NSAS // 128)),
                  pl.BlockSpec((seq, 128), lambda bb, i: (bb, C_NSAW // 128)),
                  pl.BlockSpec((BLK, 128), lambda bb, i: (bb * nq + i, C_NSAG // 128))],
        out_specs=pl.BlockSpec((BLK, 256), lambda bb, i: (bb * nq + i, 0)),
        out_shape=jax.ShapeDtypeStruct((t, NSA_HEADS * HEAD_DIM), BF16),
        scratch_shapes=[staged_k, staged_k, pltpu.VMEM((ncmp, NSA_PAD), BF16),
                        staged_v, staged_v, pltpu.VMEM((NSA_PAD, ncmp), BF16),
                        staged_k, pltpu.VMEM((seq, BLK), F32)],
        compiler_params=_params("parallel", "arbitrary"),
        name="nsa_attention",
    )(ua, qsel, kcmp, vcmp, ua, ua, ua)


MERGE_TM = 256


def _merge_kernel(x_ref, conv_ref, swa_ref, ret_ref, nsa_ref, wm_ref, pc_ref, ps_ref, pr_ref, pn_ref,
                  wo_ref, g_ref, b_ref, o_ref, op_ref):
    x = x_ref[...]
    xb = x.astype(BF16)
    merged = None
    for n, (a_ref, p_ref) in enumerate(((conv_ref, pc_ref), (swa_ref, ps_ref), (ret_ref, pr_ref), (nsa_ref, pn_ref))):
        gate = jax.nn.sigmoid(_dot(xb, wm_ref[:, n * D_MODEL:(n + 1) * D_MODEL]))
        term = gate * _dot(a_ref[...], p_ref[...])
        merged = term if merged is None else merged + term
    mix = _dot(merged.astype(BF16), wo_ref[...])
    y = _layer_norm(ALPHA * x + mix, g_ref[...], b_ref[...])
    o_ref[...] = y
    op_ref[...] = _pack_halves(y)


def _merge(x2d, conv_a, swa_o, ret_o, nsa_o, w_m, p_conv, p_swa, p_ret, p_nsa, w_out, g, b):
    t = x2d.shape[0]
    tm = MERGE_TM

    def rows(width):
        return pl.BlockSpec((tm, width), lambda i: (i, 0))

    def whole(arr):
        return pl.BlockSpec(arr.shape, lambda i: (0, 0))

    g2, b2 = g.reshape(1, -1), b.reshape(1, -1)
    return pl.pallas_call(
        _merge_kernel,
        grid=(t // tm,),
        in_specs=[rows(D_MODEL), rows(256), rows(256), rows(512), rows(256),
                  whole(w_m), whole(p_conv), whole(p_swa), whole(p_ret), whole(p_nsa), whole(w_out),
                  whole(g2), whole(b2)],
        out_specs=[rows(D_MODEL), rows(D_MODEL // 2)],
        out_shape=[jax.ShapeDtypeStruct((t, D_MODEL), F32), jax.ShapeDtypeStruct((t, D_MODEL // 2), jnp.uint32)],
        compiler_params=_params("parallel"),
        name="merge_outproj_ln",
    )(x2d, conv_a, swa_o, ret_o, nsa_o, w_m, p_conv, p_swa, p_ret, p_nsa, w_out, g2, b2)


ROUTER_TR = 512


def _first_argmax(v, iota, size):
    m = jnp.max(v, axis=0, keepdims=True)
    idx = jnp.min(jnp.where(v == m, iota, size), axis=0, keepdims=True)
    return m, idx


def _router_kernel(x_ref, w_ref, b_ref, eidx_ref, gate_ref, pos_ref, cnt_ref, carry_ref):
    i = pl.program_id(0)
    tr = x_ref.shape[0]

    @pl.when(i == 0)
    def _():
        carry_ref[...] = jnp.zeros_like(carry_ref)

    x = x_ref[...]
    w = w_ref[...]
    xh = x.astype(BF16)
    xl = (x - xh.astype(F32)).astype(BF16)
    wh = w.astype(BF16)
    wl = (w - wh.astype(F32)).astype(BF16)
    logits = _dot_nt(wh, xh) + _dot_nt(wh, xl) + _dot_nt(wl, xh)
    scores = jax.nn.sigmoid(logits)
    biased = scores + b_ref[...]

    iota_g = lax.broadcasted_iota(jnp.int32, (GROUP_SIZE, tr), 0)
    grp = []
    for g in range(N_GROUPS):
        vg = biased[g * GROUP_SIZE:(g + 1) * GROUP_SIZE, :]
        m1, i1 = _first_argmax(vg, iota_g, GROUP_SIZE)
        m2 = jnp.max(jnp.where(iota_g == i1, -jnp.inf, vg), axis=0, keepdims=True)
        grp.append(m1 + m2)
    work = jnp.concatenate(grp, axis=0)
    iota_n = lax.broadcasted_iota(jnp.int32, (N_GROUPS, tr), 0)
    gsel = jnp.zeros((N_GROUPS, tr), F32)
    for _ in range(TOPK_GROUPS):
        _, gi = _first_argmax(work, iota_n, N_GROUPS)
        hit = iota_n == gi
        gsel = jnp.where(hit, 1.0, gsel)
        work = jnp.where(hit, -jnp.inf, work)
    emask = jnp.concatenate([jnp.broadcast_to(gsel[g:g + 1, :], (GROUP_SIZE, tr)) for g in range(N_GROUPS)], axis=0)
    work = jnp.where(emask > 0.5, biased, NEG_INF)

    iota_e = lax.broadcasted_iota(jnp.int32, (N_EXPERTS, tr), 0)
    hits, eids, gates = [], [], []
    chosen = jnp.zeros((N_EXPERTS, tr), F32)
    for _ in range(TOP_K):
        _, ei = _first_argmax(work, iota_e, N_EXPERTS)
        hit = iota_e == ei
        hits.append(hit)
        eids.append(ei)
        gates.append(jnp.sum(jnp.where(hit, scores, 0.0), axis=0, keepdims=True))
        chosen = jnp.where(hit, 1.0, chosen)
        work = jnp.where(hit, -jnp.inf, work)
    gsum = gates[0]
    for gk in gates[1:]:
        gsum = gsum + gk

    r = lax.broadcasted_iota(jnp.int32, (tr, tr), 0)
    c = lax.broadcasted_iota(jnp.int32, (tr, tr), 1)
    upper = jnp.where(r < c, 1.0, 0.0).astype(BF16)
    before = _dot(chosen.astype(BF16), upper) + carry_ref[...]
    zero_i = jnp.zeros((8 - TOP_K, tr), jnp.int32)
    eidx_ref[...] = jnp.concatenate(eids + [zero_i], axis=0)
    pos_ref[...] = jnp.concatenate(
        [jnp.sum(jnp.where(h, before, 0.0), axis=0, keepdims=True).astype(jnp.int32) for h in hits] + [zero_i], axis=0)
    gmat = jnp.concatenate([gk / gsum * ROUTE_SCALE for gk in gates] + [jnp.zeros((128 - TOP_K, tr), F32)], axis=0)
    g_hi = gmat.astype(BF16)
    g_lo = (gmat - g_hi.astype(F32)).astype(BF16)
    eye = jnp.where(r == c, 1.0, 0.0).astype(BF16)
    gate_ref[...] = _dot_nt(eye, g_hi) + _dot_nt(eye, g_lo)
    carry_ref[...] = carry_ref[...] + jnp.sum(chosen, axis=1, keepdims=True)
    cnt_ref[...] = jnp.broadcast_to(carry_ref[...], cnt_ref.shape)


def _router(x1, router_w, router_b):
    t = x1.shape[0]
    tr = ROUTER_TR
    col = pl.BlockSpec((8, tr), lambda i: (0, i))
    return pl.pallas_call(
        _router_kernel,
        grid=(t // tr,),
        in_specs=[pl.BlockSpec((tr, D_MODEL), lambda i: (i, 0)),
                  pl.BlockSpec((N_EXPERTS, D_MODEL), lambda i: (0, 0)),
                  pl.BlockSpec((N_EXPERTS, 1), lambda i: (0, 0))],
        out_specs=[col, pl.BlockSpec((tr, 128), lambda i: (i, 0)), col,
                   pl.BlockSpec((N_EXPERTS, 128), lambda i: (0, 0))],
        out_shape=[jax.ShapeDtypeStruct((8, t), jnp.int32), jax.ShapeDtypeStruct((t, 128), F32),
                   jax.ShapeDtypeStruct((8, t), jnp.int32), jax.ShapeDtypeStruct((N_EXPERTS, 128), F32)],
        scratch_shapes=[pltpu.VMEM((N_EXPERTS, 1), F32)],
        compiler_params=_params("arbitrary"),
        name="moe_router",
    )(x1, router_w, router_b.reshape(-1, 1))


DISPATCH_TD = 256


WAIT_UNROLL = 8


def _row_copy_loops(n_tok, row_copy):
    def issue(dest_ref):
        def body(t, carry):
            for k in range(TOP_K):
                row_copy(t, k, dest_ref[k, t]).start()
            return carry
        lax.fori_loop(0, n_tok, body, 0)

    def drain():
        def body(_, carry):
            for _ in range(WAIT_UNROLL):
                for k in range(TOP_K):
                    row_copy(0, k, 0).wait()
            return carry
        lax.fori_loop(0, n_tok // WAIT_UNROLL, body, 0)

    return issue, drain


DEST_TN = 2048


def _dest_kernel(pstart_ref, eidx_ref, pos_ref, o_ref):
    eidx = eidx_ref[...]
    dest = pos_ref[...]
    for e in range(N_EXPERTS):
        dest = dest + jnp.where(eidx == e, pstart_ref[e], 0)
    o_ref[...] = dest


def _dest_rows(pstart, eidx, pos):
    t = eidx.shape[1]
    tn = min(DEST_TN, t)
    col = pl.BlockSpec((8, tn), lambda i, ps: (0, i))
    grid_spec = pltpu.PrefetchScalarGridSpec(num_scalar_prefetch=1, grid=(t // tn,),
                                             in_specs=[col, col], out_specs=col)
    return pl.pallas_call(
        _dest_kernel,
        grid_spec=grid_spec,
        out_shape=jax.ShapeDtypeStruct((8, t), jnp.int32),
        compiler_params=_params("parallel"),
        name="moe_dest_rows",
    )(pstart, eidx, pos)


def _dispatch_kernel(dest_ref, x_ref, buf_hbm, o_hbm, sem):
    del buf_hbm

    def row_copy(t, k, row):
        return pltpu.make_async_copy(x_ref.at[pl.ds(t, 1)], o_hbm.at[pl.ds(row, 1)], sem)

    issue, drain = _row_copy_loops(DISPATCH_TD, row_copy)
    issue(dest_ref)
    drain()


def _dispatch(dest, x1p, rows):
    t, half = x1p.shape
    td = DISPATCH_TD
    return pl.pallas_call(
        _dispatch_kernel,
        grid=(t // td,),
        in_specs=[pl.BlockSpec((8, td), lambda i: (0, i), memory_space=pltpu.SMEM),
                  pl.BlockSpec((td, half), lambda i: (i, 0)),
                  pl.BlockSpec(memory_space=pl.ANY)],
        out_specs=pl.BlockSpec(memory_space=pl.ANY),
        out_shape=jax.ShapeDtypeStruct((rows, half), jnp.uint32),
        scratch_shapes=[pltpu.SemaphoreType.DMA],
        input_output_aliases={2: 0},
        compiler_params=_params("arbitrary"),
        name="moe_dispatch",
    )(dest, x1p, jnp.zeros((rows, half), jnp.uint32))


def _expert_kernel(be_ref, nu_ref, x_ref, wg_ref, wu_ref, wd_ref, o_ref, wgb_ref, wub_ref, wdb_ref):
    r = pl.program_id(0)

    @pl.when((r == 0) | (be_ref[r] != be_ref[jnp.maximum(r - 1, 0)]))
    def _():
        wgb_ref[...] = wg_ref[...].astype(BF16)
        wub_ref[...] = wu_ref[...].astype(BF16)
        wdb_ref[...] = wd_ref[...].astype(BF16)

    @pl.when(r < nu_ref[0])
    def _():
        xb = _unpack_halves(x_ref[...]).astype(BF16)
        h = _silu(_dot(xb, wgb_ref[...])) * _dot(xb, wub_ref[...])
        o_ref[...] = _pack_halves(_dot(h.astype(BF16), wdb_ref[...]))

    @pl.when(r >= nu_ref[0])
    def _():
        o_ref[...] = jnp.zeros_like(o_ref)


def _experts(blk_exp, n_used, xs, wg, wu, wd, layer):
    rows, half = xs.shape
    nblk = rows // MOE_BLK
    grid_spec = pltpu.PrefetchScalarGridSpec(
        num_scalar_prefetch=2,
        grid=(nblk,),
        in_specs=[pl.BlockSpec((MOE_BLK, half), lambda r, be, nu: (r, 0)),
                  pl.BlockSpec((None, None, D_MODEL, D_EXPERT), lambda r, be, nu: (layer, be[r], 0, 0)),
                  pl.BlockSpec((None, None, D_MODEL, D_EXPERT), lambda r, be, nu: (layer, be[r], 0, 0)),
                  pl.BlockSpec((None, None, D_EXPERT, D_MODEL), lambda r, be, nu: (layer, be[r], 0, 0))],
        out_specs=pl.BlockSpec((MOE_BLK, half), lambda r, be, nu: (r, 0)),
        scratch_shapes=[pltpu.VMEM((D_MODEL, D_EXPERT), BF16), pltpu.VMEM((D_MODEL, D_EXPERT), BF16),
                        pltpu.VMEM((D_EXPERT, D_MODEL), BF16)],
    )
    return pl.pallas_call(
        _expert_kernel,
        grid_spec=grid_spec,
        out_shape=jax.ShapeDtypeStruct((rows, half), jnp.uint32),
        compiler_params=_params("arbitrary"),
        name="moe_experts",
    )(blk_exp, n_used, xs, wg, wu, wd)


COMBINE_TC = 256


def _combine_kernel(dest_ref, x_ref, gate_ref, y_hbm, swg_ref, swu_ref, swd_ref, g_ref, b_ref, o_ref, buf_ref, sem):
    def row_copy(t, k, row):
        return pltpu.make_async_copy(y_hbm.at[pl.ds(row, 1)], buf_ref.at[k, pl.ds(t, 1)], sem)

    issue, drain = _row_copy_loops(COMBINE_TC, row_copy)
    issue(dest_ref)

    x = x_ref[...]
    xb = x.astype(BF16)
    h = _silu(_dot(xb, swg_ref[...])) * _dot(xb, swu_ref[...])
    acc = ALPHA * x + _dot(h.astype(BF16), swd_ref[...])

    drain()

    gates = gate_ref[...]
    for k in range(TOP_K):
        acc = acc + gates[:, k:k + 1] * _unpack_halves(buf_ref[k])
    o_ref[...] = _layer_norm(acc, g_ref[...], b_ref[...])


def _combine(dest, x1, gates_t, y_rows, swg, swu, swd, g, b):
    t = x1.shape[0]
    tc = COMBINE_TC
    half = y_rows.shape[1]

    def whole(arr):
        return pl.BlockSpec(arr.shape, lambda i: (0, 0))

    g2, b2 = g.reshape(1, -1), b.reshape(1, -1)
    return pl.pallas_call(
        _combine_kernel,
        grid=(t // tc,),
        in_specs=[pl.BlockSpec((8, tc), lambda i: (0, i), memory_space=pltpu.SMEM),
                  pl.BlockSpec((tc, D_MODEL), lambda i: (i, 0)),
                  pl.BlockSpec((tc, 128), lambda i: (i, 0)),
                  pl.BlockSpec(memory_space=pl.ANY),
                  whole(swg), whole(swu), whole(swd), whole(g2), whole(b2)],
        out_specs=pl.BlockSpec((tc, D_MODEL), lambda i: (i, 0)),
        out_shape=jax.ShapeDtypeStruct((t, D_MODEL), F32),
        scratch_shapes=[pltpu.VMEM((TOP_K, tc, half), jnp.uint32), pltpu.SemaphoreType.DMA],
        compiler_params=_params("arbitrary"),
        name="moe_combine_shared_ln",
    )(dest, x1, gates_t, y_rows, swg, swu, swd, g2, b2)


def _moe(x1, x1p, layer, router_w, router_b, wg, wu, wd, swg, swu, swd, g, b):
    t = x1.shape[0]
    eidx, gates, pos, cnt = _router(x1, router_w, router_b)
    counts = cnt[:, 0].astype(jnp.int32)
    padded = (counts + MOE_BLK - 1) // MOE_BLK * MOE_BLK
    pend = jnp.cumsum(padded)
    pstart = pend - padded
    n_blocks = (t * TOP_K + N_EXPERTS * (MOE_BLK - 1) + MOE_BLK - 1) // MOE_BLK
    blk_first = jnp.arange(n_blocks, dtype=jnp.int32) * MOE_BLK
    blk_exp = jnp.minimum(jnp.sum((pend[None, :] <= blk_first[:, None]).astype(jnp.int32), axis=1), N_EXPERTS - 1)
    n_used = (pend[-1:] // MOE_BLK).astype(jnp.int32)
    dest = _dest_rows(pstart, eidx, pos)
    xs = _dispatch(dest, x1p, n_blocks * MOE_BLK)
    y_rows = _experts(blk_exp, n_used, xs, wg, wu, wd, layer)
    return _combine(dest, x1, gates, y_rows, swg, swu, swd, g, b)


def _layer(x2d, bsz, seq, layer, w_in, moe_wg, moe_wu, moe_wd,
           conv_dw, conv_db, conv_ln_g, conv_ln_b, swa_sink, ret_gn_g, ret_gn_b,
           nsa_pe_k, nsa_w1_k, nsa_w2_k, nsa_pe_v, nsa_w1_v, nsa_w2_v, p_conv, p_swa, p_ret, p_nsa, w_out,
           ln1_g, ln1_b, router_w, router_b, sh_wg, sh_wu, sh_wd, ln2_g, ln2_b):
    bf = lambda a: a.astype(BF16)
    w_a = _mix_weights(w_in, layer)
    w_m = _merge_gate_weights(w_in, layer)
    ua, kcvc = _inproj(x2d, w_a)
    conv_a = _conv(ua, conv_dw, conv_db, conv_ln_g, conv_ln_b, bsz, seq)
    swa_o = _swa(ua, swa_sink, bsz, seq)
    ret_o = _retention(ua, ret_gn_g, ret_gn_b, bsz, seq)
    kcmp, vcmp = _compress(kcvc, nsa_pe_k, bf(nsa_w1_k), bf(nsa_w2_k), nsa_pe_v, bf(nsa_w1_v), bf(nsa_w2_v), bsz, seq)
    nsa_o = _nsa(ua, kcmp, vcmp, bsz, seq)
    x1, x1p = _merge(x2d, conv_a, swa_o, ret_o, nsa_o, w_m, bf(p_conv), bf(p_swa), bf(p_ret), bf(p_nsa),
                     bf(w_out), ln1_g, ln1_b)
    return _moe(x1, x1p, layer, router_w, router_b, moe_wg, moe_wu, moe_wd,
                bf(sh_wg), bf(sh_wu), bf(sh_wd), ln2_g, ln2_b)


def kernel(x, w_in, conv_dw, conv_db, conv_ln_g, conv_ln_b, swa_sink, ret_gn_g, ret_gn_b, nsa_pe_k, nsa_w1_k, nsa_w2_k, nsa_pe_v, nsa_w1_v, nsa_w2_v, p_conv, p_swa, p_ret, p_nsa, w_out, ln1_g, ln1_b, router_w, router_b, moe_wg, moe_wu, moe_wd, sh_wg, sh_wu, sh_wd, ln2_g, ln2_b):
    bsz, seq, d = x.shape
    params = (conv_dw, conv_db, conv_ln_g, conv_ln_b, swa_sink, ret_gn_g, ret_gn_b,
              nsa_pe_k, nsa_w1_k, nsa_w2_k, nsa_pe_v, nsa_w1_v, nsa_w2_v, p_conv, p_swa, p_ret, p_nsa, w_out,
              ln1_g, ln1_b, router_w, router_b, sh_wg, sh_wu, sh_wd, ln2_g, ln2_b)
    x2d = x.reshape(bsz * seq, d)
    for l in range(w_in.shape[0]):
        x2d = _layer(x2d, bsz, seq, l, w_in, moe_wg, moe_wu, moe_wd, *[p[l] for p in params])
    return x2d.reshape(bsz, seq, d)
```

```python
import functools
import math

import jax
import jax.numpy as jnp
import numpy as np
from jax import lax
from jax.experimental import pallas as pl
from jax.experimental.pallas import tpu as pltpu

F32 = jnp.float32
BF16 = jnp.bfloat16

D_MODEL = 1024
DEPTH = 2
HEAD_DIM = 64
BLK = 128
CONV_CH = 256
CONV_K = 31
SWA_HEADS = 4
SWA_KV_HEADS = 2
SWA_WINDOW = 128
RET_HEADS = 4
RET_QK = 64
RET_V = 128
NSA_HEADS = 4
NSA_CMP_LEN = 32
NSA_CMP_STRIDE = 16
NSA_CMP_HIDDEN = 256
NSA_SLC_LEN = 64
NSA_TOPN = 8
NSA_WINDOW = 512
NSA_FORCE = 1e4
N_EXPERTS = 64
TOP_K = 6
N_GROUPS = 8
GROUP_SIZE = N_EXPERTS // N_GROUPS
TOPK_GROUPS = 4
D_EXPERT = 256
ROUTE_SCALE = 2.5
MOE_BLK = 256
N_BRANCH = 4
ALPHA = (2 * DEPTH) ** 0.25
LN_EPS = 1e-5
NEG_INF = -1e30
SCALE = HEAD_DIM ** -0.5

C_CONV = 0
C_SWA = 512
C_RETQK = 1024
C_RETV = 1536
C_RETG = 2048
C_NSAQ = 2560
C_NSAC = 2816
C_NSAS = 2944
C_NSAW = 3072
C_NSAG = 3200
N_MIX_COLS = 3212
UA = 3328

ALIBI = [2.0 ** (-8.0 * (i + 1) / 8) for i in range(8)]
SLOPES_SWA = ALIBI[:4]
SLOPES_NSA = ALIBI[4:]
RET_LOG_GAMMA = [math.log1p(-(2.0 ** (-5.0 - h))) for h in range(RET_HEADS)]

VMEM_LIMIT = 48 * 1024 * 1024
NT_DIMS = (((1,), (1,)), ((), ()))
TN_DIMS = (((0,), (0,)), ((), ()))


def _params(*sem):
    return pltpu.CompilerParams(dimension_semantics=sem, vmem_limit_bytes=VMEM_LIMIT)


def _dot(a, b):
    return jnp.dot(a, b, preferred_element_type=F32)


def _dot_nt(a, b):
    return lax.dot_general(a, b, NT_DIMS, preferred_element_type=F32)


def _layer_norm(v, g, b):
    mu = jnp.mean(v, axis=-1, keepdims=True)
    c = v - mu
    var = jnp.mean(c * c, axis=-1, keepdims=True)
    return c * lax.rsqrt(var + LN_EPS) * g + b


def _silu(v):
    return v * jax.nn.sigmoid(v)


CAST_TM = 256
LANES = 128


def _cast_kernel(x_ref, o_ref):
    o_ref[...] = x_ref[...].astype(o_ref.dtype)


def _mix_weights(w_in, layer):
    return pl.pallas_call(
        _cast_kernel,
        grid=(D_MODEL // CAST_TM,),
        in_specs=[pl.BlockSpec((None, CAST_TM, UA), lambda i: (layer, i, 0))],
        out_specs=pl.BlockSpec((CAST_TM, UA), lambda i: (i, 0)),
        out_shape=jax.ShapeDtypeStruct((D_MODEL, UA), BF16),
        compiler_params=_params("parallel"),
        name="cast_mix_weights",
    )(w_in)


def _shifted_cast_kernel(a_ref, b_ref, o_ref):
    shift = N_MIX_COLS % LANES
    both = jnp.concatenate([a_ref[...], b_ref[...]], axis=1)
    o_ref[...] = both[:, shift:shift + LANES].astype(o_ref.dtype)


def _merge_gate_weights(w_in, layer):
    first = N_MIX_COLS // LANES
    width = N_BRANCH * D_MODEL
    return pl.pallas_call(
        _shifted_cast_kernel,
        grid=(width // LANES,),
        in_specs=[pl.BlockSpec((None, D_MODEL, LANES), lambda j: (layer, 0, first + j)),
                  pl.BlockSpec((None, D_MODEL, LANES), lambda j: (layer, 0, first + j + 1))],
        out_specs=pl.BlockSpec((D_MODEL, LANES), lambda j: (0, j)),
        out_shape=jax.ShapeDtypeStruct((D_MODEL, width), BF16),
        compiler_params=_params("parallel"),
        name="cast_merge_gate_weights",
    )(w_in, w_in)


INPROJ_TM = 512


def _inproj_kernel(x_ref, w_ref, o_ref, oc_ref):
    xb = x_ref[...].astype(BF16)
    for c in range(0, UA, 256):
        o_ref[:, c:c + 256] = _dot(xb, w_ref[:, c:c + 256]).astype(o_ref.dtype)
    oc_ref[...] = _dot(xb, w_ref[:, C_NSAC:C_NSAC + 128])


def _inproj(x2d, w_a):
    t = x2d.shape[0]
    return pl.pallas_call(
        _inproj_kernel,
        grid=(t // INPROJ_TM,),
        in_specs=[pl.BlockSpec((INPROJ_TM, D_MODEL), lambda i: (i, 0)),
                  pl.BlockSpec((D_MODEL, UA), lambda i: (0, 0))],
        out_specs=[pl.BlockSpec((INPROJ_TM, UA), lambda i: (i, 0)),
                   pl.BlockSpec((INPROJ_TM, 128), lambda i: (i, 0))],
        out_shape=[jax.ShapeDtypeStruct((t, UA), BF16), jax.ShapeDtypeStruct((t, 128), F32)],
        compiler_params=_params("parallel"),
        name="inproj",
    )(x2d, w_a)


CONV_TS = 256
CONV_HALO = 32


def _conv_kernel(cur_ref, halo_ref, dw_ref, db_ref, g_ref, b_ref, o_ref, hc_ref):
    i = pl.program_id(1)

    def glu(u):
        u = u.astype(F32)
        return u[:, :CONV_CH] * jax.nn.sigmoid(u[:, CONV_CH:])

    hc_ref[0:CONV_HALO, :] = jnp.where(i > 0, glu(halo_ref[...]), 0.0)
    hc_ref[CONV_HALO:CONV_HALO + CONV_TS, :] = glu(cur_ref[...])
    acc = jnp.broadcast_to(db_ref[...], (CONV_TS, CONV_CH))
    off = CONV_HALO - (CONV_K - 1)
    for j in range(CONV_K):
        acc = acc + hc_ref[off + j:off + j + CONV_TS, :] * dw_ref[j:j + 1, :]
    o_ref[...] = _silu(_layer_norm(acc, g_ref[...], b_ref[...])).astype(o_ref.dtype)


def _conv(ua, dw, db, g, b, bsz, seq):
    t = bsz * seq
    ns = seq // CONV_TS
    per = CONV_TS // CONV_HALO
    dwp = jnp.concatenate([dw, jnp.zeros((32 - CONV_K, CONV_CH), F32)], axis=0)
    vec = pl.BlockSpec((1, CONV_CH), lambda bb, i: (0, 0))
    return pl.pallas_call(
        _conv_kernel,
        grid=(bsz, ns),
        in_specs=[pl.BlockSpec((CONV_TS, 512), lambda bb, i: (bb * ns + i, C_CONV // 512)),
                  pl.BlockSpec((CONV_HALO, 512),
                               lambda bb, i: (jnp.maximum((bb * ns + i) * per - 1, 0), C_CONV // 512)),
                  pl.BlockSpec((32, CONV_CH), lambda bb, i: (0, 0)), vec, vec, vec],
        out_specs=pl.BlockSpec((CONV_TS, CONV_CH), lambda bb, i: (bb * ns + i, 0)),
        out_shape=jax.ShapeDtypeStruct((t, CONV_CH), BF16),
        scratch_shapes=[pltpu.VMEM((CONV_HALO + CONV_TS, CONV_CH), F32)],
        compiler_params=_params("parallel", "parallel"),
        name="conformer_conv",
    )(ua, ua, dwp, db.reshape(1, -1), g.reshape(1, -1), b.reshape(1, -1))


def _swa_kernel(sink_ref, q_ref, kvc_ref, kvp_ref, o_ref):
    i = pl.program_id(1)
    q = q_ref[...]
    kvc = kvc_ref[...]
    kvp = kvp_ref[...]
    a = lax.broadcasted_iota(jnp.int32, (BLK, 2 * BLK), 0)
    c = lax.broadcasted_iota(jnp.int32, (BLK, 2 * BLK), 1)
    dist = BLK + a - c
    mask = (dist >= 0) & (dist < SWA_WINDOW) & ((c >= BLK) | (i > 0))
    distf = dist.astype(F32)
    rep = SWA_HEADS // SWA_KV_HEADS
    for h in range(SWA_HEADS):
        g = h // rep
        k = jnp.concatenate([kvp[:, g * HEAD_DIM:(g + 1) * HEAD_DIM],
                             kvc[:, g * HEAD_DIM:(g + 1) * HEAD_DIM]], axis=0)
        v = jnp.concatenate([kvp[:, BLK + g * HEAD_DIM:BLK + (g + 1) * HEAD_DIM],
                             kvc[:, BLK + g * HEAD_DIM:BLK + (g + 1) * HEAD_DIM]], axis=0)
        s = _dot_nt(q[:, h * HEAD_DIM:(h + 1) * HEAD_DIM], k) * SCALE - SLOPES_SWA[h] * distf
        s = jnp.where(mask, s, NEG_INF)
        sink = sink_ref[h]
        m = jnp.maximum(jnp.max(s, axis=-1, keepdims=True), sink)
        e = jnp.where(mask, jnp.exp(s - m), 0.0)
        den = jnp.sum(e, axis=-1, keepdims=True) + jnp.exp(sink - m)
        p = e / jnp.maximum(den, 1e-30)
        o_ref[:, h * HEAD_DIM:(h + 1) * HEAD_DIM] = _dot(p.astype(BF16), v).astype(o_ref.dtype)


def _swa(ua, sink, bsz, seq):
    t = bsz * seq
    nb = seq // BLK
    return pl.pallas_call(
        _swa_kernel,
        grid=(bsz, nb),
        in_specs=[pl.BlockSpec(memory_space=pltpu.SMEM),
                  pl.BlockSpec((BLK, 256), lambda bb, i: (bb * nb + i, C_SWA // 256)),
                  pl.BlockSpec((BLK, 256), lambda bb, i: (bb * nb + i, C_SWA // 256 + 1)),
                  pl.BlockSpec((BLK, 256), lambda bb, i: (jnp.maximum(bb * nb + i - 1, 0), C_SWA // 256 + 1))],
        out_specs=pl.BlockSpec((BLK, 256), lambda bb, i: (bb * nb + i, 0)),
        out_shape=jax.ShapeDtypeStruct((t, SWA_HEADS * HEAD_DIM), BF16),
        compiler_params=_params("parallel", "parallel"),
        name="swa",
    )(sink, ua, ua, ua)


def _ret_kernel(qk_ref, v_ref, g_ref, gng_ref, gnb_ref, o_ref, state_ref):
    n = pl.program_id(1)

    @pl.when(n == 0)
    def _():
        state_ref[...] = jnp.zeros_like(state_ref)

    qk = qk_ref[...]
    vv = v_ref[...]
    gate = g_ref[...].astype(F32)
    a = lax.broadcasted_iota(jnp.int32, (BLK, BLK), 0)
    c = lax.broadcasted_iota(jnp.int32, (BLK, BLK), 1)
    diff = (a - c).astype(F32)
    idx = lax.broadcasted_iota(jnp.int32, (BLK, 1), 0).astype(F32)
    hq = RET_HEADS * RET_QK
    for h in range(RET_HEADS):
        lg = RET_LOG_GAMMA[h]
        q = qk[:, h * RET_QK:(h + 1) * RET_QK]
        k = qk[:, hq + h * RET_QK:hq + (h + 1) * RET_QK].astype(F32) * (RET_QK ** -0.5)
        v = vv[:, h * RET_V:(h + 1) * RET_V]
        dmat = jnp.where(diff >= 0, jnp.exp(lg * jnp.maximum(diff, 0.0)), 0.0)
        inner = _dot_nt(q, k.astype(BF16)) * dmat
        o_inner = _dot(inner.astype(BF16), v)
        zeta = jnp.exp(lg * (BLK - 1 - idx))
        xi = jnp.exp(lg * (idx + 1.0))
        kv = lax.dot_general((k * zeta).astype(BF16), v, TN_DIMS, preferred_element_type=F32)
        prev = state_ref[h]
        o = o_inner + _dot(q, prev.astype(BF16)) * xi
        state_ref[h] = math.exp(lg * BLK) * prev + kv
        mu = jnp.mean(o, axis=-1, keepdims=True)
        cen = o - mu
        var = jnp.mean(cen * cen, axis=-1, keepdims=True)
        sl = slice(h * RET_V, (h + 1) * RET_V)
        on = cen * lax.rsqrt(var + LN_EPS) * gng_ref[:, sl] + gnb_ref[:, sl]
        o_ref[:, sl] = (_silu(gate[:, sl]) * on).astype(o_ref.dtype)


def _retention(ua, gn_g, gn_b, bsz, seq):
    t = bsz * seq
    nc = seq // BLK
    vec = pl.BlockSpec((1, RET_HEADS * RET_V), lambda bb, i: (0, 0))
    return pl.pallas_call(
        _ret_kernel,
        grid=(bsz, nc),
        in_specs=[pl.BlockSpec((BLK, 512), lambda bb, i: (bb * nc + i, C_RETQK // 512)),
                  pl.BlockSpec((BLK, 512), lambda bb, i: (bb * nc + i, C_RETV // 512)),
                  pl.BlockSpec((BLK, 512), lambda bb, i: (bb * nc + i, C_RETG // 512)),
                  vec, vec],
        out_specs=pl.BlockSpec((BLK, 512), lambda bb, i: (bb * nc + i, 0)),
        out_shape=jax.ShapeDtypeStruct((t, RET_HEADS * RET_V), BF16),
        scratch_shapes=[pltpu.VMEM((RET_HEADS, RET_QK, RET_V), F32)],
        compiler_params=_params("parallel", "arbitrary"),
        name="retention",
    )(ua, ua, ua, gn_g.reshape(1, -1), gn_b.reshape(1, -1))


def _compress_kernel(t_ref, pek_ref, w1k_ref, w2k_ref, pev_ref, w1v_ref, w2v_ref, ok_ref, ov_ref):
    st = NSA_CMP_STRIDE
    ng = t_ref.shape[0] // st
    branches = ((pek_ref, w1k_ref, w2k_ref, ok_ref), (pev_ref, w1v_ref, w2v_ref, ov_ref))
    first = [jnp.zeros((ng, NSA_CMP_HIDDEN), F32) for _ in branches]
    second = [jnp.zeros((ng, NSA_CMP_HIDDEN), F32) for _ in branches]
    for r in range(st):
        rows = t_ref[pl.ds(r, ng, stride=st), :]
        for n, (pe_ref, w1_ref, _, _) in enumerate(branches):
            v = rows[:, n * HEAD_DIM:(n + 1) * HEAD_DIM]
            lo = (v + pe_ref[r:r + 1, :]).astype(BF16)
            hi = (v + pe_ref[st + r:st + r + 1, :]).astype(BF16)
            first[n] = first[n] + _dot(lo, w1_ref[r * HEAD_DIM:(r + 1) * HEAD_DIM, :])
            second[n] = second[n] + _dot(hi, w1_ref[(st + r) * HEAD_DIM:(st + r + 1) * HEAD_DIM, :])
    for n, (_, _, w2_ref, o_ref) in enumerate(branches):
        hid = first[n] + pltpu.roll(second[n], ng - 1, axis=0)
        act = 0.5 * hid * (1.0 + lax.erf(hid * (2.0 ** -0.5)))
        o_ref[0] = _dot(act.astype(BF16), w2_ref[...]).astype(o_ref.dtype)


def _compress(kcvc, pe_k, w1_k, w2_k, pe_v, w1_v, w2_v, bsz, seq):
    ng = seq // NSA_CMP_STRIDE
    pes = pl.BlockSpec((NSA_CMP_LEN, HEAD_DIM), lambda bb: (0, 0))
    w1s = pl.BlockSpec((NSA_CMP_LEN * HEAD_DIM, NSA_CMP_HIDDEN), lambda bb: (0, 0))
    w2s = pl.BlockSpec((NSA_CMP_HIDDEN, HEAD_DIM), lambda bb: (0, 0))
    osp = pl.BlockSpec((1, ng, HEAD_DIM), lambda bb: (bb, 0, 0))
    return pl.pallas_call(
        _compress_kernel,
        grid=(bsz,),
        in_specs=[pl.BlockSpec((seq, 128), lambda bb: (bb, 0)), pes, w1s, w2s, pes, w1s, w2s],
        out_specs=[osp, osp],
        out_shape=[jax.ShapeDtypeStruct((bsz, ng, HEAD_DIM), BF16)] * 2,
        compiler_params=_params("parallel"),
        name="nsa_compress",
    )(kcvc, pe_k, w1_k, w2_k, pe_v, w1_v, w2_v)


NSA_KC = 512
NSA_WSPAN = NSA_WINDOW + BLK


NSA_PAD = 128
Q_LANES = NSA_HEADS * BLK


def _nsa_query_select():
    sel = np.zeros((NSA_HEADS * HEAD_DIM, NSA_HEADS * NSA_PAD), np.float32)
    for h in range(NSA_HEADS):
        for dd in range(HEAD_DIM):
            sel[h * HEAD_DIM + dd, h * NSA_PAD + dd] = SCALE
    return jnp.asarray(sel, BF16)


def _softmax_down(s):
    m = jnp.maximum(jnp.max(s, axis=0, keepdims=True), -1e29)
    e = jnp.exp(s - m)
    return e / jnp.maximum(jnp.sum(e, axis=0, keepdims=True), 1e-30)


def _nsa_kernel(q_ref, qsel_ref, kcmp_ref, vcmp_ref, ksvs_ref, kwvw_ref, gate_ref, o_ref,
                ksa_ref, kwa_ref, kca_ref, vst_ref, vwt_ref, vct_ref, ext_ref, bsel_ref):
    i = pl.program_id(1)
    seq = ksvs_ref.shape[0]
    ncmp = kcmp_ref.shape[1]
    nslc = seq // NSA_SLC_LEN
    lane = lax.broadcasted_iota(jnp.int32, (1, NSA_PAD), 1)

    @pl.when(i == 0)
    def _():
        def aug(k, pos):
            kf = jnp.concatenate([k.astype(F32), jnp.zeros((k.shape[0], NSA_PAD - HEAD_DIM), F32)], axis=1)
            hi = (pos // 128).astype(F32)
            lo = (pos % 128).astype(F32)
            return (kf + jnp.where(lane == HEAD_DIM, hi, jnp.where(lane == HEAD_DIM + 1, lo, 0.0))).astype(BF16)

        def value_t(v):
            vf = jnp.concatenate([v.astype(F32), jnp.zeros((v.shape[0], NSA_PAD - HEAD_DIM), F32)], axis=1)
            return vf.T.astype(BF16)

        kpos = lax.broadcasted_iota(jnp.int32, (seq, 1), 0)
        ksvs = ksvs_ref[...]
        kwvw = kwvw_ref[...]
        ksa_ref[...] = aug(ksvs[:, :HEAD_DIM], kpos)
        kwa_ref[...] = aug(kwvw[:, :HEAD_DIM], kpos)
        cend = lax.broadcasted_iota(jnp.int32, (ncmp, 1), 0) * NSA_CMP_STRIDE + (NSA_CMP_LEN - 1)
        kca_ref[...] = aug(kcmp_ref[0], cend)
        vst_ref[...] = value_t(ksvs[:, HEAD_DIM:])
        vwt_ref[...] = value_t(kwvw[:, HEAD_DIM:])
        vct_ref[...] = value_t(vcmp_ref[0])
        ext_ref[...] = jnp.where(kpos // NSA_SLC_LEN == lax.broadcasted_iota(jnp.int32, (seq, NSA_PAD), 1),
                                 1.0, 0.0).astype(BF16)

    t0 = i * BLK
    tq = t0 + lax.broadcasted_iota(jnp.int32, (1, BLK), 1)
    qw = _dot(q_ref[...], qsel_ref[...])
    qaug = []
    for h in range(NSA_HEADS):
        coef = jnp.where(lane == HEAD_DIM, 128.0 * SLOPES_NSA[h], jnp.where(lane == HEAD_DIM + 1, SLOPES_NSA[h], 0.0))
        qaug.append((qw[:, h * NSA_PAD:(h + 1) * NSA_PAD] + coef).astype(BF16))
    qaug = jnp.concatenate(qaug, axis=0)

    def tile4(b):
        return jnp.concatenate([b] * NSA_HEADS, axis=1)

    cend = lax.broadcasted_iota(jnp.int32, (ncmp, 1), 0) * NSA_CMP_STRIDE + (NSA_CMP_LEN - 1)
    p_cmp = _softmax_down(_dot_nt(kca_ref[...], qaug) + tile4(jnp.where(cend <= tq, 0.0, NEG_INF)))
    o_cmp = _dot(vct_ref[...], p_cmp.astype(BF16))
    psum = p_cmp[:, 0:BLK]
    for h in range(1, NSA_HEADS):
        psum = psum + p_cmp[:, h * BLK:(h + 1) * BLK]

    sst = lax.broadcasted_iota(jnp.int32, (nslc, ncmp), 0) * NSA_SLC_LEN
    cst = lax.broadcasted_iota(jnp.int32, (nslc, ncmp), 1) * NSA_CMP_STRIDE
    overlap = jnp.where((cst < sst + NSA_SLC_LEN) & (cst + NSA_CMP_LEN > sst), 1.0, 0.0).astype(BF16)
    p_hi = psum.astype(BF16)
    p_lo = (psum - p_hi.astype(F32)).astype(BF16)
    imp = _dot(overlap, p_hi) + _dot(overlap, p_lo)
    jj = lax.broadcasted_iota(jnp.int32, (nslc, BLK), 0)
    cur = tq // NSA_SLC_LEN
    forced = (jj == 0) | (jj == cur) | (jj == cur - 1)
    imp = jnp.where(forced, NSA_FORCE, jnp.where(jj > cur, -1.0, imp))

    sel = jnp.zeros((nslc, BLK), F32)
    for _ in range(min(NSA_TOPN, nslc)):
        mx = jnp.max(imp, axis=0, keepdims=True)
        first = jnp.min(jnp.where(imp == mx, jj, nslc), axis=0, keepdims=True)
        hit = jj == first
        sel = jnp.where(hit, 1.0, sel)
        imp = jnp.where(hit, -3e38, imp)
    sel = jnp.concatenate([sel, jnp.zeros((NSA_PAD - nslc, BLK), F32)], axis=0).astype(BF16)
    bsel_ref[...] = (_dot(ext_ref[...], sel) - 1.0) * 1e30

    def slc_step(c, carry):
        m, l, acc = carry
        start = pl.multiple_of(c * NSA_KC, NSA_KC)
        kpos = start + lax.broadcasted_iota(jnp.int32, (NSA_KC, 1), 0)
        bias = bsel_ref[pl.ds(start, NSA_KC), :] + jnp.where(kpos <= tq, 0.0, NEG_INF)
        s = _dot_nt(ksa_ref[pl.ds(start, NSA_KC), :], qaug) + tile4(bias)
        m_new = jnp.maximum(m, jnp.max(s, axis=0, keepdims=True))
        alpha = jnp.exp(m - m_new)
        p = jnp.exp(s - m_new)
        l_new = alpha * l + jnp.sum(p, axis=0, keepdims=True)
        acc_new = alpha * acc + _dot(vst_ref[:, pl.ds(start, NSA_KC)], p.astype(BF16))
        return m_new, l_new, acc_new

    init = (jnp.full((1, Q_LANES), -1e29, F32), jnp.zeros((1, Q_LANES), F32), jnp.zeros((NSA_PAD, Q_LANES), F32))
    _, l_slc, acc_slc = lax.fori_loop(0, (t0 + BLK - 1) // NSA_KC + 1, slc_step, init)
    o_slc = acc_slc / jnp.maximum(l_slc, 1e-30)

    wspan = min(NSA_WSPAN, seq)
    wstart = pl.multiple_of(jnp.maximum(t0 + BLK - wspan, 0), BLK)
    wdist = tq - (wstart + lax.broadcasted_iota(jnp.int32, (wspan, 1), 0))
    wbias = jnp.where((wdist >= 0) & (wdist < NSA_WINDOW), 0.0, NEG_INF)
    p_win = _softmax_down(_dot_nt(kwa_ref[pl.ds(wstart, wspan), :], qaug) + tile4(wbias))
    o_win = _dot(vwt_ref[:, pl.ds(wstart, wspan)], p_win.astype(BF16))

    gts = jax.nn.sigmoid(gate_ref[...].astype(F32)).T
    outs = []
    for h in range(NSA_HEADS):
        cols = slice(h * BLK, (h + 1) * BLK)
        outs.append(gts[h:h + 1, :] * o_cmp[:HEAD_DIM, cols]
                    + gts[NSA_HEADS + h:NSA_HEADS + h + 1, :] * o_slc[:HEAD_DIM, cols]
                    + gts[2 * NSA_HEADS + h:2 * NSA_HEADS + h + 1, :] * o_win[:HEAD_DIM, cols])
    o_ref[...] = jnp.concatenate(outs, axis=0).T.astype(o_ref.dtype)


def _nsa(ua, kcmp, vcmp, bsz, seq):
    t = bsz * seq
    nq = seq // BLK
    ncmp = kcmp.shape[1]
    qsel = _nsa_query_select()
    staged_k = pltpu.VMEM((seq, NSA_PAD), BF16)
    staged_v = pltpu.VMEM((NSA_PAD, seq), BF16)
    return pl.pallas_call(
        _nsa_kernel,
        grid=(bsz, nq),
        in_specs=[pl.BlockSpec((BLK, 256), lambda bb, i: (bb * nq + i, C_NSAQ // 256)),
                  pl.BlockSpec(qsel.shape, lambda bb, i: (0, 0)),
                  pl.BlockSpec((1, ncmp, HEAD_DIM), lambda bb, i: (bb, 0, 0)),
                  pl.BlockSpec((1, ncmp, HEAD_DIM), lambda bb, i: (bb, 0, 0)),
                  pl.BlockSpec((seq, 128), lambda bb, i: (bb, C_NSAS // 128)),
                  pl.BlockSpec((seq, 128), lambda bb, i: (bb, C_NSAW // 128)),
                  pl.BlockSpec((BLK, 128), lambda bb, i: (bb * nq + i, C_NSAG // 128))],
        out_specs=pl.BlockSpec((BLK, 256), lambda bb, i: (bb * nq + i, 0)),
        out_shape=jax.ShapeDtypeStruct((t, NSA_HEADS * HEAD_DIM), BF16),
        scratch_shapes=[staged_k, staged_k, pltpu.VMEM((ncmp, NSA_PAD), BF16),
                        staged_v, staged_v, pltpu.VMEM((NSA_PAD, ncmp), BF16),
                        staged_k, pltpu.VMEM((seq, BLK), F32)],
        compiler_params=_params("parallel", "arbitrary"),
        name="nsa_attention",
    )(ua, qsel, kcmp, vcmp, ua, ua, ua)


MERGE_TM = 512


def _merge_kernel(x_ref, conv_ref, swa_ref, ret_ref, nsa_ref, wm_ref, pc_ref, ps_ref, pr_ref, pn_ref,
                  wo_ref, g_ref, b_ref, o_ref, op_ref):
    x = x_ref[...]
    xb = x.astype(BF16)
    merged = None
    for n, (a_ref, p_ref) in enumerate(((conv_ref, pc_ref), (swa_ref, ps_ref), (ret_ref, pr_ref), (nsa_ref, pn_ref))):
        gate = jax.nn.sigmoid(_dot(xb, wm_ref[:, n * D_MODEL:(n + 1) * D_MODEL]))
        term = gate * _dot(a_ref[...], p_ref[...])
        merged = term if merged is None else merged + term
    mix = _dot(merged.astype(BF16), wo_ref[...])
    y = _layer_norm(ALPHA * x + mix, g_ref[...], b_ref[...])
    o_ref[...] = y
    op_ref[...] = y.astype(BF16)


def _merge(x2d, conv_a, swa_o, ret_o, nsa_o, w_m, p_conv, p_swa, p_ret, p_nsa, w_out, g, b):
    t = x2d.shape[0]
    tm = min(MERGE_TM, t)

    def rows(width):
        return pl.BlockSpec((tm, width), lambda i: (i, 0))

    def whole(arr):
        return pl.BlockSpec(arr.shape, lambda i: (0, 0), pipeline_mode=pl.Buffered(1))

    g2, b2 = g.reshape(1, -1), b.reshape(1, -1)
    return pl.pallas_call(
        _merge_kernel,
        grid=(t // tm,),
        in_specs=[rows(D_MODEL), rows(256), rows(256), rows(512), rows(256),
                  whole(w_m), whole(p_conv), whole(p_swa), whole(p_ret), whole(p_nsa), whole(w_out),
                  whole(g2), whole(b2)],
        out_specs=[rows(D_MODEL), rows(D_MODEL)],
        out_shape=[jax.ShapeDtypeStruct((t, D_MODEL), F32), jax.ShapeDtypeStruct((t, D_MODEL), BF16)],
        compiler_params=_params("parallel"),
        name="merge_outproj_ln",
    )(x2d, conv_a, swa_o, ret_o, nsa_o, w_m, p_conv, p_swa, p_ret, p_nsa, w_out, g2, b2)


MOE_TT = 256
RUN_ALIGN = 16
MOE_LR = 2560
MOE_CHUNK = 256
RUN_PIECES = (256, 128, 64, 32, 16)
ROW_W = D_MODEL + 128
assert MOE_LR >= TOP_K * MOE_TT + N_EXPERTS * (RUN_ALIGN - 1) and MOE_LR % MOE_CHUNK == 0


def _first_argmax(v, iota, size):
    m = jnp.max(v, axis=0, keepdims=True)
    idx = jnp.min(jnp.where(v == m, iota, size), axis=0, keepdims=True)
    return m, idx


def _router_kernel(x_ref, w_ref, b_ref, lp_ref, lpt_ref, gate_ref, tab_ref, cnt_ref, carry_ref):
    i = pl.program_id(0)
    tr = x_ref.shape[0]

    @pl.when(i == 0)
    def _():
        carry_ref[...] = jnp.zeros_like(carry_ref)

    x = x_ref[...]
    w = w_ref[...]
    xh = x.astype(BF16)
    xl = (x - xh.astype(F32)).astype(BF16)
    wh = w.astype(BF16)
    wl = (w - wh.astype(F32)).astype(BF16)
    logits = _dot_nt(wh, xh) + _dot_nt(wh, xl) + _dot_nt(wl, xh)
    scores = jax.nn.sigmoid(logits)
    biased = scores + b_ref[...]

    iota_g = lax.broadcasted_iota(jnp.int32, (GROUP_SIZE, tr), 0)
    grp = []
    for g in range(N_GROUPS):
        vg = biased[g * GROUP_SIZE:(g + 1) * GROUP_SIZE, :]
        m1, i1 = _first_argmax(vg, iota_g, GROUP_SIZE)
        m2 = jnp.max(jnp.where(iota_g == i1, -jnp.inf, vg), axis=0, keepdims=True)
        grp.append(m1 + m2)
    work = jnp.concatenate(grp, axis=0)
    iota_n = lax.broadcasted_iota(jnp.int32, (N_GROUPS, tr), 0)
    gsel = jnp.zeros((N_GROUPS, tr), F32)
    for _ in range(TOPK_GROUPS):
        _, gi = _first_argmax(work, iota_n, N_GROUPS)
        hit = iota_n == gi
        gsel = jnp.where(hit, 1.0, gsel)
        work = jnp.where(hit, -jnp.inf, work)
    emask = jnp.concatenate([jnp.broadcast_to(gsel[g:g + 1, :], (GROUP_SIZE, tr)) for g in range(N_GROUPS)], axis=0)
    work = jnp.where(emask > 0.5, biased, NEG_INF)

    iota_e = lax.broadcasted_iota(jnp.int32, (N_EXPERTS, tr), 0)
    hits, gates = [], []
    chosen = jnp.zeros((N_EXPERTS, tr), F32)
    for _ in range(TOP_K):
        _, ei = _first_argmax(work, iota_e, N_EXPERTS)
        hit = iota_e == ei
        hits.append(hit)
        gates.append(jnp.sum(jnp.where(hit, scores, 0.0), axis=0, keepdims=True))
        chosen = jnp.where(hit, 1.0, chosen)
        work = jnp.where(hit, -jnp.inf, work)
    gsum = gates[0]
    for gk in gates[1:]:
        gsum = gsum + gk
    gate_ref[...] = jnp.concatenate([gk / gsum * ROUTE_SCALE for gk in gates] + [jnp.zeros((8 - TOP_K, tr), F32)], axis=0)

    r = lax.broadcasted_iota(jnp.int32, (tr, tr), 0)
    c = lax.broadcasted_iota(jnp.int32, (tr, tr), 1)
    rank = _dot(chosen.astype(BF16), jnp.where(r < c, 1.0, 0.0).astype(BF16))
    run_len = jnp.ceil(jnp.sum(chosen, axis=1, keepdims=True) * (1.0 / RUN_ALIGN)) * RUN_ALIGN
    er = lax.broadcasted_iota(jnp.int32, (N_EXPERTS, N_EXPERTS), 0)
    ec = lax.broadcasted_iota(jnp.int32, (N_EXPERTS, N_EXPERTS), 1)
    units = jnp.broadcast_to(run_len * (1.0 / RUN_ALIGN), (N_EXPERTS, 128)).astype(BF16)
    run_off = _dot(jnp.where(ec < er, 1.0, 0.0).astype(BF16), units)[:, :1] * RUN_ALIGN
    local = [jnp.sum(jnp.where(h, run_off + rank, 0.0), axis=0, keepdims=True) for h in hits]
    lp = jnp.concatenate(local + [jnp.zeros((8 - TOP_K, tr), F32)], axis=0)
    lp_ref[...] = lp.astype(jnp.int32)

    lp128 = jnp.concatenate([lp, jnp.zeros((128 - 8, tr), F32)], axis=0)
    hi = jnp.floor(lp128 * (1.0 / 64.0))
    lo = lp128 - 64.0 * hi
    eye = jnp.where(r == c, 1.0, 0.0).astype(BF16)
    lpt_ref[...] = 64.0 * _dot_nt(eye, hi.astype(BF16)) + _dot_nt(eye, lo.astype(BF16))

    def as_row(col):
        row = jnp.sum(jnp.where(er == ec, jnp.broadcast_to(col, (N_EXPERTS, N_EXPERTS)), 0.0), axis=0, keepdims=True)
        return jnp.concatenate([row, jnp.zeros((1, 128 - N_EXPERTS), F32)], axis=1)

    tab = jnp.concatenate([as_row(run_len), as_row(run_off), as_row(carry_ref[...]), jnp.zeros((5, 128), F32)], axis=0)
    tab_ref[...] = tab.astype(jnp.int32)
    carry_ref[...] = carry_ref[...] + run_len
    cnt_ref[...] = jnp.broadcast_to(carry_ref[...], cnt_ref.shape)


def _router(x1, router_w, router_b):
    t = x1.shape[0]
    tr = MOE_TT
    nt = t // tr
    col = pl.BlockSpec((8, tr), lambda i: (0, i))
    return pl.pallas_call(
        _router_kernel,
        grid=(nt,),
        in_specs=[pl.BlockSpec((tr, D_MODEL), lambda i: (i, 0)),
                  pl.BlockSpec((N_EXPERTS, D_MODEL), lambda i: (0, 0)),
                  pl.BlockSpec((N_EXPERTS, 1), lambda i: (0, 0))],
        out_specs=[col, pl.BlockSpec((tr, 128), lambda i: (i, 0)), col,
                   pl.BlockSpec((None, 8, 128), lambda i: (i, 0, 0)),
                   pl.BlockSpec((N_EXPERTS, 128), lambda i: (0, 0))],
        out_shape=[jax.ShapeDtypeStruct((8, t), jnp.int32), jax.ShapeDtypeStruct((t, 128), F32),
                   jax.ShapeDtypeStruct((8, t), F32), jax.ShapeDtypeStruct((nt, 8, 128), jnp.int32),
                   jax.ShapeDtypeStruct((N_EXPERTS, 128), F32)],
        scratch_shapes=[pltpu.VMEM((N_EXPERTS, 1), F32)],
        compiler_params=_params("arbitrary"),
        name="moe_router",
    )(x1, router_w, router_b.reshape(-1, 1))


def _run_copies(pstart_ref, tab_ref, make_copy, action):
    def per_expert(e, carry):
        length = tab_ref[0, e]
        local = tab_ref[1, e]
        glob = pstart_ref[e] + tab_ref[2, e]
        done = 0
        for size in RUN_PIECES:
            piece = length & size

            @pl.when(piece != 0)
            def _():
                cp = make_copy(pl.multiple_of(local + done, RUN_ALIGN), pl.multiple_of(glob + done, RUN_ALIGN), size)
                cp.start() if action == "start" else cp.wait()

            done = done + piece
        return carry

    lax.fori_loop(0, N_EXPERTS, per_expert, 0)


def _dispatch_kernel(pstart_ref, tail_ref, taillen_ref, nused_ref, tab_ref, lp_ref, gate_ref, x_ref, o_hbm,
                     buf_ref, sem):
    i = pl.program_id(0)

    def make_copy(local, glob, size):
        return pltpu.make_async_copy(buf_ref.at[pl.ds(local, size)], o_hbm.at[pl.ds(glob, size)], sem)

    @pl.when(i == 0)
    def _():
        buf_ref[0:MOE_BLK, :] = jnp.zeros((MOE_BLK, ROW_W), BF16)
        for action in ("start", "wait"):
            def per_expert(e, carry, action=action):
                done = 0
                for size in RUN_PIECES:
                    piece = taillen_ref[e] & size

                    @pl.when(piece != 0)
                    def _():
                        cp = make_copy(0, pl.multiple_of(tail_ref[e] + done, RUN_ALIGN), size)
                        cp.start() if action == "start" else cp.wait()

                    done = done + piece
                return carry
            lax.fori_loop(0, N_EXPERTS, per_expert, 0)

            def per_block(r, carry, action=action):
                cp = make_copy(0, pl.multiple_of(r * MOE_BLK, MOE_BLK), MOE_BLK)
                cp.start() if action == "start" else cp.wait()
                return carry
            lax.fori_loop(nused_ref[0], o_hbm.shape[0] // MOE_BLK, per_block, 0)

    lp = lp_ref[...]
    gates = gate_ref[...]
    xb = x_ref[...]
    glane = lax.broadcasted_iota(jnp.int32, (1, ROW_W - D_MODEL), 1)
    for c in range(MOE_LR // MOE_CHUNK):
        rows = c * MOE_CHUNK + lax.broadcasted_iota(jnp.int32, (MOE_CHUNK, MOE_TT), 0)
        member = rows == lp[0:1, :]
        weight = jnp.where(member, gates[0:1, :], 0.0)
        for k in range(1, TOP_K):
            hit = rows == lp[k:k + 1, :]
            member = member | hit
            weight = jnp.where(hit, gates[k:k + 1, :], weight)
        picked = _dot(jnp.where(member, 1.0, 0.0).astype(BF16), xb)
        row_gate = jnp.sum(weight, axis=1, keepdims=True)
        sl = slice(c * MOE_CHUNK, (c + 1) * MOE_CHUNK)
        buf_ref[sl, :D_MODEL] = picked.astype(BF16)
        g_hi = row_gate.astype(BF16).astype(F32)
        buf_ref[sl, D_MODEL:] = jnp.where(glane == 0, g_hi, jnp.where(glane == 1, row_gate - g_hi, 0.0)).astype(BF16)

    _run_copies(pstart_ref, tab_ref, make_copy, "start")
    _run_copies(pstart_ref, tab_ref, make_copy, "wait")


def _dispatch(pstart, tail_start, tail_len, n_used, tab, lp, gates, x1b, rows):
    t = x1b.shape[0]
    tt = MOE_TT
    grid_spec = pltpu.PrefetchScalarGridSpec(
        num_scalar_prefetch=4,
        grid=(t // tt,),
        in_specs=[pl.BlockSpec((None, 8, 128), lambda i, *_: (i, 0, 0), memory_space=pltpu.SMEM),
                  pl.BlockSpec((8, tt), lambda i, *_: (0, i)),
                  pl.BlockSpec((8, tt), lambda i, *_: (0, i)),
                  pl.BlockSpec((tt, D_MODEL), lambda i, *_: (i, 0))],
        out_specs=pl.BlockSpec(memory_space=pl.ANY),
        scratch_shapes=[pltpu.VMEM((MOE_LR, ROW_W), BF16), pltpu.SemaphoreType.DMA],
    )
    return pl.pallas_call(
        _dispatch_kernel,
        grid_spec=grid_spec,
        out_shape=jax.ShapeDtypeStruct((rows, ROW_W), BF16),
        compiler_params=_params("arbitrary"),
        name="moe_dispatch",
    )(pstart, tail_start, tail_len, n_used, tab, lp, gates, x1b)


def _expert_kernel(be_ref, nu_ref, x_ref, wg_ref, wu_ref, wd_ref, o_ref, wgb_ref, wub_ref, wdb_ref):
    r = pl.program_id(0)

    @pl.when((r == 0) | (be_ref[r] != be_ref[jnp.maximum(r - 1, 0)]))
    def _():
        wgb_ref[...] = wg_ref[...].astype(BF16)
        wub_ref[...] = wu_ref[...].astype(BF16)
        wdb_ref[...] = wd_ref[...].astype(BF16)

    @pl.when(r < nu_ref[0])
    def _():
        xb = x_ref[:, :D_MODEL]
        gate_terms = x_ref[:, D_MODEL:D_MODEL + 2].astype(F32)
        row_gate = gate_terms[:, 0:1] + gate_terms[:, 1:2]
        h = _silu(_dot(xb, wgb_ref[...])) * _dot(xb, wub_ref[...])
        o_ref[...] = (_dot(h.astype(BF16), wdb_ref[...]) * row_gate).astype(o_ref.dtype)

    @pl.when(r >= nu_ref[0])
    def _():
        o_ref[...] = jnp.zeros_like(o_ref)


def _experts(blk_exp, n_used, xs, wg, wu, wd, layer):
    rows = xs.shape[0]
    nblk = rows // MOE_BLK
    grid_spec = pltpu.PrefetchScalarGridSpec(
        num_scalar_prefetch=2,
        grid=(nblk,),
        in_specs=[pl.BlockSpec((MOE_BLK, ROW_W), lambda r, be, nu: (r, 0)),
                  pl.BlockSpec((None, None, D_MODEL, D_EXPERT), lambda r, be, nu: (layer, be[r], 0, 0)),
                  pl.BlockSpec((None, None, D_MODEL, D_EXPERT), lambda r, be, nu: (layer, be[r], 0, 0)),
                  pl.BlockSpec((None, None, D_EXPERT, D_MODEL), lambda r, be, nu: (layer, be[r], 0, 0))],
        out_specs=pl.BlockSpec((MOE_BLK, D_MODEL), lambda r, be, nu: (r, 0)),
        scratch_shapes=[pltpu.VMEM((D_MODEL, D_EXPERT), BF16), pltpu.VMEM((D_MODEL, D_EXPERT), BF16),
                        pltpu.VMEM((D_EXPERT, D_MODEL), BF16)],
    )
    return pl.pallas_call(
        _expert_kernel,
        grid_spec=grid_spec,
        out_shape=jax.ShapeDtypeStruct((rows, D_MODEL), BF16),
        compiler_params=_params("arbitrary"),
        name="moe_experts",
    )(blk_exp, n_used, xs, wg, wu, wd)


def _combine_kernel(pstart_ref, tab_ref, lpt_ref, x_ref, y_hbm, swg_ref, swu_ref, swd_ref, g_ref, b_ref, o_ref,
                    buf_ref, sem):
    i = pl.program_id(0)

    @pl.when(i == 0)
    def _():
        buf_ref[...] = jnp.zeros_like(buf_ref)

    def make_copy(local, glob, size):
        return pltpu.make_async_copy(y_hbm.at[pl.ds(glob, size)], buf_ref.at[pl.ds(local, size)], sem)

    _run_copies(pstart_ref, tab_ref, make_copy, "start")

    x = x_ref[...]
    xb = x.astype(BF16)
    h = _silu(_dot(xb, swg_ref[...])) * _dot(xb, swu_ref[...])
    acc = ALPHA * x + _dot(h.astype(BF16), swd_ref[...])

    _run_copies(pstart_ref, tab_ref, make_copy, "wait")

    lpt = lpt_ref[...].astype(jnp.int32)
    for c in range(MOE_LR // MOE_CHUNK):
        cols = c * MOE_CHUNK + lax.broadcasted_iota(jnp.int32, (MOE_TT, MOE_CHUNK), 1)
        member = cols == lpt[:, 0:1]
        for k in range(1, TOP_K):
            member = member | (cols == lpt[:, k:k + 1])
        sel = jnp.where(member, 1.0, 0.0).astype(BF16)
        acc = acc + _dot(sel, buf_ref[c * MOE_CHUNK:(c + 1) * MOE_CHUNK, :])
    o_ref[...] = _layer_norm(acc, g_ref[...], b_ref[...])


def _combine(pstart, tab, lpt, x1, y_rows, swg, swu, swd, g, b):
    t = x1.shape[0]
    tt = MOE_TT

    def whole(arr):
        return pl.BlockSpec(arr.shape, lambda i, ps: (0, 0))

    g2, b2 = g.reshape(1, -1), b.reshape(1, -1)
    grid_spec = pltpu.PrefetchScalarGridSpec(
        num_scalar_prefetch=1,
        grid=(t // tt,),
        in_specs=[pl.BlockSpec((None, 8, 128), lambda i, ps: (i, 0, 0), memory_space=pltpu.SMEM),
                  pl.BlockSpec((tt, 128), lambda i, ps: (i, 0)),
                  pl.BlockSpec((tt, D_MODEL), lambda i, ps: (i, 0)),
                  pl.BlockSpec(memory_space=pl.ANY),
                  whole(swg), whole(swu), whole(swd), whole(g2), whole(b2)],
        out_specs=pl.BlockSpec((tt, D_MODEL), lambda i, ps: (i, 0)),
        scratch_shapes=[pltpu.VMEM((MOE_LR, D_MODEL), BF16), pltpu.SemaphoreType.DMA],
    )
    return pl.pallas_call(
        _combine_kernel,
        grid_spec=grid_spec,
        out_shape=jax.ShapeDtypeStruct((t, D_MODEL), F32),
        compiler_params=_params("arbitrary"),
        name="moe_combine_shared_ln",
    )(pstart, tab, lpt, x1, y_rows, swg, swu, swd, g2, b2)


def _moe(x1, x1b, layer, router_w, router_b, wg, wu, wd, swg, swu, swd, g, b):
    t = x1.shape[0]
    lp, lpt, gates, tab, cnt = _router(x1, router_w, router_b)
    rows_e = cnt[:, 0].astype(jnp.int32)
    padded = (rows_e + MOE_BLK - 1) // MOE_BLK * MOE_BLK
    pend = jnp.cumsum(padded)
    pstart = pend - padded
    max_rows = t * TOP_K + (t // MOE_TT) * N_EXPERTS * (RUN_ALIGN - 1) + N_EXPERTS * (MOE_BLK - 1)
    n_blocks = -(-max_rows // MOE_BLK)
    blk_first = jnp.arange(n_blocks, dtype=jnp.int32) * MOE_BLK
    blk_exp = jnp.minimum(jnp.sum((pend[None, :] <= blk_first[:, None]).astype(jnp.int32), axis=1), N_EXPERTS - 1)
    n_used = (pend[-1:] // MOE_BLK).astype(jnp.int32)
    xs = _dispatch(pstart, pstart + rows_e, padded - rows_e, n_used, tab, lp, gates, x1b, n_blocks * MOE_BLK)
    y_rows = _experts(blk_exp, n_used, xs, wg, wu, wd, layer)
    return _combine(pstart, tab, lpt, x1, y_rows, swg, swu, swd, g, b)


def _layer(x2d, bsz, seq, layer, w_in, moe_wg, moe_wu, moe_wd,
           conv_dw, conv_db, conv_ln_g, conv_ln_b, swa_sink, ret_gn_g, ret_gn_b,
           nsa_pe_k, nsa_w1_k, nsa_w2_k, nsa_pe_v, nsa_w1_v, nsa_w2_v, p_conv, p_swa, p_ret, p_nsa, w_out,
           ln1_g, ln1_b, router_w, router_b, sh_wg, sh_wu, sh_wd, ln2_g, ln2_b):
    bf = lambda a: a.astype(BF16)
    w_a = _mix_weights(w_in, layer)
    w_m = _merge_gate_weights(w_in, layer)
    ua, kcvc = _inproj(x2d, w_a)
    conv_a = _conv(ua, conv_dw, conv_db, conv_ln_g, conv_ln_b, bsz, seq)
    swa_o = _swa(ua, swa_sink, bsz, seq)
    ret_o = _retention(ua, ret_gn_g, ret_gn_b, bsz, seq)
    kcmp, vcmp = _compress(kcvc, nsa_pe_k, bf(nsa_w1_k), bf(nsa_w2_k), nsa_pe_v, bf(nsa_w1_v), bf(nsa_w2_v), bsz, seq)
    nsa_o = _nsa(ua, kcmp, vcmp, bsz, seq)
    x1, x1b = _merge(x2d, conv_a, swa_o, ret_o, nsa_o, w_m, bf(p_conv), bf(p_swa), bf(p_ret), bf(p_nsa),
                     bf(w_out), ln1_g, ln1_b)
    return _moe(x1, x1b, layer, router_w, router_b, moe_wg, moe_wu, moe_wd,
                bf(sh_wg), bf(sh_wu), bf(sh_wd), ln2_g, ln2_b)


def kernel(x, w_in, conv_dw, conv_db, conv_ln_g, conv_ln_b, swa_sink, ret_gn_g, ret_gn_b, nsa_pe_k, nsa_w1_k, nsa_w2_k, nsa_pe_v, nsa_w1_v, nsa_w2_v, p_conv, p_swa, p_ret, p_nsa, w_out, ln1_g, ln1_b, router_w, router_b, moe_wg, moe_wu, moe_wd, sh_wg, sh_wu, sh_wd, ln2_g, ln2_b):
    bsz, seq, d = x.shape
    params = (conv_dw, conv_db, conv_ln_g, conv_ln_b, swa_sink, ret_gn_g, ret_gn_b,
              nsa_pe_k, nsa_w1_k, nsa_w2_k, nsa_pe_v, nsa_w1_v, nsa_w2_v, p_conv, p_swa, p_ret, p_nsa, w_out,
              ln1_g, ln1_b, router_w, router_b, sh_wg, sh_wu, sh_wd, ln2_g, ln2_b)
    x2d = x.reshape(bsz * seq, d)
    for l in range(w_in.shape[0]):
        x2d = _layer(x2d, bsz, seq, l, w_in, moe_wg, moe_wu, moe_wd, *[p[l] for p in params])
    return x2d.reshape(bsz, seq, d)
```

```python
import functools
import math

import jax
import jax.numpy as jnp
import numpy as np
from jax import lax
from jax.experimental import pallas as pl
from jax.experimental.pallas import tpu as pltpu

F32 = jnp.float32
BF16 = jnp.bfloat16

D_MODEL = 1024
DEPTH = 2
HEAD_DIM = 64
BLK = 128
CONV_CH = 256
CONV_K = 31
SWA_HEADS = 4
SWA_KV_HEADS = 2
SWA_WINDOW = 128
RET_HEADS = 4
RET_QK = 64
RET_V = 128
NSA_HEADS = 4
NSA_CMP_LEN = 32
NSA_CMP_STRIDE = 16
NSA_CMP_HIDDEN = 256
NSA_SLC_LEN = 64
NSA_TOPN = 8
NSA_WINDOW = 512
NSA_FORCE = 1e4
N_EXPERTS = 64
TOP_K = 6
N_GROUPS = 8
GROUP_SIZE = N_EXPERTS // N_GROUPS
TOPK_GROUPS = 4
D_EXPERT = 256
ROUTE_SCALE = 2.5
MOE_BLK = 512
N_BRANCH = 4
ALPHA = (2 * DEPTH) ** 0.25
LN_EPS = 1e-5
NEG_INF = -1e30
SCALE = HEAD_DIM ** -0.5

C_CONV = 0
C_SWA = 512
C_RETQK = 1024
C_RETV = 1536
C_RETG = 2048
C_NSAQ = 2560
C_NSAC = 2816
C_NSAS = 2944
C_NSAW = 3072
C_NSAG = 3200
N_MIX_COLS = 3212
UA = 3328

ALIBI = [2.0 ** (-8.0 * (i + 1) / 8) for i in range(8)]
SLOPES_SWA = ALIBI[:4]
SLOPES_NSA = ALIBI[4:]
RET_LOG_GAMMA = [math.log1p(-(2.0 ** (-5.0 - h))) for h in range(RET_HEADS)]

VMEM_LIMIT = 48 * 1024 * 1024
NT_DIMS = (((1,), (1,)), ((), ()))
TN_DIMS = (((0,), (0,)), ((), ()))


def _params(*sem):
    return pltpu.CompilerParams(dimension_semantics=sem, vmem_limit_bytes=VMEM_LIMIT)


def _dot(a, b):
    return jnp.dot(a, b, preferred_element_type=F32)


def _dot_nt(a, b):
    return lax.dot_general(a, b, NT_DIMS, preferred_element_type=F32)


def _layer_norm(v, g, b):
    mu = jnp.mean(v, axis=-1, keepdims=True)
    c = v - mu
    var = jnp.mean(c * c, axis=-1, keepdims=True)
    return c * lax.rsqrt(var + LN_EPS) * g + b


def _silu(v):
    return v * jax.nn.sigmoid(v)


CAST_TM = 256
LANES = 128


def _cast_kernel(x_ref, o_ref):
    o_ref[...] = x_ref[...].astype(o_ref.dtype)


def _mix_weights(w_in, layer):
    return pl.pallas_call(
        _cast_kernel,
        grid=(D_MODEL // CAST_TM,),
        in_specs=[pl.BlockSpec((None, CAST_TM, UA), lambda i: (layer, i, 0))],
        out_specs=pl.BlockSpec((CAST_TM, UA), lambda i: (i, 0)),
        out_shape=jax.ShapeDtypeStruct((D_MODEL, UA), BF16),
        compiler_params=_params("parallel"),
        name="cast_mix_weights",
    )(w_in)


def _shifted_cast_kernel(a_ref, b_ref, o_ref):
    shift = N_MIX_COLS % LANES
    both = jnp.concatenate([a_ref[...], b_ref[...]], axis=1)
    o_ref[...] = both[:, shift:shift + LANES].astype(o_ref.dtype)


def _merge_gate_weights(w_in, layer):
    first = N_MIX_COLS // LANES
    width = N_BRANCH * D_MODEL
    return pl.pallas_call(
        _shifted_cast_kernel,
        grid=(width // LANES,),
        in_specs=[pl.BlockSpec((None, D_MODEL, LANES), lambda j: (layer, 0, first + j)),
                  pl.BlockSpec((None, D_MODEL, LANES), lambda j: (layer, 0, first + j + 1))],
        out_specs=pl.BlockSpec((D_MODEL, LANES), lambda j: (0, j)),
        out_shape=jax.ShapeDtypeStruct((D_MODEL, width), BF16),
        compiler_params=_params("parallel"),
        name="cast_merge_gate_weights",
    )(w_in, w_in)


INPROJ_TM = 512


def _inproj_kernel(x_ref, w_ref, o_ref, oc_ref):
    xb = x_ref[...].astype(BF16)
    for c in range(0, UA, 256):
        o_ref[:, c:c + 256] = _dot(xb, w_ref[:, c:c + 256]).astype(o_ref.dtype)
    oc_ref[...] = _dot(xb, w_ref[:, C_NSAC:C_NSAC + 128])


def _inproj(x2d, w_a):
    t = x2d.shape[0]
    return pl.pallas_call(
        _inproj_kernel,
        grid=(t // INPROJ_TM,),
        in_specs=[pl.BlockSpec((INPROJ_TM, D_MODEL), lambda i: (i, 0)),
                  pl.BlockSpec((D_MODEL, UA), lambda i: (0, 0))],
        out_specs=[pl.BlockSpec((INPROJ_TM, UA), lambda i: (i, 0)),
                   pl.BlockSpec((INPROJ_TM, 128), lambda i: (i, 0))],
        out_shape=[jax.ShapeDtypeStruct((t, UA), BF16), jax.ShapeDtypeStruct((t, 128), F32)],
        compiler_params=_params("parallel"),
        name="inproj",
    )(x2d, w_a)


CONV_TS = 256
CONV_HALO = 32


def _conv_kernel(cur_ref, halo_ref, dw_ref, db_ref, g_ref, b_ref, o_ref, hc_ref):
    i = pl.program_id(1)

    def glu(u):
        u = u.astype(F32)
        return u[:, :CONV_CH] * jax.nn.sigmoid(u[:, CONV_CH:])

    hc_ref[0:CONV_HALO, :] = jnp.where(i > 0, glu(halo_ref[...]), 0.0)
    hc_ref[CONV_HALO:CONV_HALO + CONV_TS, :] = glu(cur_ref[...])
    acc = jnp.broadcast_to(db_ref[...], (CONV_TS, CONV_CH))
    off = CONV_HALO - (CONV_K - 1)
    for j in range(CONV_K):
        acc = acc + hc_ref[off + j:off + j + CONV_TS, :] * dw_ref[j:j + 1, :]
    o_ref[...] = _silu(_layer_norm(acc, g_ref[...], b_ref[...])).astype(o_ref.dtype)


def _conv(ua, dw, db, g, b, bsz, seq):
    t = bsz * seq
    ns = seq // CONV_TS
    per = CONV_TS // CONV_HALO
    dwp = jnp.concatenate([dw, jnp.zeros((32 - CONV_K, CONV_CH), F32)], axis=0)
    vec = pl.BlockSpec((1, CONV_CH), lambda bb, i: (0, 0))
    return pl.pallas_call(
        _conv_kernel,
        grid=(bsz, ns),
        in_specs=[pl.BlockSpec((CONV_TS, 512), lambda bb, i: (bb * ns + i, C_CONV // 512)),
                  pl.BlockSpec((CONV_HALO, 512),
                               lambda bb, i: (jnp.maximum((bb * ns + i) * per - 1, 0), C_CONV // 512)),
                  pl.BlockSpec((32, CONV_CH), lambda bb, i: (0, 0)), vec, vec, vec],
        out_specs=pl.BlockSpec((CONV_TS, CONV_CH), lambda bb, i: (bb * ns + i, 0)),
        out_shape=jax.ShapeDtypeStruct((t, CONV_CH), BF16),
        scratch_shapes=[pltpu.VMEM((CONV_HALO + CONV_TS, CONV_CH), F32)],
        compiler_params=_params("parallel", "parallel"),
        name="conformer_conv",
    )(ua, ua, dwp, db.reshape(1, -1), g.reshape(1, -1), b.reshape(1, -1))


def _swa_kernel(sink_ref, q_ref, kvc_ref, kvp_ref, o_ref):
    i = pl.program_id(1)
    q = q_ref[...]
    kvc = kvc_ref[...]
    kvp = kvp_ref[...]
    a = lax.broadcasted_iota(jnp.int32, (BLK, 2 * BLK), 0)
    c = lax.broadcasted_iota(jnp.int32, (BLK, 2 * BLK), 1)
    dist = BLK + a - c
    mask = (dist >= 0) & (dist < SWA_WINDOW) & ((c >= BLK) | (i > 0))
    distf = dist.astype(F32)
    rep = SWA_HEADS // SWA_KV_HEADS
    for h in range(SWA_HEADS):
        g = h // rep
        k = jnp.concatenate([kvp[:, g * HEAD_DIM:(g + 1) * HEAD_DIM],
                             kvc[:, g * HEAD_DIM:(g + 1) * HEAD_DIM]], axis=0)
        v = jnp.concatenate([kvp[:, BLK + g * HEAD_DIM:BLK + (g + 1) * HEAD_DIM],
                             kvc[:, BLK + g * HEAD_DIM:BLK + (g + 1) * HEAD_DIM]], axis=0)
        s = _dot_nt(q[:, h * HEAD_DIM:(h + 1) * HEAD_DIM], k) * SCALE - SLOPES_SWA[h] * distf
        s = jnp.where(mask, s, NEG_INF)
        sink = sink_ref[h]
        m = jnp.maximum(jnp.max(s, axis=-1, keepdims=True), sink)
        e = jnp.where(mask, jnp.exp(s - m), 0.0)
        den = jnp.sum(e, axis=-1, keepdims=True) + jnp.exp(sink - m)
        p = e / jnp.maximum(den, 1e-30)
        o_ref[:, h * HEAD_DIM:(h + 1) * HEAD_DIM] = _dot(p.astype(BF16), v).astype(o_ref.dtype)


def _swa(ua, sink, bsz, seq):
    t = bsz * seq
    nb = seq // BLK
    return pl.pallas_call(
        _swa_kernel,
        grid=(bsz, nb),
        in_specs=[pl.BlockSpec(memory_space=pltpu.SMEM),
                  pl.BlockSpec((BLK, 256), lambda bb, i: (bb * nb + i, C_SWA // 256)),
                  pl.BlockSpec((BLK, 256), lambda bb, i: (bb * nb + i, C_SWA // 256 + 1)),
                  pl.BlockSpec((BLK, 256), lambda bb, i: (jnp.maximum(bb * nb + i - 1, 0), C_SWA // 256 + 1))],
        out_specs=pl.BlockSpec((BLK, 256), lambda bb, i: (bb * nb + i, 0)),
        out_shape=jax.ShapeDtypeStruct((t, SWA_HEADS * HEAD_DIM), BF16),
        compiler_params=_params("parallel", "parallel"),
        name="swa",
    )(sink, ua, ua, ua)


def _ret_kernel(qk_ref, v_ref, g_ref, gng_ref, gnb_ref, o_ref, state_ref):
    n = pl.program_id(1)

    @pl.when(n == 0)
    def _():
        state_ref[...] = jnp.zeros_like(state_ref)

    qk = qk_ref[...]
    vv = v_ref[...]
    gate = g_ref[...].astype(F32)
    a = lax.broadcasted_iota(jnp.int32, (BLK, BLK), 0)
    c = lax.broadcasted_iota(jnp.int32, (BLK, BLK), 1)
    diff = (a - c).astype(F32)
    idx = lax.broadcasted_iota(jnp.int32, (BLK, 1), 0).astype(F32)
    hq = RET_HEADS * RET_QK
    for h in range(RET_HEADS):
        lg = RET_LOG_GAMMA[h]
        q = qk[:, h * RET_QK:(h + 1) * RET_QK]
        k = qk[:, hq + h * RET_QK:hq + (h + 1) * RET_QK].astype(F32) * (RET_QK ** -0.5)
        v = vv[:, h * RET_V:(h + 1) * RET_V]
        dmat = jnp.where(diff >= 0, jnp.exp(lg * jnp.maximum(diff, 0.0)), 0.0)
        inner = _dot_nt(q, k.astype(BF16)) * dmat
        o_inner = _dot(inner.astype(BF16), v)
        zeta = jnp.exp(lg * (BLK - 1 - idx))
        xi = jnp.exp(lg * (idx + 1.0))
        kv = lax.dot_general((k * zeta).astype(BF16), v, TN_DIMS, preferred_element_type=F32)
        prev = state_ref[h]
        o = o_inner + _dot(q, prev.astype(BF16)) * xi
        state_ref[h] = math.exp(lg * BLK) * prev + kv
        mu = jnp.mean(o, axis=-1, keepdims=True)
        cen = o - mu
        var = jnp.mean(cen * cen, axis=-1, keepdims=True)
        sl = slice(h * RET_V, (h + 1) * RET_V)
        on = cen * lax.rsqrt(var + LN_EPS) * gng_ref[:, sl] + gnb_ref[:, sl]
        o_ref[:, sl] = (_silu(gate[:, sl]) * on).astype(o_ref.dtype)


def _retention(ua, gn_g, gn_b, bsz, seq):
    t = bsz * seq
    nc = seq // BLK
    vec = pl.BlockSpec((1, RET_HEADS * RET_V), lambda bb, i: (0, 0))
    return pl.pallas_call(
        _ret_kernel,
        grid=(bsz, nc),
        in_specs=[pl.BlockSpec((BLK, 512), lambda bb, i: (bb * nc + i, C_RETQK // 512)),
                  pl.BlockSpec((BLK, 512), lambda bb, i: (bb * nc + i, C_RETV // 512)),
                  pl.BlockSpec((BLK, 512), lambda bb, i: (bb * nc + i, C_RETG // 512)),
                  vec, vec],
        out_specs=pl.BlockSpec((BLK, 512), lambda bb, i: (bb * nc + i, 0)),
        out_shape=jax.ShapeDtypeStruct((t, RET_HEADS * RET_V), BF16),
        scratch_shapes=[pltpu.VMEM((RET_HEADS, RET_QK, RET_V), F32)],
        compiler_params=_params("parallel", "arbitrary"),
        name="retention",
    )(ua, ua, ua, gn_g.reshape(1, -1), gn_b.reshape(1, -1))


def _compress_kernel(t_ref, pek_ref, w1k_ref, w2k_ref, pev_ref, w1v_ref, w2v_ref, ok_ref, ov_ref):
    st = NSA_CMP_STRIDE
    ng = t_ref.shape[0] // st
    branches = ((pek_ref, w1k_ref, w2k_ref, ok_ref), (pev_ref, w1v_ref, w2v_ref, ov_ref))
    first = [jnp.zeros((ng, NSA_CMP_HIDDEN), F32) for _ in branches]
    second = [jnp.zeros((ng, NSA_CMP_HIDDEN), F32) for _ in branches]
    for r in range(st):
        rows = t_ref[pl.ds(r, ng, stride=st), :]
        for n, (pe_ref, w1_ref, _, _) in enumerate(branches):
            v = rows[:, n * HEAD_DIM:(n + 1) * HEAD_DIM]
            lo = (v + pe_ref[r:r + 1, :]).astype(BF16)
            hi = (v + pe_ref[st + r:st + r + 1, :]).astype(BF16)
            first[n] = first[n] + _dot(lo, w1_ref[r * HEAD_DIM:(r + 1) * HEAD_DIM, :])
            second[n] = second[n] + _dot(hi, w1_ref[(st + r) * HEAD_DIM:(st + r + 1) * HEAD_DIM, :])
    for n, (_, _, w2_ref, o_ref) in enumerate(branches):
        hid = first[n] + pltpu.roll(second[n], ng - 1, axis=0)
        act = 0.5 * hid * (1.0 + lax.erf(hid * (2.0 ** -0.5)))
        o_ref[0] = _dot(act.astype(BF16), w2_ref[...]).astype(o_ref.dtype)


def _compress(kcvc, pe_k, w1_k, w2_k, pe_v, w1_v, w2_v, bsz, seq):
    ng = seq // NSA_CMP_STRIDE
    pes = pl.BlockSpec((NSA_CMP_LEN, HEAD_DIM), lambda bb: (0, 0))
    w1s = pl.BlockSpec((NSA_CMP_LEN * HEAD_DIM, NSA_CMP_HIDDEN), lambda bb: (0, 0))
    w2s = pl.BlockSpec((NSA_CMP_HIDDEN, HEAD_DIM), lambda bb: (0, 0))
    osp = pl.BlockSpec((1, ng, HEAD_DIM), lambda bb: (bb, 0, 0))
    return pl.pallas_call(
        _compress_kernel,
        grid=(bsz,),
        in_specs=[pl.BlockSpec((seq, 128), lambda bb: (bb, 0)), pes, w1s, w2s, pes, w1s, w2s],
        out_specs=[osp, osp],
        out_shape=[jax.ShapeDtypeStruct((bsz, ng, HEAD_DIM), BF16)] * 2,
        compiler_params=_params("parallel"),
        name="nsa_compress",
    )(kcvc, pe_k, w1_k, w2_k, pe_v, w1_v, w2_v)


NSA_KC = 512
NSA_WSPAN = NSA_WINDOW + BLK


NSA_PAD = 128
Q_LANES = NSA_HEADS * BLK


def _nsa_query_select():
    sel = np.zeros((NSA_HEADS * HEAD_DIM, NSA_HEADS * NSA_PAD), np.float32)
    for h in range(NSA_HEADS):
        for dd in range(HEAD_DIM):
            sel[h * HEAD_DIM + dd, h * NSA_PAD + dd] = SCALE
    return jnp.asarray(sel, BF16)


def _softmax_down(s):
    m = jnp.maximum(jnp.max(s, axis=0, keepdims=True), -1e29)
    e = jnp.exp(s - m)
    return e / jnp.maximum(jnp.sum(e, axis=0, keepdims=True), 1e-30)


def _nsa_kernel(q_ref, qsel_ref, kcmp_ref, vcmp_ref, ksvs_ref, kwvw_ref, gate_ref, o_ref,
                ksa_ref, kwa_ref, kca_ref, vst_ref, vwt_ref, vct_ref, ext_ref, bsel_ref):
    i = pl.program_id(1)
    seq = ksvs_ref.shape[0]
    ncmp = kcmp_ref.shape[1]
    nslc = seq // NSA_SLC_LEN
    lane = lax.broadcasted_iota(jnp.int32, (1, NSA_PAD), 1)

    @pl.when(i == 0)
    def _():
        def aug(k, pos):
            kf = jnp.concatenate([k.astype(F32), jnp.zeros((k.shape[0], NSA_PAD - HEAD_DIM), F32)], axis=1)
            hi = (pos // 128).astype(F32)
            lo = (pos % 128).astype(F32)
            return (kf + jnp.where(lane == HEAD_DIM, hi, jnp.where(lane == HEAD_DIM + 1, lo, 0.0))).astype(BF16)

        def value_t(v):
            vf = jnp.concatenate([v.astype(F32), jnp.zeros((v.shape[0], NSA_PAD - HEAD_DIM), F32)], axis=1)
            return vf.T.astype(BF16)

        kpos = lax.broadcasted_iota(jnp.int32, (seq, 1), 0)
        ksvs = ksvs_ref[...]
        kwvw = kwvw_ref[...]
        ksa_ref[...] = aug(ksvs[:, :HEAD_DIM], kpos)
        kwa_ref[...] = aug(kwvw[:, :HEAD_DIM], kpos)
        cend = lax.broadcasted_iota(jnp.int32, (ncmp, 1), 0) * NSA_CMP_STRIDE + (NSA_CMP_LEN - 1)
        kca_ref[...] = aug(kcmp_ref[0], cend)
        vst_ref[...] = value_t(ksvs[:, HEAD_DIM:])
        vwt_ref[...] = value_t(kwvw[:, HEAD_DIM:])
        vct_ref[...] = value_t(vcmp_ref[0])
        ext_ref[...] = jnp.where(kpos // NSA_SLC_LEN == lax.broadcasted_iota(jnp.int32, (seq, NSA_PAD), 1),
                                 1.0, 0.0).astype(BF16)

    t0 = i * BLK
    tq = t0 + lax.broadcasted_iota(jnp.int32, (1, BLK), 1)
    qw = _dot(q_ref[...], qsel_ref[...])
    qaug = []
    for h in range(NSA_HEADS):
        coef = jnp.where(lane == HEAD_DIM, 128.0 * SLOPES_NSA[h], jnp.where(lane == HEAD_DIM + 1, SLOPES_NSA[h], 0.0))
        qaug.append((qw[:, h * NSA_PAD:(h + 1) * NSA_PAD] + coef).astype(BF16))
    qaug = jnp.concatenate(qaug, axis=0)

    def tile4(b):
        return jnp.concatenate([b] * NSA_HEADS, axis=1)

    cend = lax.broadcasted_iota(jnp.int32, (ncmp, 1), 0) * NSA_CMP_STRIDE + (NSA_CMP_LEN - 1)
    p_cmp = _softmax_down(_dot_nt(kca_ref[...], qaug) + tile4(jnp.where(cend <= tq, 0.0, NEG_INF)))
    o_cmp = _dot(vct_ref[...], p_cmp.astype(BF16))
    psum = p_cmp[:, 0:BLK]
    for h in range(1, NSA_HEADS):
        psum = psum + p_cmp[:, h * BLK:(h + 1) * BLK]

    sst = lax.broadcasted_iota(jnp.int32, (nslc, ncmp), 0) * NSA_SLC_LEN
    cst = lax.broadcasted_iota(jnp.int32, (nslc, ncmp), 1) * NSA_CMP_STRIDE
    overlap = jnp.where((cst < sst + NSA_SLC_LEN) & (cst + NSA_CMP_LEN > sst), 1.0, 0.0).astype(BF16)
    p_hi = psum.astype(BF16)
    p_lo = (psum - p_hi.astype(F32)).astype(BF16)
    imp = _dot(overlap, p_hi) + _dot(overlap, p_lo)
    jj = lax.broadcasted_iota(jnp.int32, (nslc, BLK), 0)
    cur = tq // NSA_SLC_LEN
    forced = (jj == 0) | (jj == cur) | (jj == cur - 1)
    imp = jnp.where(forced, NSA_FORCE, jnp.where(jj > cur, -1.0, imp))

    sel = jnp.zeros((nslc, BLK), F32)
    for _ in range(min(NSA_TOPN, nslc)):
        mx = jnp.max(imp, axis=0, keepdims=True)
        first = jnp.min(jnp.where(imp == mx, jj, nslc), axis=0, keepdims=True)
        hit = jj == first
        sel = jnp.where(hit, 1.0, sel)
        imp = jnp.where(hit, -3e38, imp)
    sel = jnp.concatenate([sel, jnp.zeros((NSA_PAD - nslc, BLK), F32)], axis=0).astype(BF16)
    bsel_ref[...] = (_dot(ext_ref[...], sel) - 1.0) * 1e30

    def slc_step(c, carry):
        m, l, acc = carry
        start = pl.multiple_of(c * NSA_KC, NSA_KC)
        kpos = start + lax.broadcasted_iota(jnp.int32, (NSA_KC, 1), 0)
        bias = bsel_ref[pl.ds(start, NSA_KC), :] + jnp.where(kpos <= tq, 0.0, NEG_INF)
        s = _dot_nt(ksa_ref[pl.ds(start, NSA_KC), :], qaug) + tile4(bias)
        m_new = jnp.maximum(m, jnp.max(s, axis=0, keepdims=True))
        alpha = jnp.exp(m - m_new)
        p = jnp.exp(s - m_new)
        l_new = alpha * l + jnp.sum(p, axis=0, keepdims=True)
        acc_new = alpha * acc + _dot(vst_ref[:, pl.ds(start, NSA_KC)], p.astype(BF16))
        return m_new, l_new, acc_new

    init = (jnp.full((1, Q_LANES), -1e29, F32), jnp.zeros((1, Q_LANES), F32), jnp.zeros((NSA_PAD, Q_LANES), F32))
    _, l_slc, acc_slc = lax.fori_loop(0, (t0 + BLK - 1) // NSA_KC + 1, slc_step, init)
    o_slc = acc_slc / jnp.maximum(l_slc, 1e-30)

    wspan = min(NSA_WSPAN, seq)
    wstart = pl.multiple_of(jnp.maximum(t0 + BLK - wspan, 0), BLK)
    wdist = tq - (wstart + lax.broadcasted_iota(jnp.int32, (wspan, 1), 0))
    wbias = jnp.where((wdist >= 0) & (wdist < NSA_WINDOW), 0.0, NEG_INF)
    p_win = _softmax_down(_dot_nt(kwa_ref[pl.ds(wstart, wspan), :], qaug) + tile4(wbias))
    o_win = _dot(vwt_ref[:, pl.ds(wstart, wspan)], p_win.astype(BF16))

    gts = jax.nn.sigmoid(gate_ref[...].astype(F32)).T
    outs = []
    for h in range(NSA_HEADS):
        cols = slice(h * BLK, (h + 1) * BLK)
        outs.append(gts[h:h + 1, :] * o_cmp[:HEAD_DIM, cols]
                    + gts[NSA_HEADS + h:NSA_HEADS + h + 1, :] * o_slc[:HEAD_DIM, cols]
                    + gts[2 * NSA_HEADS + h:2 * NSA_HEADS + h + 1, :] * o_win[:HEAD_DIM, cols])
    o_ref[...] = jnp.concatenate(outs, axis=0).T.astype(o_ref.dtype)


def _nsa(ua, kcmp, vcmp, bsz, seq):
    t = bsz * seq
    nq = seq // BLK
    ncmp = kcmp.shape[1]
    qsel = _nsa_query_select()
    staged_k = pltpu.VMEM((seq, NSA_PAD), BF16)
    staged_v = pltpu.VMEM((NSA_PAD, seq), BF16)
    return pl.pallas_call(
        _nsa_kernel,
        grid=(bsz, nq),
        in_specs=[pl.BlockSpec((BLK, 256), lambda bb, i: (bb * nq + i, C_NSAQ // 256)),
                  pl.BlockSpec(qsel.shape, lambda bb, i: (0, 0)),
                  pl.BlockSpec((1, ncmp, HEAD_DIM), lambda bb, i: (bb, 0, 0)),
                  pl.BlockSpec((1, ncmp, HEAD_DIM), lambda bb, i: (bb, 0, 0)),
                  pl.BlockSpec((seq, 128), lambda bb, i: (bb, C_NSAS // 128)),
                  pl.BlockSpec((seq, 128), lambda bb, i: (bb, C_NSAW // 128)),
                  pl.BlockSpec((BLK, 128), lambda bb, i: (bb * nq + i, C_NSAG // 128))],
        out_specs=pl.BlockSpec((BLK, 256), lambda bb, i: (bb * nq + i, 0)),
        out_shape=jax.ShapeDtypeStruct((t, NSA_HEADS * HEAD_DIM), BF16),
        scratch_shapes=[staged_k, staged_k, pltpu.VMEM((ncmp, NSA_PAD), BF16),
                        staged_v, staged_v, pltpu.VMEM((NSA_PAD, ncmp), BF16),
                        staged_k, pltpu.VMEM((seq, BLK), F32)],
        compiler_params=_params("parallel", "arbitrary"),
        name="nsa_attention",
    )(ua, qsel, kcmp, vcmp, ua, ua, ua)


MERGE_TM = 512


def _merge_kernel(x_ref, conv_ref, swa_ref, ret_ref, nsa_ref, wm_ref, pc_ref, ps_ref, pr_ref, pn_ref,
                  wo_ref, g_ref, b_ref, o_ref, op_ref):
    x = x_ref[...]
    xb = x.astype(BF16)
    merged = None
    for n, (a_ref, p_ref) in enumerate(((conv_ref, pc_ref), (swa_ref, ps_ref), (ret_ref, pr_ref), (nsa_ref, pn_ref))):
        gate = jax.nn.sigmoid(_dot(xb, wm_ref[:, n * D_MODEL:(n + 1) * D_MODEL]))
        term = gate * _dot(a_ref[...], p_ref[...])
        merged = term if merged is None else merged + term
    mix = _dot(merged.astype(BF16), wo_ref[...])
    y = _layer_norm(ALPHA * x + mix, g_ref[...], b_ref[...])
    o_ref[...] = y
    op_ref[...] = y.astype(BF16)


def _merge(x2d, conv_a, swa_o, ret_o, nsa_o, w_m, p_conv, p_swa, p_ret, p_nsa, w_out, g, b):
    t = x2d.shape[0]
    tm = min(MERGE_TM, t)

    def rows(width):
        return pl.BlockSpec((tm, width), lambda i: (i, 0))

    def whole(arr):
        return pl.BlockSpec(arr.shape, lambda i: (0, 0), pipeline_mode=pl.Buffered(1))

    g2, b2 = g.reshape(1, -1), b.reshape(1, -1)
    return pl.pallas_call(
        _merge_kernel,
        grid=(t // tm,),
        in_specs=[rows(D_MODEL), rows(256), rows(256), rows(512), rows(256),
                  whole(w_m), whole(p_conv), whole(p_swa), whole(p_ret), whole(p_nsa), whole(w_out),
                  whole(g2), whole(b2)],
        out_specs=[rows(D_MODEL), rows(D_MODEL)],
        out_shape=[jax.ShapeDtypeStruct((t, D_MODEL), F32), jax.ShapeDtypeStruct((t, D_MODEL), BF16)],
        compiler_params=_params("parallel"),
        name="merge_outproj_ln",
    )(x2d, conv_a, swa_o, ret_o, nsa_o, w_m, p_conv, p_swa, p_ret, p_nsa, w_out, g2, b2)


MOE_TT = 256
RUN_ALIGN = 16
MOE_LR = 2560
MOE_CHUNK = 256
RUN_PIECES = (256, 128, 64, 32, 16)
ROW_W = D_MODEL + 128
assert MOE_LR >= TOP_K * MOE_TT + N_EXPERTS * (RUN_ALIGN - 1) and MOE_LR % MOE_CHUNK == 0


def _first_argmax(v, iota, size):
    m = jnp.max(v, axis=0, keepdims=True)
    idx = jnp.min(jnp.where(v == m, iota, size), axis=0, keepdims=True)
    return m, idx


def _router_kernel(x_ref, w_ref, b_ref, lp_ref, lpt_ref, gate_ref, tab_ref, cnt_ref, carry_ref):
    i = pl.program_id(0)
    tr = x_ref.shape[0]

    @pl.when(i == 0)
    def _():
        carry_ref[...] = jnp.zeros_like(carry_ref)

    x = x_ref[...]
    w = w_ref[...]
    xh = x.astype(BF16)
    xl = (x - xh.astype(F32)).astype(BF16)
    wh = w.astype(BF16)
    wl = (w - wh.astype(F32)).astype(BF16)
    logits = _dot_nt(wh, xh) + _dot_nt(wh, xl) + _dot_nt(wl, xh)
    scores = jax.nn.sigmoid(logits)
    biased = scores + b_ref[...]

    iota_g = lax.broadcasted_iota(jnp.int32, (GROUP_SIZE, tr), 0)
    grp = []
    for g in range(N_GROUPS):
        vg = biased[g * GROUP_SIZE:(g + 1) * GROUP_SIZE, :]
        m1, i1 = _first_argmax(vg, iota_g, GROUP_SIZE)
        m2 = jnp.max(jnp.where(iota_g == i1, -jnp.inf, vg), axis=0, keepdims=True)
        grp.append(m1 + m2)
    work = jnp.concatenate(grp, axis=0)
    iota_n = lax.broadcasted_iota(jnp.int32, (N_GROUPS, tr), 0)
    gsel = jnp.zeros((N_GROUPS, tr), F32)
    for _ in range(TOPK_GROUPS):
        _, gi = _first_argmax(work, iota_n, N_GROUPS)
        hit = iota_n == gi
        gsel = jnp.where(hit, 1.0, gsel)
        work = jnp.where(hit, -jnp.inf, work)
    emask = jnp.concatenate([jnp.broadcast_to(gsel[g:g + 1, :], (GROUP_SIZE, tr)) for g in range(N_GROUPS)], axis=0)
    work = jnp.where(emask > 0.5, biased, NEG_INF)

    iota_e = lax.broadcasted_iota(jnp.int32, (N_EXPERTS, tr), 0)
    hits, gates = [], []
    chosen = jnp.zeros((N_EXPERTS, tr), F32)
    for _ in range(TOP_K):
        _, ei = _first_argmax(work, iota_e, N_EXPERTS)
        hit = iota_e == ei
        hits.append(hit)
        gates.append(jnp.sum(jnp.where(hit, scores, 0.0), axis=0, keepdims=True))
        chosen = jnp.where(hit, 1.0, chosen)
        work = jnp.where(hit, -jnp.inf, work)
    gsum = gates[0]
    for gk in gates[1:]:
        gsum = gsum + gk
    gate_ref[...] = jnp.concatenate([gk / gsum * ROUTE_SCALE for gk in gates] + [jnp.zeros((8 - TOP_K, tr), F32)], axis=0)

    r = lax.broadcasted_iota(jnp.int32, (tr, tr), 0)
    c = lax.broadcasted_iota(jnp.int32, (tr, tr), 1)
    rank = _dot(chosen.astype(BF16), jnp.where(r < c, 1.0, 0.0).astype(BF16))
    run_len = jnp.ceil(jnp.sum(chosen, axis=1, keepdims=True) * (1.0 / RUN_ALIGN)) * RUN_ALIGN
    er = lax.broadcasted_iota(jnp.int32, (N_EXPERTS, N_EXPERTS), 0)
    ec = lax.broadcasted_iota(jnp.int32, (N_EXPERTS, N_EXPERTS), 1)
    units = jnp.broadcast_to(run_len * (1.0 / RUN_ALIGN), (N_EXPERTS, 128)).astype(BF16)
    run_off = _dot(jnp.where(ec < er, 1.0, 0.0).astype(BF16), units)[:, :1] * RUN_ALIGN
    local = [jnp.sum(jnp.where(h, run_off + rank, 0.0), axis=0, keepdims=True) for h in hits]
    lp = jnp.concatenate(local + [jnp.zeros((8 - TOP_K, tr), F32)], axis=0)
    lp_ref[...] = lp.astype(jnp.int32)

    lp128 = jnp.concatenate([lp, jnp.zeros((128 - 8, tr), F32)], axis=0)
    hi = jnp.floor(lp128 * (1.0 / 64.0))
    lo = lp128 - 64.0 * hi
    eye = jnp.where(r == c, 1.0, 0.0).astype(BF16)
    lpt_ref[...] = 64.0 * _dot_nt(eye, hi.astype(BF16)) + _dot_nt(eye, lo.astype(BF16))

    def as_row(col):
        row = jnp.sum(jnp.where(er == ec, jnp.broadcast_to(col, (N_EXPERTS, N_EXPERTS)), 0.0), axis=0, keepdims=True)
        return jnp.concatenate([row, jnp.zeros((1, 128 - N_EXPERTS), F32)], axis=1)

    tab = jnp.concatenate([as_row(run_len), as_row(run_off), as_row(carry_ref[...]), jnp.zeros((5, 128), F32)], axis=0)
    tab_ref[...] = tab.astype(jnp.int32)
    carry_ref[...] = carry_ref[...] + run_len
    cnt_ref[...] = jnp.broadcast_to(carry_ref[...], cnt_ref.shape)


def _router(x1, router_w, router_b):
    t = x1.shape[0]
    tr = MOE_TT
    nt = t // tr
    col = pl.BlockSpec((8, tr), lambda i: (0, i))
    return pl.pallas_call(
        _router_kernel,
        grid=(nt,),
        in_specs=[pl.BlockSpec((tr, D_MODEL), lambda i: (i, 0)),
                  pl.BlockSpec((N_EXPERTS, D_MODEL), lambda i: (0, 0)),
                  pl.BlockSpec((N_EXPERTS, 1), lambda i: (0, 0))],
        out_specs=[col, pl.BlockSpec((tr, 128), lambda i: (i, 0)), col,
                   pl.BlockSpec((None, 8, 128), lambda i: (i, 0, 0)),
                   pl.BlockSpec((N_EXPERTS, 128), lambda i: (0, 0))],
        out_shape=[jax.ShapeDtypeStruct((8, t), jnp.int32), jax.ShapeDtypeStruct((t, 128), F32),
                   jax.ShapeDtypeStruct((8, t), F32), jax.ShapeDtypeStruct((nt, 8, 128), jnp.int32),
                   jax.ShapeDtypeStruct((N_EXPERTS, 128), F32)],
        scratch_shapes=[pltpu.VMEM((N_EXPERTS, 1), F32)],
        compiler_params=_params("arbitrary"),
        name="moe_router",
    )(x1, router_w, router_b.reshape(-1, 1))


def _run_copies(pstart_ref, tab_ref, make_copy, action):
    def per_expert(e, carry):
        length = tab_ref[0, e]
        local = tab_ref[1, e]
        glob = pstart_ref[e] + tab_ref[2, e]
        done = 0
        for size in RUN_PIECES:
            piece = length & size

            @pl.when(piece != 0)
            def _():
                cp = make_copy(pl.multiple_of(local + done, RUN_ALIGN), pl.multiple_of(glob + done, RUN_ALIGN), size)
                cp.start() if action == "start" else cp.wait()

            done = done + piece
        return carry

    lax.fori_loop(0, N_EXPERTS, per_expert, 0)


def _dispatch_kernel(pstart_ref, tail_ref, taillen_ref, nused_ref, tab_ref, lp_ref, gate_ref, x_ref, o_hbm,
                     buf_ref, sem):
    i = pl.program_id(0)

    def make_copy(local, glob, size):
        return pltpu.make_async_copy(buf_ref.at[pl.ds(local, size)], o_hbm.at[pl.ds(glob, size)], sem)

    @pl.when(i == 0)
    def _():
        buf_ref[0:MOE_BLK, :] = jnp.zeros((MOE_BLK, ROW_W), BF16)
        for action in ("start", "wait"):
            def per_expert(e, carry, action=action):
                done = 0
                for size in RUN_PIECES:
                    piece = taillen_ref[e] & size

                    @pl.when(piece != 0)
                    def _():
                        cp = make_copy(0, pl.multiple_of(tail_ref[e] + done, RUN_ALIGN), size)
                        cp.start() if action == "start" else cp.wait()

                    done = done + piece
                return carry
            lax.fori_loop(0, N_EXPERTS, per_expert, 0)

            def per_block(r, carry, action=action):
                cp = make_copy(0, pl.multiple_of(r * MOE_BLK, MOE_BLK), MOE_BLK)
                cp.start() if action == "start" else cp.wait()
                return carry
            lax.fori_loop(nused_ref[0], o_hbm.shape[0] // MOE_BLK, per_block, 0)

    lp = lp_ref[...]
    gates = gate_ref[...]
    xb = x_ref[...]
    glane = lax.broadcasted_iota(jnp.int32, (1, ROW_W - D_MODEL), 1)
    for c in range(MOE_LR // MOE_CHUNK):
        rows = c * MOE_CHUNK + lax.broadcasted_iota(jnp.int32, (MOE_CHUNK, MOE_TT), 0)
        member = rows == lp[0:1, :]
        weight = jnp.where(member, gates[0:1, :], 0.0)
        for k in range(1, TOP_K):
            hit = rows == lp[k:k + 1, :]
            member = member | hit
            weight = jnp.where(hit, gates[k:k + 1, :], weight)
        picked = _dot(jnp.where(member, 1.0, 0.0).astype(BF16), xb)
        row_gate = jnp.sum(weight, axis=1, keepdims=True)
        sl = slice(c * MOE_CHUNK, (c + 1) * MOE_CHUNK)
        buf_ref[sl, :D_MODEL] = picked.astype(BF16)
        g_hi = row_gate.astype(BF16).astype(F32)
        buf_ref[sl, D_MODEL:] = jnp.where(glane == 0, g_hi, jnp.where(glane == 1, row_gate - g_hi, 0.0)).astype(BF16)

    _run_copies(pstart_ref, tab_ref, make_copy, "start")
    _run_copies(pstart_ref, tab_ref, make_copy, "wait")


def _dispatch(pstart, tail_start, tail_len, n_used, tab, lp, gates, x1b, rows):
    t = x1b.shape[0]
    tt = MOE_TT
    grid_spec = pltpu.PrefetchScalarGridSpec(
        num_scalar_prefetch=4,
        grid=(t // tt,),
        in_specs=[pl.BlockSpec((None, 8, 128), lambda i, *_: (i, 0, 0), memory_space=pltpu.SMEM),
                  pl.BlockSpec((8, tt), lambda i, *_: (0, i)),
                  pl.BlockSpec((8, tt), lambda i, *_: (0, i)),
                  pl.BlockSpec((tt, D_MODEL), lambda i, *_: (i, 0))],
        out_specs=pl.BlockSpec(memory_space=pl.ANY),
        scratch_shapes=[pltpu.VMEM((MOE_LR, ROW_W), BF16), pltpu.SemaphoreType.DMA],
    )
    return pl.pallas_call(
        _dispatch_kernel,
        grid_spec=grid_spec,
        out_shape=jax.ShapeDtypeStruct((rows, ROW_W), BF16),
        compiler_params=_params("arbitrary"),
        name="moe_dispatch",
    )(pstart, tail_start, tail_len, n_used, tab, lp, gates, x1b)


def _expert_kernel(be_ref, nu_ref, x_ref, wg_ref, wu_ref, wd_ref, o_ref, wgb_ref, wub_ref, wdb_ref):
    r = pl.program_id(0)

    @pl.when((r == 0) | (be_ref[r] != be_ref[jnp.maximum(r - 1, 0)]))
    def _():
        wgb_ref[...] = wg_ref[...].astype(BF16)
        wub_ref[...] = wu_ref[...].astype(BF16)
        wdb_ref[...] = wd_ref[...].astype(BF16)

    @pl.when(r < nu_ref[0])
    def _():
        xb = x_ref[:, :D_MODEL]
        gate_terms = x_ref[:, D_MODEL:D_MODEL + 2].astype(F32)
        row_gate = gate_terms[:, 0:1] + gate_terms[:, 1:2]
        h = _silu(_dot(xb, wgb_ref[...])) * _dot(xb, wub_ref[...])
        o_ref[...] = (_dot(h.astype(BF16), wdb_ref[...]) * row_gate).astype(o_ref.dtype)

    @pl.when(r >= nu_ref[0])
    def _():
        o_ref[...] = jnp.zeros_like(o_ref)


def _experts(blk_exp, n_used, xs, wg, wu, wd, layer):
    rows = xs.shape[0]
    nblk = rows // MOE_BLK
    grid_spec = pltpu.PrefetchScalarGridSpec(
        num_scalar_prefetch=2,
        grid=(nblk,),
        in_specs=[pl.BlockSpec((MOE_BLK, ROW_W), lambda r, be, nu: (jnp.minimum(r, jnp.maximum(nu[0] - 1, 0)), 0)),
                  pl.BlockSpec((None, None, D_MODEL, D_EXPERT), lambda r, be, nu: (layer, be[r], 0, 0)),
                  pl.BlockSpec((None, None, D_MODEL, D_EXPERT), lambda r, be, nu: (layer, be[r], 0, 0)),
                  pl.BlockSpec((None, None, D_EXPERT, D_MODEL), lambda r, be, nu: (layer, be[r], 0, 0))],
        out_specs=pl.BlockSpec((MOE_BLK, D_MODEL), lambda r, be, nu: (r, 0)),
        scratch_shapes=[pltpu.VMEM((D_MODEL, D_EXPERT), BF16), pltpu.VMEM((D_MODEL, D_EXPERT), BF16),
                        pltpu.VMEM((D_EXPERT, D_MODEL), BF16)],
    )
    return pl.pallas_call(
        _expert_kernel,
        grid_spec=grid_spec,
        out_shape=jax.ShapeDtypeStruct((rows, D_MODEL), BF16),
        compiler_params=_params("arbitrary"),
        name="moe_experts",
    )(blk_exp, n_used, xs, wg, wu, wd)


def _combine_kernel(pstart_ref, tab_ref, lpt_ref, x_ref, y_hbm, swg_ref, swu_ref, swd_ref, g_ref, b_ref, o_ref,
                    buf_ref, sem):
    i = pl.program_id(0)

    @pl.when(i == 0)
    def _():
        buf_ref[...] = jnp.zeros_like(buf_ref)

    def make_copy(local, glob, size):
        return pltpu.make_async_copy(y_hbm.at[pl.ds(glob, size)], buf_ref.at[pl.ds(local, size)], sem)

    _run_copies(pstart_ref, tab_ref, make_copy, "start")

    x = x_ref[...]
    xb = x.astype(BF16)
    h = _silu(_dot(xb, swg_ref[...])) * _dot(xb, swu_ref[...])
    acc = ALPHA * x + _dot(h.astype(BF16), swd_ref[...])

    _run_copies(pstart_ref, tab_ref, make_copy, "wait")

    lpt = lpt_ref[...].astype(jnp.int32)
    for c in range(MOE_LR // MOE_CHUNK):
        cols = c * MOE_CHUNK + lax.broadcasted_iota(jnp.int32, (MOE_TT, MOE_CHUNK), 1)
        member = cols == lpt[:, 0:1]
        for k in range(1, TOP_K):
            member = member | (cols == lpt[:, k:k + 1])
        sel = jnp.where(member, 1.0, 0.0).astype(BF16)
        acc = acc + _dot(sel, buf_ref[c * MOE_CHUNK:(c + 1) * MOE_CHUNK, :])
    o_ref[...] = _layer_norm(acc, g_ref[...], b_ref[...])


def _combine(pstart, tab, lpt, x1, y_rows, swg, swu, swd, g, b):
    t = x1.shape[0]
    tt = MOE_TT

    def whole(arr):
        return pl.BlockSpec(arr.shape, lambda i, ps: (0, 0))

    g2, b2 = g.reshape(1, -1), b.reshape(1, -1)
    grid_spec = pltpu.PrefetchScalarGridSpec(
        num_scalar_prefetch=1,
        grid=(t // tt,),
        in_specs=[pl.BlockSpec((None, 8, 128), lambda i, ps: (i, 0, 0), memory_space=pltpu.SMEM),
                  pl.BlockSpec((tt, 128), lambda i, ps: (i, 0)),
                  pl.BlockSpec((tt, D_MODEL), lambda i, ps: (i, 0)),
                  pl.BlockSpec(memory_space=pl.ANY),
                  whole(swg), whole(swu), whole(swd), whole(g2), whole(b2)],
        out_specs=pl.BlockSpec((tt, D_MODEL), lambda i, ps: (i, 0)),
        scratch_shapes=[pltpu.VMEM((MOE_LR, D_MODEL), BF16), pltpu.SemaphoreType.DMA],
    )
    return pl.pallas_call(
        _combine_kernel,
        grid_spec=grid_spec,
        out_shape=jax.ShapeDtypeStruct((t, D_MODEL), F32),
        compiler_params=_params("arbitrary"),
        name="moe_combine_shared_ln",
    )(pstart, tab, lpt, x1, y_rows, swg, swu, swd, g2, b2)


def _moe(x1, x1b, layer, router_w, router_b, wg, wu, wd, swg, swu, swd, g, b):
    t = x1.shape[0]
    lp, lpt, gates, tab, cnt = _router(x1, router_w, router_b)
    rows_e = cnt[:, 0].astype(jnp.int32)
    padded = (rows_e + MOE_BLK - 1) // MOE_BLK * MOE_BLK
    pend = jnp.cumsum(padded)
    pstart = pend - padded
    max_rows = t * TOP_K + (t // MOE_TT) * N_EXPERTS * (RUN_ALIGN - 1) + N_EXPERTS * (MOE_BLK - 1)
    n_blocks = -(-max_rows // MOE_BLK)
    blk_first = jnp.arange(n_blocks, dtype=jnp.int32) * MOE_BLK
    blk_exp = jnp.minimum(jnp.sum((pend[None, :] <= blk_first[:, None]).astype(jnp.int32), axis=1), N_EXPERTS - 1)
    n_used = (pend[-1:] // MOE_BLK).astype(jnp.int32)
    xs = _dispatch(pstart, pstart + rows_e, padded - rows_e, n_used, tab, lp, gates, x1b, n_blocks * MOE_BLK)
    y_rows = _experts(blk_exp, n_used, xs, wg, wu, wd, layer)
    return _combine(pstart, tab, lpt, x1, y_rows, swg, swu, swd, g, b)


def _layer(x2d, bsz, seq, layer, w_in, moe_wg, moe_wu, moe_wd,
           conv_dw, conv_db, conv_ln_g, conv_ln_b, swa_sink, ret_gn_g, ret_gn_b,
           nsa_pe_k, nsa_w1_k, nsa_w2_k, nsa_pe_v, nsa_w1_v, nsa_w2_v, p_conv, p_swa, p_ret, p_nsa, w_out,
           ln1_g, ln1_b, router_w, router_b, sh_wg, sh_wu, sh_wd, ln2_g, ln2_b):
    bf = lambda a: a.astype(BF16)
    w_a = _mix_weights(w_in, layer)
    w_m = _merge_gate_weights(w_in, layer)
    ua, kcvc = _inproj(x2d, w_a)
    conv_a = _conv(ua, conv_dw, conv_db, conv_ln_g, conv_ln_b, bsz, seq)
    swa_o = _swa(ua, swa_sink, bsz, seq)
    ret_o = _retention(ua, ret_gn_g, ret_gn_b, bsz, seq)
    kcmp, vcmp = _compress(kcvc, nsa_pe_k, bf(nsa_w1_k), bf(nsa_w2_k), nsa_pe_v, bf(nsa_w1_v), bf(nsa_w2_v), bsz, seq)
    nsa_o = _nsa(ua, kcmp, vcmp, bsz, seq)
    x1, x1b = _merge(x2d, conv_a, swa_o, ret_o, nsa_o, w_m, bf(p_conv), bf(p_swa), bf(p_ret), bf(p_nsa),
                     bf(w_out), ln1_g, ln1_b)
    return _moe(x1, x1b, layer, router_w, router_b, moe_wg, moe_wu, moe_wd,
                bf(sh_wg), bf(sh_wu), bf(sh_wd), ln2_g, ln2_b)


def kernel(x, w_in, conv_dw, conv_db, conv_ln_g, conv_ln_b, swa_sink, ret_gn_g, ret_gn_b, nsa_pe_k, nsa_w1_k, nsa_w2_k, nsa_pe_v, nsa_w1_v, nsa_w2_v, p_conv, p_swa, p_ret, p_nsa, w_out, ln1_g, ln1_b, router_w, router_b, moe_wg, moe_wu, moe_wd, sh_wg, sh_wu, sh_wd, ln2_g, ln2_b):
    bsz, seq, d = x.shape
    params = (conv_dw, conv_db, conv_ln_g, conv_ln_b, swa_sink, ret_gn_g, ret_gn_b,
              nsa_pe_k, nsa_w1_k, nsa_w2_k, nsa_pe_v, nsa_w1_v, nsa_w2_v, p_conv, p_swa, p_ret, p_nsa, w_out,
              ln1_g, ln1_b, router_w, router_b, sh_wg, sh_wu, sh_wd, ln2_g, ln2_b)
    x2d = x.reshape(bsz * seq, d)
    for l in range(w_in.shape[0]):
        x2d = _layer(x2d, bsz, seq, l, w_in, moe_wg, moe_wu, moe_wd, *[p[l] for p in params])
    return x2d.reshape(bsz, seq, d)
```

```python
import functools
import math

import jax
import jax.numpy as jnp
import numpy as np
from jax import lax
from jax.experimental import pallas as pl
from jax.experimental.pallas import tpu as pltpu

F32 = jnp.float32
BF16 = jnp.bfloat16

D_MODEL = 1024
DEPTH = 2
HEAD_DIM = 64
BLK = 128
CONV_CH = 256
CONV_K = 31
SWA_HEADS = 4
SWA_KV_HEADS = 2
SWA_WINDOW = 128
RET_HEADS = 4
RET_QK = 64
RET_V = 128
NSA_HEADS = 4
NSA_CMP_LEN = 32
NSA_CMP_STRIDE = 16
NSA_CMP_HIDDEN = 256
NSA_SLC_LEN = 64
NSA_TOPN = 8
NSA_WINDOW = 512
NSA_FORCE = 1e4
N_EXPERTS = 64
TOP_K = 6
N_GROUPS = 8
GROUP_SIZE = N_EXPERTS // N_GROUPS
TOPK_GROUPS = 4
D_EXPERT = 256
ROUTE_SCALE = 2.5
MOE_BLK = 512
N_BRANCH = 4
ALPHA = (2 * DEPTH) ** 0.25
LN_EPS = 1e-5
NEG_INF = -1e30
SCALE = HEAD_DIM ** -0.5

C_CONV = 0
C_SWA = 512
C_RETQK = 1024
C_RETV = 1536
C_RETG = 2048
C_NSAQ = 2560
C_NSAC = 2816
C_NSAS = 2944
C_NSAW = 3072
C_NSAG = 3200
N_MIX_COLS = 3212
UA = 3328

ALIBI = [2.0 ** (-8.0 * (i + 1) / 8) for i in range(8)]
SLOPES_SWA = ALIBI[:4]
SLOPES_NSA = ALIBI[4:]
RET_LOG_GAMMA = [math.log1p(-(2.0 ** (-5.0 - h))) for h in range(RET_HEADS)]

VMEM_LIMIT = 48 * 1024 * 1024
NT_DIMS = (((1,), (1,)), ((), ()))
TN_DIMS = (((0,), (0,)), ((), ()))


def _params(*sem):
    return pltpu.CompilerParams(dimension_semantics=sem, vmem_limit_bytes=VMEM_LIMIT)


def _dot(a, b):
    return jnp.dot(a, b, preferred_element_type=F32)


def _dot_nt(a, b):
    return lax.dot_general(a, b, NT_DIMS, preferred_element_type=F32)


def _layer_norm(v, g, b):
    mu = jnp.mean(v, axis=-1, keepdims=True)
    c = v - mu
    var = jnp.mean(c * c, axis=-1, keepdims=True)
    return c * lax.rsqrt(var + LN_EPS) * g + b


def _silu(v):
    return v * jax.nn.sigmoid(v)


CAST_TM = 256
LANES = 128


def _cast_kernel(x_ref, o_ref):
    o_ref[...] = x_ref[...].astype(o_ref.dtype)


def _mix_weights(w_in, layer):
    return pl.pallas_call(
        _cast_kernel,
        grid=(D_MODEL // CAST_TM,),
        in_specs=[pl.BlockSpec((None, CAST_TM, UA), lambda i: (layer, i, 0))],
        out_specs=pl.BlockSpec((CAST_TM, UA), lambda i: (i, 0)),
        out_shape=jax.ShapeDtypeStruct((D_MODEL, UA), BF16),
        compiler_params=_params("parallel"),
        name="cast_mix_weights",
    )(w_in)


def _shifted_cast_kernel(a_ref, b_ref, o_ref):
    shift = N_MIX_COLS % LANES
    both = jnp.concatenate([a_ref[...], b_ref[...]], axis=1)
    o_ref[...] = both[:, shift:shift + LANES].astype(o_ref.dtype)


def _merge_gate_weights(w_in, layer):
    first = N_MIX_COLS // LANES
    width = N_BRANCH * D_MODEL
    return pl.pallas_call(
        _shifted_cast_kernel,
        grid=(width // LANES,),
        in_specs=[pl.BlockSpec((None, D_MODEL, LANES), lambda j: (layer, 0, first + j)),
                  pl.BlockSpec((None, D_MODEL, LANES), lambda j: (layer, 0, first + j + 1))],
        out_specs=pl.BlockSpec((D_MODEL, LANES), lambda j: (0, j)),
        out_shape=jax.ShapeDtypeStruct((D_MODEL, width), BF16),
        compiler_params=_params("parallel"),
        name="cast_merge_gate_weights",
    )(w_in, w_in)


INPROJ_TM = 512


def _inproj_kernel(x_ref, w_ref, o_ref, oc_ref):
    xb = x_ref[...].astype(BF16)
    for c in range(0, UA, 256):
        o_ref[:, c:c + 256] = _dot(xb, w_ref[:, c:c + 256]).astype(o_ref.dtype)
    oc_ref[...] = _dot(xb, w_ref[:, C_NSAC:C_NSAC + 128])


def _inproj(x2d, w_a):
    t = x2d.shape[0]
    return pl.pallas_call(
        _inproj_kernel,
        grid=(t // INPROJ_TM,),
        in_specs=[pl.BlockSpec((INPROJ_TM, D_MODEL), lambda i: (i, 0)),
                  pl.BlockSpec((D_MODEL, UA), lambda i: (0, 0))],
        out_specs=[pl.BlockSpec((INPROJ_TM, UA), lambda i: (i, 0)),
                   pl.BlockSpec((INPROJ_TM, 128), lambda i: (i, 0))],
        out_shape=[jax.ShapeDtypeStruct((t, UA), BF16), jax.ShapeDtypeStruct((t, 128), F32)],
        compiler_params=_params("parallel"),
        name="inproj",
    )(x2d, w_a)


CONV_TS = 256
CONV_HALO = 32


def _conv_kernel(cur_ref, halo_ref, dw_ref, db_ref, g_ref, b_ref, o_ref, hc_ref):
    i = pl.program_id(1)

    def glu(u):
        u = u.astype(F32)
        return u[:, :CONV_CH] * jax.nn.sigmoid(u[:, CONV_CH:])

    hc_ref[0:CONV_HALO, :] = jnp.where(i > 0, glu(halo_ref[...]), 0.0)
    hc_ref[CONV_HALO:CONV_HALO + CONV_TS, :] = glu(cur_ref[...])
    acc = jnp.broadcast_to(db_ref[...], (CONV_TS, CONV_CH))
    off = CONV_HALO - (CONV_K - 1)
    for j in range(CONV_K):
        acc = acc + hc_ref[off + j:off + j + CONV_TS, :] * dw_ref[j:j + 1, :]
    o_ref[...] = _silu(_layer_norm(acc, g_ref[...], b_ref[...])).astype(o_ref.dtype)


def _conv(ua, dw, db, g, b, bsz, seq):
    t = bsz * seq
    ns = seq // CONV_TS
    per = CONV_TS // CONV_HALO
    dwp = jnp.concatenate([dw, jnp.zeros((32 - CONV_K, CONV_CH), F32)], axis=0)
    vec = pl.BlockSpec((1, CONV_CH), lambda bb, i: (0, 0))
    return pl.pallas_call(
        _conv_kernel,
        grid=(bsz, ns),
        in_specs=[pl.BlockSpec((CONV_TS, 512), lambda bb, i: (bb * ns + i, C_CONV // 512)),
                  pl.BlockSpec((CONV_HALO, 512),
                               lambda bb, i: (jnp.maximum((bb * ns + i) * per - 1, 0), C_CONV // 512)),
                  pl.BlockSpec((32, CONV_CH), lambda bb, i: (0, 0)), vec, vec, vec],
        out_specs=pl.BlockSpec((CONV_TS, CONV_CH), lambda bb, i: (bb * ns + i, 0)),
        out_shape=jax.ShapeDtypeStruct((t, CONV_CH), BF16),
        scratch_shapes=[pltpu.VMEM((CONV_HALO + CONV_TS, CONV_CH), F32)],
        compiler_params=_params("parallel", "parallel"),
        name="conformer_conv",
    )(ua, ua, dwp, db.reshape(1, -1), g.reshape(1, -1), b.reshape(1, -1))


def _swa_kernel(sink_ref, q_ref, kvc_ref, kvp_ref, o_ref):
    i = pl.program_id(1)
    q = q_ref[...]
    kvc = kvc_ref[...]
    kvp = kvp_ref[...]
    a = lax.broadcasted_iota(jnp.int32, (BLK, 2 * BLK), 0)
    c = lax.broadcasted_iota(jnp.int32, (BLK, 2 * BLK), 1)
    dist = BLK + a - c
    mask = (dist >= 0) & (dist < SWA_WINDOW) & ((c >= BLK) | (i > 0))
    distf = dist.astype(F32)
    rep = SWA_HEADS // SWA_KV_HEADS
    for h in range(SWA_HEADS):
        g = h // rep
        k = jnp.concatenate([kvp[:, g * HEAD_DIM:(g + 1) * HEAD_DIM],
                             kvc[:, g * HEAD_DIM:(g + 1) * HEAD_DIM]], axis=0)
        v = jnp.concatenate([kvp[:, BLK + g * HEAD_DIM:BLK + (g + 1) * HEAD_DIM],
                             kvc[:, BLK + g * HEAD_DIM:BLK + (g + 1) * HEAD_DIM]], axis=0)
        s = _dot_nt(q[:, h * HEAD_DIM:(h + 1) * HEAD_DIM], k) * SCALE - SLOPES_SWA[h] * distf
        s = jnp.where(mask, s, NEG_INF)
        sink = sink_ref[h]
        m = jnp.maximum(jnp.max(s, axis=-1, keepdims=True), sink)
        e = jnp.where(mask, jnp.exp(s - m), 0.0)
        den = jnp.sum(e, axis=-1, keepdims=True) + jnp.exp(sink - m)
        p = e / jnp.maximum(den, 1e-30)
        o_ref[:, h * HEAD_DIM:(h + 1) * HEAD_DIM] = _dot(p.astype(BF16), v).astype(o_ref.dtype)


def _swa(ua, sink, bsz, seq):
    t = bsz * seq
    nb = seq // BLK
    return pl.pallas_call(
        _swa_kernel,
        grid=(bsz, nb),
        in_specs=[pl.BlockSpec(memory_space=pltpu.SMEM),
                  pl.BlockSpec((BLK, 256), lambda bb, i: (bb * nb + i, C_SWA // 256)),
                  pl.BlockSpec((BLK, 256), lambda bb, i: (bb * nb + i, C_SWA // 256 + 1)),
                  pl.BlockSpec((BLK, 256), lambda bb, i: (jnp.maximum(bb * nb + i - 1, 0), C_SWA // 256 + 1))],
        out_specs=pl.BlockSpec((BLK, 256), lambda bb, i: (bb * nb + i, 0)),
        out_shape=jax.ShapeDtypeStruct((t, SWA_HEADS * HEAD_DIM), BF16),
        compiler_params=_params("parallel", "parallel"),
        name="swa",
    )(sink, ua, ua, ua)


def _ret_kernel(qk_ref, v_ref, g_ref, gng_ref, gnb_ref, o_ref, state_ref):
    n = pl.program_id(1)

    @pl.when(n == 0)
    def _():
        state_ref[...] = jnp.zeros_like(state_ref)

    qk = qk_ref[...]
    vv = v_ref[...]
    gate = g_ref[...].astype(F32)
    a = lax.broadcasted_iota(jnp.int32, (BLK, BLK), 0)
    c = lax.broadcasted_iota(jnp.int32, (BLK, BLK), 1)
    diff = (a - c).astype(F32)
    idx = lax.broadcasted_iota(jnp.int32, (BLK, 1), 0).astype(F32)
    hq = RET_HEADS * RET_QK
    for h in range(RET_HEADS):
        lg = RET_LOG_GAMMA[h]
        q = qk[:, h * RET_QK:(h + 1) * RET_QK]
        k = qk[:, hq + h * RET_QK:hq + (h + 1) * RET_QK].astype(F32) * (RET_QK ** -0.5)
        v = vv[:, h * RET_V:(h + 1) * RET_V]
        dmat = jnp.where(diff >= 0, jnp.exp(lg * jnp.maximum(diff, 0.0)), 0.0)
        inner = _dot_nt(q, k.astype(BF16)) * dmat
        o_inner = _dot(inner.astype(BF16), v)
        zeta = jnp.exp(lg * (BLK - 1 - idx))
        xi = jnp.exp(lg * (idx + 1.0))
        kv = lax.dot_general((k * zeta).astype(BF16), v, TN_DIMS, preferred_element_type=F32)
        prev = state_ref[h]
        o = o_inner + _dot(q, prev.astype(BF16)) * xi
        state_ref[h] = math.exp(lg * BLK) * prev + kv
        mu = jnp.mean(o, axis=-1, keepdims=True)
        cen = o - mu
        var = jnp.mean(cen * cen, axis=-1, keepdims=True)
        sl = slice(h * RET_V, (h + 1) * RET_V)
        on = cen * lax.rsqrt(var + LN_EPS) * gng_ref[:, sl] + gnb_ref[:, sl]
        o_ref[:, sl] = (_silu(gate[:, sl]) * on).astype(o_ref.dtype)


def _retention(ua, gn_g, gn_b, bsz, seq):
    t = bsz * seq
    nc = seq // BLK
    vec = pl.BlockSpec((1, RET_HEADS * RET_V), lambda bb, i: (0, 0))
    return pl.pallas_call(
        _ret_kernel,
        grid=(bsz, nc),
        in_specs=[pl.BlockSpec((BLK, 512), lambda bb, i: (bb * nc + i, C_RETQK // 512)),
                  pl.BlockSpec((BLK, 512), lambda bb, i: (bb * nc + i, C_RETV // 512)),
                  pl.BlockSpec((BLK, 512), lambda bb, i: (bb * nc + i, C_RETG // 512)),
                  vec, vec],
        out_specs=pl.BlockSpec((BLK, 512), lambda bb, i: (bb * nc + i, 0)),
        out_shape=jax.ShapeDtypeStruct((t, RET_HEADS * RET_V), BF16),
        scratch_shapes=[pltpu.VMEM((RET_HEADS, RET_QK, RET_V), F32)],
        compiler_params=_params("parallel", "arbitrary"),
        name="retention",
    )(ua, ua, ua, gn_g.reshape(1, -1), gn_b.reshape(1, -1))


def _compress_kernel(t_ref, pek_ref, w1k_ref, w2k_ref, pev_ref, w1v_ref, w2v_ref, ok_ref, ov_ref):
    st = NSA_CMP_STRIDE
    ng = t_ref.shape[0] // st
    branches = ((pek_ref, w1k_ref, w2k_ref, ok_ref), (pev_ref, w1v_ref, w2v_ref, ov_ref))
    first = [jnp.zeros((ng, NSA_CMP_HIDDEN), F32) for _ in branches]
    second = [jnp.zeros((ng, NSA_CMP_HIDDEN), F32) for _ in branches]
    for r in range(st):
        rows = t_ref[pl.ds(r, ng, stride=st), :]
        for n, (pe_ref, w1_ref, _, _) in enumerate(branches):
            v = rows[:, n * HEAD_DIM:(n + 1) * HEAD_DIM]
            lo = (v + pe_ref[r:r + 1, :]).astype(BF16)
            hi = (v + pe_ref[st + r:st + r + 1, :]).astype(BF16)
            first[n] = first[n] + _dot(lo, w1_ref[r * HEAD_DIM:(r + 1) * HEAD_DIM, :])
            second[n] = second[n] + _dot(hi, w1_ref[(st + r) * HEAD_DIM:(st + r + 1) * HEAD_DIM, :])
    for n, (_, _, w2_ref, o_ref) in enumerate(branches):
        hid = first[n] + pltpu.roll(second[n], ng - 1, axis=0)
        act = 0.5 * hid * (1.0 + lax.erf(hid * (2.0 ** -0.5)))
        o_ref[0] = _dot(act.astype(BF16), w2_ref[...]).astype(o_ref.dtype)


def _compress(kcvc, pe_k, w1_k, w2_k, pe_v, w1_v, w2_v, bsz, seq):
    ng = seq // NSA_CMP_STRIDE
    pes = pl.BlockSpec((NSA_CMP_LEN, HEAD_DIM), lambda bb: (0, 0))
    w1s = pl.BlockSpec((NSA_CMP_LEN * HEAD_DIM, NSA_CMP_HIDDEN), lambda bb: (0, 0))
    w2s = pl.BlockSpec((NSA_CMP_HIDDEN, HEAD_DIM), lambda bb: (0, 0))
    osp = pl.BlockSpec((1, ng, HEAD_DIM), lambda bb: (bb, 0, 0))
    return pl.pallas_call(
        _compress_kernel,
        grid=(bsz,),
        in_specs=[pl.BlockSpec((seq, 128), lambda bb: (bb, 0)), pes, w1s, w2s, pes, w1s, w2s],
        out_specs=[osp, osp],
        out_shape=[jax.ShapeDtypeStruct((bsz, ng, HEAD_DIM), BF16)] * 2,
        compiler_params=_params("parallel"),
        name="nsa_compress",
    )(kcvc, pe_k, w1_k, w2_k, pe_v, w1_v, w2_v)


NSA_KC = 512
NSA_WSPAN = NSA_WINDOW + BLK


NSA_PAD = 128
Q_LANES = NSA_HEADS * BLK


def _nsa_query_select():
    sel = np.zeros((NSA_HEADS * HEAD_DIM, NSA_HEADS * NSA_PAD), np.float32)
    for h in range(NSA_HEADS):
        for dd in range(HEAD_DIM):
            sel[h * HEAD_DIM + dd, h * NSA_PAD + dd] = SCALE
    return jnp.asarray(sel, BF16)


def _softmax_down(s):
    m = jnp.maximum(jnp.max(s, axis=0, keepdims=True), -1e29)
    e = jnp.exp(s - m)
    return e / jnp.maximum(jnp.sum(e, axis=0, keepdims=True), 1e-30)


def _nsa_kernel(q_ref, qsel_ref, kcmp_ref, vcmp_ref, ksvs_ref, kwvw_ref, gate_ref, o_ref,
                ksa_ref, kwa_ref, kca_ref, vst_ref, vwt_ref, vct_ref, ext_ref, bsel_ref):
    i = pl.program_id(1)
    seq = ksvs_ref.shape[0]
    ncmp = kcmp_ref.shape[1]
    nslc = seq // NSA_SLC_LEN
    lane = lax.broadcasted_iota(jnp.int32, (1, NSA_PAD), 1)

    @pl.when(i == 0)
    def _():
        def aug(k, pos):
            kf = jnp.concatenate([k.astype(F32), jnp.zeros((k.shape[0], NSA_PAD - HEAD_DIM), F32)], axis=1)
            hi = (pos // 128).astype(F32)
            lo = (pos % 128).astype(F32)
            return (kf + jnp.where(lane == HEAD_DIM, hi, jnp.where(lane == HEAD_DIM + 1, lo, 0.0))).astype(BF16)

        def value_t(v):
            vf = jnp.concatenate([v.astype(F32), jnp.zeros((v.shape[0], NSA_PAD - HEAD_DIM), F32)], axis=1)
            return vf.T.astype(BF16)

        kpos = lax.broadcasted_iota(jnp.int32, (seq, 1), 0)
        ksvs = ksvs_ref[...]
        kwvw = kwvw_ref[...]
        ksa_ref[...] = aug(ksvs[:, :HEAD_DIM], kpos)
        kwa_ref[...] = aug(kwvw[:, :HEAD_DIM], kpos)
        cend = lax.broadcasted_iota(jnp.int32, (ncmp, 1), 0) * NSA_CMP_STRIDE + (NSA_CMP_LEN - 1)
        kca_ref[...] = aug(kcmp_ref[0], cend)
        vst_ref[...] = value_t(ksvs[:, HEAD_DIM:])
        vwt_ref[...] = value_t(kwvw[:, HEAD_DIM:])
        vct_ref[...] = value_t(vcmp_ref[0])
        ext_ref[...] = jnp.where(kpos // NSA_SLC_LEN == lax.broadcasted_iota(jnp.int32, (seq, NSA_PAD), 1),
                                 1.0, 0.0).astype(BF16)

    t0 = i * BLK
    tq = t0 + lax.broadcasted_iota(jnp.int32, (1, BLK), 1)
    qw = _dot(q_ref[...], qsel_ref[...])
    qaug = []
    for h in range(NSA_HEADS):
        coef = jnp.where(lane == HEAD_DIM, 128.0 * SLOPES_NSA[h], jnp.where(lane == HEAD_DIM + 1, SLOPES_NSA[h], 0.0))
        qaug.append((qw[:, h * NSA_PAD:(h + 1) * NSA_PAD] + coef).astype(BF16))
    qaug = jnp.concatenate(qaug, axis=0)

    def tile4(b):
        return jnp.concatenate([b] * NSA_HEADS, axis=1)

    cend = lax.broadcasted_iota(jnp.int32, (ncmp, 1), 0) * NSA_CMP_STRIDE + (NSA_CMP_LEN - 1)
    p_cmp = _softmax_down(_dot_nt(kca_ref[...], qaug) + tile4(jnp.where(cend <= tq, 0.0, NEG_INF)))
    o_cmp = _dot(vct_ref[...], p_cmp.astype(BF16))
    psum = p_cmp[:, 0:BLK]
    for h in range(1, NSA_HEADS):
        psum = psum + p_cmp[:, h * BLK:(h + 1) * BLK]

    sst = lax.broadcasted_iota(jnp.int32, (nslc, ncmp), 0) * NSA_SLC_LEN
    cst = lax.broadcasted_iota(jnp.int32, (nslc, ncmp), 1) * NSA_CMP_STRIDE
    overlap = jnp.where((cst < sst + NSA_SLC_LEN) & (cst + NSA_CMP_LEN > sst), 1.0, 0.0).astype(BF16)
    p_hi = psum.astype(BF16)
    p_lo = (psum - p_hi.astype(F32)).astype(BF16)
    imp = _dot(overlap, p_hi) + _dot(overlap, p_lo)
    jj = lax.broadcasted_iota(jnp.int32, (nslc, BLK), 0)
    cur = tq // NSA_SLC_LEN
    forced = (jj == 0) | (jj == cur) | (jj == cur - 1)
    imp = jnp.where(forced, NSA_FORCE, jnp.where(jj > cur, -1.0, imp))

    sel = jnp.zeros((nslc, BLK), F32)
    for _ in range(min(NSA_TOPN, nslc)):
        mx = jnp.max(imp, axis=0, keepdims=True)
        first = jnp.min(jnp.where(imp == mx, jj, nslc), axis=0, keepdims=True)
        hit = jj == first
        sel = jnp.where(hit, 1.0, sel)
        imp = jnp.where(hit, -3e38, imp)
    sel = jnp.concatenate([sel, jnp.zeros((NSA_PAD - nslc, BLK), F32)], axis=0).astype(BF16)
    bsel_ref[...] = (_dot(ext_ref[...], sel) - 1.0) * 1e30

    def slc_step(c, carry):
        m, l, acc = carry
        start = pl.multiple_of(c * NSA_KC, NSA_KC)
        kpos = start + lax.broadcasted_iota(jnp.int32, (NSA_KC, 1), 0)
        bias = bsel_ref[pl.ds(start, NSA_KC), :] + jnp.where(kpos <= tq, 0.0, NEG_INF)
        s = _dot_nt(ksa_ref[pl.ds(start, NSA_KC), :], qaug) + tile4(bias)
        m_new = jnp.maximum(m, jnp.max(s, axis=0, keepdims=True))
        alpha = jnp.exp(m - m_new)
        p = jnp.exp(s - m_new)
        l_new = alpha * l + jnp.sum(p, axis=0, keepdims=True)
        acc_new = alpha * acc + _dot(vst_ref[:, pl.ds(start, NSA_KC)], p.astype(BF16))
        return m_new, l_new, acc_new

    init = (jnp.full((1, Q_LANES), -1e29, F32), jnp.zeros((1, Q_LANES), F32), jnp.zeros((NSA_PAD, Q_LANES), F32))
    _, l_slc, acc_slc = lax.fori_loop(0, (t0 + BLK - 1) // NSA_KC + 1, slc_step, init)
    o_slc = acc_slc / jnp.maximum(l_slc, 1e-30)

    wspan = min(NSA_WSPAN, seq)
    wstart = pl.multiple_of(jnp.maximum(t0 + BLK - wspan, 0), BLK)
    wdist = tq - (wstart + lax.broadcasted_iota(jnp.int32, (wspan, 1), 0))
    wbias = jnp.where((wdist >= 0) & (wdist < NSA_WINDOW), 0.0, NEG_INF)
    p_win = _softmax_down(_dot_nt(kwa_ref[pl.ds(wstart, wspan), :], qaug) + tile4(wbias))
    o_win = _dot(vwt_ref[:, pl.ds(wstart, wspan)], p_win.astype(BF16))

    gts = jax.nn.sigmoid(gate_ref[...].astype(F32)).T
    outs = []
    for h in range(NSA_HEADS):
        cols = slice(h * BLK, (h + 1) * BLK)
        outs.append(gts[h:h + 1, :] * o_cmp[:HEAD_DIM, cols]
                    + gts[NSA_HEADS + h:NSA_HEADS + h + 1, :] * o_slc[:HEAD_DIM, cols]
                    + gts[2 * NSA_HEADS + h:2 * NSA_HEADS + h + 1, :] * o_win[:HEAD_DIM, cols])
    o_ref[...] = jnp.concatenate(outs, axis=0).T.astype(o_ref.dtype)


def _nsa(ua, kcmp, vcmp, bsz, seq):
    t = bsz * seq
    nq = seq // BLK
    ncmp = kcmp.shape[1]
    qsel = _nsa_query_select()
    staged_k = pltpu.VMEM((seq, NSA_PAD), BF16)
    staged_v = pltpu.VMEM((NSA_PAD, seq), BF16)
    return pl.pallas_call(
        _nsa_kernel,
        grid=(bsz, nq),
        in_specs=[pl.BlockSpec((BLK, 256), lambda bb, i: (bb * nq + i, C_NSAQ // 256)),
                  pl.BlockSpec(qsel.shape, lambda bb, i: (0, 0)),
                  pl.BlockSpec((1, ncmp, HEAD_DIM), lambda bb, i: (bb, 0, 0)),
                  pl.BlockSpec((1, ncmp, HEAD_DIM), lambda bb, i: (bb, 0, 0)),
                  pl.BlockSpec((seq, 128), lambda bb, i: (bb, C_NSAS // 128)),
                  pl.BlockSpec((seq, 128), lambda bb, i: (bb, C_NSAW // 128)),
                  pl.BlockSpec((BLK, 128), lambda bb, i: (bb * nq + i, C_NSAG // 128))],
        out_specs=pl.BlockSpec((BLK, 256), lambda bb, i: (bb * nq + i, 0)),
        out_shape=jax.ShapeDtypeStruct((t, NSA_HEADS * HEAD_DIM), BF16),
        scratch_shapes=[staged_k, staged_k, pltpu.VMEM((ncmp, NSA_PAD), BF16),
                        staged_v, staged_v, pltpu.VMEM((NSA_PAD, ncmp), BF16),
                        staged_k, pltpu.VMEM((seq, BLK), F32)],
        compiler_params=_params("parallel", "arbitrary"),
        name="nsa_attention",
    )(ua, qsel, kcmp, vcmp, ua, ua, ua)


MERGE_TM = 512


def _merge_kernel(x_ref, conv_ref, swa_ref, ret_ref, nsa_ref, wm_ref, pc_ref, ps_ref, pr_ref, pn_ref,
                  wo_ref, g_ref, b_ref, o_ref, op_ref):
    x = x_ref[...]
    xb = x.astype(BF16)
    merged = None
    for n, (a_ref, p_ref) in enumerate(((conv_ref, pc_ref), (swa_ref, ps_ref), (ret_ref, pr_ref), (nsa_ref, pn_ref))):
        gate = jax.nn.sigmoid(_dot(xb, wm_ref[:, n * D_MODEL:(n + 1) * D_MODEL]))
        term = gate * _dot(a_ref[...], p_ref[...])
        merged = term if merged is None else merged + term
    mix = _dot(merged.astype(BF16), wo_ref[...])
    y = _layer_norm(ALPHA * x + mix, g_ref[...], b_ref[...])
    o_ref[...] = y
    op_ref[...] = y.astype(BF16)


def _merge(x2d, conv_a, swa_o, ret_o, nsa_o, w_m, p_conv, p_swa, p_ret, p_nsa, w_out, g, b):
    t = x2d.shape[0]
    tm = min(MERGE_TM, t)

    def rows(width):
        return pl.BlockSpec((tm, width), lambda i: (i, 0))

    def whole(arr):
        return pl.BlockSpec(arr.shape, lambda i: (0, 0), pipeline_mode=pl.Buffered(1))

    g2, b2 = g.reshape(1, -1), b.reshape(1, -1)
    return pl.pallas_call(
        _merge_kernel,
        grid=(t // tm,),
        in_specs=[rows(D_MODEL), rows(256), rows(256), rows(512), rows(256),
                  whole(w_m), whole(p_conv), whole(p_swa), whole(p_ret), whole(p_nsa), whole(w_out),
                  whole(g2), whole(b2)],
        out_specs=[rows(D_MODEL), rows(D_MODEL)],
        out_shape=[jax.ShapeDtypeStruct((t, D_MODEL), F32), jax.ShapeDtypeStruct((t, D_MODEL), BF16)],
        compiler_params=_params("parallel"),
        name="merge_outproj_ln",
    )(x2d, conv_a, swa_o, ret_o, nsa_o, w_m, p_conv, p_swa, p_ret, p_nsa, w_out, g2, b2)


MOE_TT = 256
RUN_ALIGN = 16
MOE_LR = 2560
MOE_CHUNK = 256
RUN_PIECES = (256, 128, 64, 32, 16)
ROW_W = D_MODEL + 128
assert MOE_LR >= TOP_K * MOE_TT + N_EXPERTS * (RUN_ALIGN - 1) and MOE_LR % MOE_CHUNK == 0


def _first_argmax(v, iota, size):
    m = jnp.max(v, axis=0, keepdims=True)
    idx = jnp.min(jnp.where(v == m, iota, size), axis=0, keepdims=True)
    return m, idx


def _router_kernel(x_ref, w_ref, b_ref, lp_ref, lpt_ref, gate_ref, tab_ref, cnt_ref, carry_ref):
    i = pl.program_id(0)
    tr = x_ref.shape[0]

    @pl.when(i == 0)
    def _():
        carry_ref[...] = jnp.zeros_like(carry_ref)

    x = x_ref[...]
    w = w_ref[...]
    xh = x.astype(BF16)
    xl = (x - xh.astype(F32)).astype(BF16)
    wh = w.astype(BF16)
    wl = (w - wh.astype(F32)).astype(BF16)
    logits = _dot_nt(wh, xh) + _dot_nt(wh, xl) + _dot_nt(wl, xh)
    scores = jax.nn.sigmoid(logits)
    biased = scores + b_ref[...]

    iota_g = lax.broadcasted_iota(jnp.int32, (GROUP_SIZE, tr), 0)
    grp = []
    for g in range(N_GROUPS):
        vg = biased[g * GROUP_SIZE:(g + 1) * GROUP_SIZE, :]
        m1, i1 = _first_argmax(vg, iota_g, GROUP_SIZE)
        m2 = jnp.max(jnp.where(iota_g == i1, -jnp.inf, vg), axis=0, keepdims=True)
        grp.append(m1 + m2)
    work = jnp.concatenate(grp, axis=0)
    iota_n = lax.broadcasted_iota(jnp.int32, (N_GROUPS, tr), 0)
    gsel = jnp.zeros((N_GROUPS, tr), F32)
    for _ in range(TOPK_GROUPS):
        _, gi = _first_argmax(work, iota_n, N_GROUPS)
        hit = iota_n == gi
        gsel = jnp.where(hit, 1.0, gsel)
        work = jnp.where(hit, -jnp.inf, work)
    emask = jnp.concatenate([jnp.broadcast_to(gsel[g:g + 1, :], (GROUP_SIZE, tr)) for g in range(N_GROUPS)], axis=0)
    work = jnp.where(emask > 0.5, biased, NEG_INF)

    iota_e = lax.broadcasted_iota(jnp.int32, (N_EXPERTS, tr), 0)
    hits, gates = [], []
    chosen = jnp.zeros((N_EXPERTS, tr), F32)
    for _ in range(TOP_K):
        _, ei = _first_argmax(work, iota_e, N_EXPERTS)
        hit = iota_e == ei
        hits.append(hit)
        gates.append(jnp.sum(jnp.where(hit, scores, 0.0), axis=0, keepdims=True))
        chosen = jnp.where(hit, 1.0, chosen)
        work = jnp.where(hit, -jnp.inf, work)
    gsum = gates[0]
    for gk in gates[1:]:
        gsum = gsum + gk
    gate_ref[...] = jnp.concatenate([gk / gsum * ROUTE_SCALE for gk in gates] + [jnp.zeros((8 - TOP_K, tr), F32)], axis=0)

    r = lax.broadcasted_iota(jnp.int32, (tr, tr), 0)
    c = lax.broadcasted_iota(jnp.int32, (tr, tr), 1)
    rank = _dot(chosen.astype(BF16), jnp.where(r < c, 1.0, 0.0).astype(BF16))
    run_len = jnp.ceil(jnp.sum(chosen, axis=1, keepdims=True) * (1.0 / RUN_ALIGN)) * RUN_ALIGN
    er = lax.broadcasted_iota(jnp.int32, (N_EXPERTS, N_EXPERTS), 0)
    ec = lax.broadcasted_iota(jnp.int32, (N_EXPERTS, N_EXPERTS), 1)
    units = jnp.broadcast_to(run_len * (1.0 / RUN_ALIGN), (N_EXPERTS, 128)).astype(BF16)
    run_off = _dot(jnp.where(ec < er, 1.0, 0.0).astype(BF16), units)[:, :1] * RUN_ALIGN
    local = [jnp.sum(jnp.where(h, run_off + rank, 0.0), axis=0, keepdims=True) for h in hits]
    lp = jnp.concatenate(local + [jnp.zeros((8 - TOP_K, tr), F32)], axis=0)
    lp_ref[...] = lp.astype(jnp.int32)

    lp128 = jnp.concatenate([lp, jnp.zeros((128 - 8, tr), F32)], axis=0)
    hi = jnp.floor(lp128 * (1.0 / 64.0))
    lo = lp128 - 64.0 * hi
    eye = jnp.where(r == c, 1.0, 0.0).astype(BF16)
    lpt_ref[...] = 64.0 * _dot_nt(eye, hi.astype(BF16)) + _dot_nt(eye, lo.astype(BF16))

    def as_row(col):
        row = jnp.sum(jnp.where(er == ec, jnp.broadcast_to(col, (N_EXPERTS, N_EXPERTS)), 0.0), axis=0, keepdims=True)
        return jnp.concatenate([row, jnp.zeros((1, 128 - N_EXPERTS), F32)], axis=1)

    total = jnp.sum(run_len, axis=0, keepdims=True)
    n_large = jnp.sum(jnp.where(run_len >= float(LARGE_PIECES[0]), 1.0, 0.0), axis=0, keepdims=True)
    lane_t = lax.broadcasted_iota(jnp.int32, (1, 128), 1)
    total_row = jnp.where(lane_t == 0, total, jnp.where(lane_t == 1, n_large, 0.0))
    tab = jnp.concatenate([as_row(run_len), as_row(run_off), as_row(carry_ref[...]), total_row,
                           jnp.zeros((4, 128), F32)], axis=0)
    tab_ref[...] = tab.astype(jnp.int32)
    carry_ref[...] = carry_ref[...] + run_len
    cnt_ref[...] = jnp.broadcast_to(carry_ref[...], cnt_ref.shape)


def _router(x1, router_w, router_b):
    t = x1.shape[0]
    tr = MOE_TT
    nt = t // tr
    col = pl.BlockSpec((8, tr), lambda i: (0, i))
    return pl.pallas_call(
        _router_kernel,
        grid=(nt,),
        in_specs=[pl.BlockSpec((tr, D_MODEL), lambda i: (i, 0)),
                  pl.BlockSpec((N_EXPERTS, D_MODEL), lambda i: (0, 0)),
                  pl.BlockSpec((N_EXPERTS, 1), lambda i: (0, 0))],
        out_specs=[col, pl.BlockSpec((tr, 128), lambda i: (i, 0)), col,
                   pl.BlockSpec((None, 8, 128), lambda i: (i, 0, 0)),
                   pl.BlockSpec((N_EXPERTS, 128), lambda i: (0, 0))],
        out_shape=[jax.ShapeDtypeStruct((8, t), jnp.int32), jax.ShapeDtypeStruct((t, 128), F32),
                   jax.ShapeDtypeStruct((8, t), F32), jax.ShapeDtypeStruct((nt, 8, 128), jnp.int32),
                   jax.ShapeDtypeStruct((N_EXPERTS, 128), F32)],
        scratch_shapes=[pltpu.VMEM((N_EXPERTS, 1), F32)],
        compiler_params=_params("arbitrary"),
        name="moe_router",
    )(x1, router_w, router_b.reshape(-1, 1))


SMALL_PIECES = (16, 32)
LARGE_PIECES = (64, 128, 256)
WAIT_PIECES = (2048, 1024, 512, 256, 128, 64, 32, 16)
assert sum(WAIT_PIECES) >= MOE_LR and sum(SMALL_PIECES + LARGE_PIECES) >= MOE_TT


def _start_run_copies(pstart_ref, tab_ref, make_copy):
    def start_pieces(e, sizes, skip_mask):
        length = tab_ref[0, e]
        local = tab_ref[1, e]
        glob = pstart_ref[e] + tab_ref[2, e]
        done = length & skip_mask
        for size in sizes:
            piece = length & size

            @pl.when(piece != 0)
            def _():
                make_copy(pl.multiple_of(local + done, RUN_ALIGN), pl.multiple_of(glob + done, RUN_ALIGN),
                          size).start()

            done = done + piece

    def small(e, carry):
        start_pieces(e, SMALL_PIECES, 0)
        return carry

    def large(e, carry):
        start_pieces(e, LARGE_PIECES, sum(SMALL_PIECES))
        return carry

    lax.fori_loop(0, N_EXPERTS, small, 0)

    @pl.when(tab_ref[3, 1] > 0)
    def _():
        lax.fori_loop(0, N_EXPERTS, large, 0)


def _wait_run_copies(tab_ref, make_copy):
    total = tab_ref[3, 0]
    for size in WAIT_PIECES:
        @pl.when((total & size) != 0)
        def _():
            make_copy(0, 0, size).wait()


def _dispatch_kernel(pstart_ref, tail_ref, taillen_ref, nused_ref, tab_ref, lp_ref, gate_ref, x_ref, o_hbm,
                     buf_ref, sem):
    i = pl.program_id(0)

    def make_copy(local, glob, size):
        return pltpu.make_async_copy(buf_ref.at[pl.ds(local, size)], o_hbm.at[pl.ds(glob, size)], sem)

    @pl.when(i == 0)
    def _():
        buf_ref[0:MOE_BLK, :] = jnp.zeros((MOE_BLK, ROW_W), BF16)
        for action in ("start", "wait"):
            def per_expert(e, carry, action=action):
                done = 0
                for size in RUN_PIECES:
                    piece = taillen_ref[e] & size

                    @pl.when(piece != 0)
                    def _():
                        cp = make_copy(0, pl.multiple_of(tail_ref[e] + done, RUN_ALIGN), size)
                        cp.start() if action == "start" else cp.wait()

                    done = done + piece
                return carry
            lax.fori_loop(0, N_EXPERTS, per_expert, 0)

            def per_block(r, carry, action=action):
                cp = make_copy(0, pl.multiple_of(r * MOE_BLK, MOE_BLK), MOE_BLK)
                cp.start() if action == "start" else cp.wait()
                return carry
            lax.fori_loop(nused_ref[0], o_hbm.shape[0] // MOE_BLK, per_block, 0)

    lp = lp_ref[...]
    gates = gate_ref[...]
    xb = x_ref[...]
    glane = lax.broadcasted_iota(jnp.int32, (1, ROW_W - D_MODEL), 1)
    for c in range(MOE_LR // MOE_CHUNK):
        rows = c * MOE_CHUNK + lax.broadcasted_iota(jnp.int32, (MOE_CHUNK, MOE_TT), 0)
        member = rows == lp[0:1, :]
        weight = jnp.where(member, gates[0:1, :], 0.0)
        for k in range(1, TOP_K):
            hit = rows == lp[k:k + 1, :]
            member = member | hit
            weight = jnp.where(hit, gates[k:k + 1, :], weight)
        picked = _dot(jnp.where(member, 1.0, 0.0).astype(BF16), xb)
        row_gate = jnp.sum(weight, axis=1, keepdims=True)
        sl = slice(c * MOE_CHUNK, (c + 1) * MOE_CHUNK)
        buf_ref[sl, :D_MODEL] = picked.astype(BF16)
        g_hi = row_gate.astype(BF16).astype(F32)
        buf_ref[sl, D_MODEL:] = jnp.where(glane == 0, g_hi, jnp.where(glane == 1, row_gate - g_hi, 0.0)).astype(BF16)

    _start_run_copies(pstart_ref, tab_ref, make_copy)
    _wait_run_copies(tab_ref, make_copy)


def _dispatch(pstart, tail_start, tail_len, n_used, tab, lp, gates, x1b, rows):
    t = x1b.shape[0]
    tt = MOE_TT
    grid_spec = pltpu.PrefetchScalarGridSpec(
        num_scalar_prefetch=4,
        grid=(t // tt,),
        in_specs=[pl.BlockSpec((None, 8, 128), lambda i, *_: (i, 0, 0), memory_space=pltpu.SMEM),
                  pl.BlockSpec((8, tt), lambda i, *_: (0, i)),
                  pl.BlockSpec((8, tt), lambda i, *_: (0, i)),
                  pl.BlockSpec((tt, D_MODEL), lambda i, *_: (i, 0))],
        out_specs=pl.BlockSpec(memory_space=pl.ANY),
        scratch_shapes=[pltpu.VMEM((MOE_LR, ROW_W), BF16), pltpu.SemaphoreType.DMA],
    )
    return pl.pallas_call(
        _dispatch_kernel,
        grid_spec=grid_spec,
        out_shape=jax.ShapeDtypeStruct((rows, ROW_W), BF16),
        compiler_params=_params("arbitrary"),
        name="moe_dispatch",
    )(pstart, tail_start, tail_len, n_used, tab, lp, gates, x1b)


def _expert_kernel(be_ref, nu_ref, x_ref, wg_ref, wu_ref, wd_ref, o_ref, wgb_ref, wub_ref, wdb_ref):
    r = pl.program_id(0)

    @pl.when((r == 0) | (be_ref[r] != be_ref[jnp.maximum(r - 1, 0)]))
    def _():
        wgb_ref[...] = wg_ref[...].astype(BF16)
        wub_ref[...] = wu_ref[...].astype(BF16)
        wdb_ref[...] = wd_ref[...].astype(BF16)

    @pl.when(r < nu_ref[0])
    def _():
        xb = x_ref[:, :D_MODEL]
        gate_terms = x_ref[:, D_MODEL:D_MODEL + 2].astype(F32)
        row_gate = gate_terms[:, 0:1] + gate_terms[:, 1:2]
        h = _silu(_dot(xb, wgb_ref[...])) * _dot(xb, wub_ref[...])
        o_ref[...] = (_dot(h.astype(BF16), wdb_ref[...]) * row_gate).astype(o_ref.dtype)

    @pl.when(r >= nu_ref[0])
    def _():
        o_ref[...] = jnp.zeros_like(o_ref)


def _experts(blk_exp, n_used, xs, wg, wu, wd, layer):
    rows = xs.shape[0]
    nblk = rows // MOE_BLK
    grid_spec = pltpu.PrefetchScalarGridSpec(
        num_scalar_prefetch=2,
        grid=(nblk,),
        in_specs=[pl.BlockSpec((MOE_BLK, ROW_W), lambda r, be, nu: (jnp.minimum(r, jnp.maximum(nu[0] - 1, 0)), 0)),
                  pl.BlockSpec((None, None, D_MODEL, D_EXPERT), lambda r, be, nu: (layer, be[r], 0, 0)),
                  pl.BlockSpec((None, None, D_MODEL, D_EXPERT), lambda r, be, nu: (layer, be[r], 0, 0)),
                  pl.BlockSpec((None, None, D_EXPERT, D_MODEL), lambda r, be, nu: (layer, be[r], 0, 0))],
        out_specs=pl.BlockSpec((MOE_BLK, D_MODEL), lambda r, be, nu: (r, 0)),
        scratch_shapes=[pltpu.VMEM((D_MODEL, D_EXPERT), BF16), pltpu.VMEM((D_MODEL, D_EXPERT), BF16),
                        pltpu.VMEM((D_EXPERT, D_MODEL), BF16)],
    )
    return pl.pallas_call(
        _expert_kernel,
        grid_spec=grid_spec,
        out_shape=jax.ShapeDtypeStruct((rows, D_MODEL), BF16),
        compiler_params=_params("arbitrary"),
        name="moe_experts",
    )(blk_exp, n_used, xs, wg, wu, wd)


def _combine_kernel(pstart_ref, tab_ref, lpt_ref, x_ref, y_hbm, swg_ref, swu_ref, swd_ref, g_ref, b_ref, o_ref,
                    buf_ref, sem):
    i = pl.program_id(0)

    @pl.when(i == 0)
    def _():
        buf_ref[...] = jnp.zeros_like(buf_ref)

    def make_copy(local, glob, size):
        return pltpu.make_async_copy(y_hbm.at[pl.ds(glob, size)], buf_ref.at[pl.ds(local, size)], sem)

    _start_run_copies(pstart_ref, tab_ref, make_copy)

    x = x_ref[...]
    xb = x.astype(BF16)
    h = _silu(_dot(xb, swg_ref[...])) * _dot(xb, swu_ref[...])
    acc = ALPHA * x + _dot(h.astype(BF16), swd_ref[...])

    _wait_run_copies(tab_ref, make_copy)

    lpt = lpt_ref[...].astype(jnp.int32)
    for c in range(MOE_LR // MOE_CHUNK):
        cols = c * MOE_CHUNK + lax.broadcasted_iota(jnp.int32, (MOE_TT, MOE_CHUNK), 1)
        member = cols == lpt[:, 0:1]
        for k in range(1, TOP_K):
            member = member | (cols == lpt[:, k:k + 1])
        sel = jnp.where(member, 1.0, 0.0).astype(BF16)
        acc = acc + _dot(sel, buf_ref[c * MOE_CHUNK:(c + 1) * MOE_CHUNK, :])
    o_ref[...] = _layer_norm(acc, g_ref[...], b_ref[...])


def _combine(pstart, tab, lpt, x1, y_rows, swg, swu, swd, g, b):
    t = x1.shape[0]
    tt = MOE_TT

    def whole(arr):
        return pl.BlockSpec(arr.shape, lambda i, ps: (0, 0))

    g2, b2 = g.reshape(1, -1), b.reshape(1, -1)
    grid_spec = pltpu.PrefetchScalarGridSpec(
        num_scalar_prefetch=1,
        grid=(t // tt,),
        in_specs=[pl.BlockSpec((None, 8, 128), lambda i, ps: (i, 0, 0), memory_space=pltpu.SMEM),
                  pl.BlockSpec((tt, 128), lambda i, ps: (i, 0)),
                  pl.BlockSpec((tt, D_MODEL), lambda i, ps: (i, 0)),
                  pl.BlockSpec(memory_space=pl.ANY),
                  whole(swg), whole(swu), whole(swd), whole(g2), whole(b2)],
        out_specs=pl.BlockSpec((tt, D_MODEL), lambda i, ps: (i, 0)),
        scratch_shapes=[pltpu.VMEM((MOE_LR, D_MODEL), BF16), pltpu.SemaphoreType.DMA],
    )
    return pl.pallas_call(
        _combine_kernel,
        grid_spec=grid_spec,
        out_shape=jax.ShapeDtypeStruct((t, D_MODEL), F32),
        compiler_params=_params("arbitrary"),
        name="moe_combine_shared_ln",
    )(pstart, tab, lpt, x1, y_rows, swg, swu, swd, g2, b2)


def _moe(x1, x1b, layer, router_w, router_b, wg, wu, wd, swg, swu, swd, g, b):
    t = x1.shape[0]
    lp, lpt, gates, tab, cnt = _router(x1, router_w, router_b)
    rows_e = cnt[:, 0].astype(jnp.int32)
    padded = (rows_e + MOE_BLK - 1) // MOE_BLK * MOE_BLK
    pend = jnp.cumsum(padded)
    pstart = pend - padded
    max_rows = t * TOP_K + (t // MOE_TT) * N_EXPERTS * (RUN_ALIGN - 1) + N_EXPERTS * (MOE_BLK - 1)
    n_blocks = -(-max_rows // MOE_BLK)
    blk_first = jnp.arange(n_blocks, dtype=jnp.int32) * MOE_BLK
    blk_exp = jnp.minimum(jnp.sum((pend[None, :] <= blk_first[:, None]).astype(jnp.int32), axis=1), N_EXPERTS - 1)
    n_used = (pend[-1:] // MOE_BLK).astype(jnp.int32)
    xs = _dispatch(pstart, pstart + rows_e, padded - rows_e, n_used, tab, lp, gates, x1b, n_blocks * MOE_BLK)
    y_rows = _experts(blk_exp, n_used, xs, wg, wu, wd, layer)
    return _combine(pstart, tab, lpt, x1, y_rows, swg, swu, swd, g, b)


def _layer(x2d, bsz, seq, layer, w_in, moe_wg, moe_wu, moe_wd,
           conv_dw, conv_db, conv_ln_g, conv_ln_b, swa_sink, ret_gn_g, ret_gn_b,
           nsa_pe_k, nsa_w1_k, nsa_w2_k, nsa_pe_v, nsa_w1_v, nsa_w2_v, p_conv, p_swa, p_ret, p_nsa, w_out,
           ln1_g, ln1_b, router_w, router_b, sh_wg, sh_wu, sh_wd, ln2_g, ln2_b):
    bf = lambda a: a.astype(BF16)
    w_a = _mix_weights(w_in, layer)
    w_m = _merge_gate_weights(w_in, layer)
    ua, kcvc = _inproj(x2d, w_a)
    conv_a = _conv(ua, conv_dw, conv_db, conv_ln_g, conv_ln_b, bsz, seq)
    swa_o = _swa(ua, swa_sink, bsz, seq)
    ret_o = _retention(ua, ret_gn_g, ret_gn_b, bsz, seq)
    kcmp, vcmp = _compress(kcvc, nsa_pe_k, bf(nsa_w1_k), bf(nsa_w2_k), nsa_pe_v, bf(nsa_w1_v), bf(nsa_w2_v), bsz, seq)
    nsa_o = _nsa(ua, kcmp, vcmp, bsz, seq)
    x1, x1b = _merge(x2d, conv_a, swa_o, ret_o, nsa_o, w_m, bf(p_conv), bf(p_swa), bf(p_ret), bf(p_nsa),
                     bf(w_out), ln1_g, ln1_b)
    return _moe(x1, x1b, layer, router_w, router_b, moe_wg, moe_wu, moe_wd,
                bf(sh_wg), bf(sh_wu), bf(sh_wd), ln2_g, ln2_b)


def kernel(x, w_in, conv_dw, conv_db, conv_ln_g, conv_ln_b, swa_sink, ret_gn_g, ret_gn_b, nsa_pe_k, nsa_w1_k, nsa_w2_k, nsa_pe_v, nsa_w1_v, nsa_w2_v, p_conv, p_swa, p_ret, p_nsa, w_out, ln1_g, ln1_b, router_w, router_b, moe_wg, moe_wu, moe_wd, sh_wg, sh_wu, sh_wd, ln2_g, ln2_b):
    bsz, seq, d = x.shape
    params = (conv_dw, conv_db, conv_ln_g, conv_ln_b, swa_sink, ret_gn_g, ret_gn_b,
              nsa_pe_k, nsa_w1_k, nsa_w2_k, nsa_pe_v, nsa_w1_v, nsa_w2_v, p_conv, p_swa, p_ret, p_nsa, w_out,
              ln1_g, ln1_b, router_w, router_b, sh_wg, sh_wu, sh_wd, ln2_g, ln2_b)
    x2d = x.reshape(bsz * seq, d)
    for l in range(w_in.shape[0]):
        x2d = _layer(x2d, bsz, seq, l, w_in, moe_wg, moe_wu, moe_wd, *[p[l] for p in params])
    return x2d.reshape(bsz, seq, d)
```

```python
import functools
import math

import jax
import jax.numpy as jnp
import numpy as np
from jax import lax
from jax.experimental import pallas as pl
from jax.experimental.pallas import tpu as pltpu

F32 = jnp.float32
BF16 = jnp.bfloat16

D_MODEL = 1024
DEPTH = 2
HEAD_DIM = 64
BLK = 128
CONV_CH = 256
CONV_K = 31
SWA_HEADS = 4
SWA_KV_HEADS = 2
SWA_WINDOW = 128
RET_HEADS = 4
RET_QK = 64
RET_V = 128
NSA_HEADS = 4
NSA_CMP_LEN = 32
NSA_CMP_STRIDE = 16
NSA_CMP_HIDDEN = 256
NSA_SLC_LEN = 64
NSA_TOPN = 8
NSA_WINDOW = 512
NSA_FORCE = 1e4
N_EXPERTS = 64
TOP_K = 6
N_GROUPS = 8
GROUP_SIZE = N_EXPERTS // N_GROUPS
TOPK_GROUPS = 4
D_EXPERT = 256
ROUTE_SCALE = 2.5
MOE_BLK = 512
N_BRANCH = 4
ALPHA = (2 * DEPTH) ** 0.25
LN_EPS = 1e-5
NEG_INF = -1e30
SCALE = HEAD_DIM ** -0.5

C_CONV = 0
C_SWA = 512
C_RETQK = 1024
C_RETV = 1536
C_RETG = 2048
C_NSAQ = 2560
C_NSAC = 2816
C_NSAS = 2944
C_NSAW = 3072
C_NSAG = 3200
N_MIX_COLS = 3212
UA = 3328

ALIBI = [2.0 ** (-8.0 * (i + 1) / 8) for i in range(8)]
SLOPES_SWA = ALIBI[:4]
SLOPES_NSA = ALIBI[4:]
RET_LOG_GAMMA = [math.log1p(-(2.0 ** (-5.0 - h))) for h in range(RET_HEADS)]

VMEM_LIMIT = 48 * 1024 * 1024
NT_DIMS = (((1,), (1,)), ((), ()))
TN_DIMS = (((0,), (0,)), ((), ()))


def _params(*sem):
    return pltpu.CompilerParams(dimension_semantics=sem, vmem_limit_bytes=VMEM_LIMIT)


def _dot(a, b):
    return jnp.dot(a, b, preferred_element_type=F32)


def _dot_nt(a, b):
    return lax.dot_general(a, b, NT_DIMS, preferred_element_type=F32)


def _layer_norm(v, g, b):
    mu = jnp.mean(v, axis=-1, keepdims=True)
    c = v - mu
    var = jnp.mean(c * c, axis=-1, keepdims=True)
    return c * lax.rsqrt(var + LN_EPS) * g + b


def _silu(v):
    return v * jax.nn.sigmoid(v)


CAST_TM = 256
LANES = 128


def _cast_kernel(x_ref, o_ref):
    o_ref[...] = x_ref[...].astype(o_ref.dtype)


def _mix_weights(w_in, layer):
    return pl.pallas_call(
        _cast_kernel,
        grid=(D_MODEL // CAST_TM,),
        in_specs=[pl.BlockSpec((None, CAST_TM, UA), lambda i: (layer, i, 0))],
        out_specs=pl.BlockSpec((CAST_TM, UA), lambda i: (i, 0)),
        out_shape=jax.ShapeDtypeStruct((D_MODEL, UA), BF16),
        compiler_params=_params("parallel"),
        name="cast_mix_weights",
    )(w_in)


def _shifted_cast_kernel(a_ref, b_ref, o_ref):
    shift = N_MIX_COLS % LANES
    both = jnp.concatenate([a_ref[...], b_ref[...]], axis=1)
    o_ref[...] = both[:, shift:shift + LANES].astype(o_ref.dtype)


def _merge_gate_weights(w_in, layer):
    first = N_MIX_COLS // LANES
    width = N_BRANCH * D_MODEL
    return pl.pallas_call(
        _shifted_cast_kernel,
        grid=(width // LANES,),
        in_specs=[pl.BlockSpec((None, D_MODEL, LANES), lambda j: (layer, 0, first + j)),
                  pl.BlockSpec((None, D_MODEL, LANES), lambda j: (layer, 0, first + j + 1))],
        out_specs=pl.BlockSpec((D_MODEL, LANES), lambda j: (0, j)),
        out_shape=jax.ShapeDtypeStruct((D_MODEL, width), BF16),
        compiler_params=_params("parallel"),
        name="cast_merge_gate_weights",
    )(w_in, w_in)


INPROJ_TM = 512


def _inproj_kernel(x_ref, w_ref, o_ref, oc_ref):
    xb = x_ref[...].astype(BF16)
    for c in range(0, UA, 256):
        o_ref[:, c:c + 256] = _dot(xb, w_ref[:, c:c + 256]).astype(o_ref.dtype)
    oc_ref[...] = _dot(xb, w_ref[:, C_NSAC:C_NSAC + 128])


def _inproj(x2d, w_a):
    t = x2d.shape[0]
    return pl.pallas_call(
        _inproj_kernel,
        grid=(t // INPROJ_TM,),
        in_specs=[pl.BlockSpec((INPROJ_TM, D_MODEL), lambda i: (i, 0)),
                  pl.BlockSpec((D_MODEL, UA), lambda i: (0, 0))],
        out_specs=[pl.BlockSpec((INPROJ_TM, UA), lambda i: (i, 0)),
                   pl.BlockSpec((INPROJ_TM, 128), lambda i: (i, 0))],
        out_shape=[jax.ShapeDtypeStruct((t, UA), BF16), jax.ShapeDtypeStruct((t, 128), F32)],
        compiler_params=_params("parallel"),
        name="inproj",
    )(x2d, w_a)


SEQS_PER_STEP = 2


def _seqs_per_step(bsz):
    return SEQS_PER_STEP if bsz % SEQS_PER_STEP == 0 else 1


CONV_TS = 256
CONV_HALO = 32


def _conv_kernel(cur_ref, halo_ref, dw_ref, db_ref, g_ref, b_ref, o_ref, hc_ref):
    i = pl.program_id(1)

    def glu(u):
        u = u.astype(F32)
        return u[:, :CONV_CH] * jax.nn.sigmoid(u[:, CONV_CH:])

    for s in range(cur_ref.shape[0]):
        hc_ref[s, 0:CONV_HALO, :] = jnp.where(i > 0, glu(halo_ref[s]), 0.0)
        hc_ref[s, CONV_HALO:CONV_HALO + CONV_TS, :] = glu(cur_ref[s])
        acc = jnp.broadcast_to(db_ref[...], (CONV_TS, CONV_CH))
        off = CONV_HALO - (CONV_K - 1)
        for j in range(CONV_K):
            acc = acc + hc_ref[s, off + j:off + j + CONV_TS, :] * dw_ref[j:j + 1, :]
        o_ref[s] = _silu(_layer_norm(acc, g_ref[...], b_ref[...])).astype(o_ref.dtype)


def _conv(ua, dw, db, g, b, bsz, seq):
    ns = seq // CONV_TS
    per = CONV_TS // CONV_HALO
    nb = _seqs_per_step(bsz)
    u3 = ua.reshape(bsz, seq, UA)
    dwp = jnp.concatenate([dw, jnp.zeros((32 - CONV_K, CONV_CH), F32)], axis=0)
    vec = pl.BlockSpec((1, CONV_CH), lambda bb, i: (0, 0))
    out = pl.pallas_call(
        _conv_kernel,
        grid=(bsz // nb, ns),
        in_specs=[pl.BlockSpec((nb, CONV_TS, 512), lambda bb, i: (bb, i, C_CONV // 512)),
                  pl.BlockSpec((nb, CONV_HALO, 512), lambda bb, i: (bb, jnp.maximum(i * per - 1, 0), C_CONV // 512)),
                  pl.BlockSpec((32, CONV_CH), lambda bb, i: (0, 0)), vec, vec, vec],
        out_specs=pl.BlockSpec((nb, CONV_TS, CONV_CH), lambda bb, i: (bb, i, 0)),
        out_shape=jax.ShapeDtypeStruct((bsz, seq, CONV_CH), BF16),
        scratch_shapes=[pltpu.VMEM((nb, CONV_HALO + CONV_TS, CONV_CH), F32)],
        compiler_params=_params("parallel", "parallel"),
        name="conformer_conv",
    )(u3, u3, dwp, db.reshape(1, -1), g.reshape(1, -1), b.reshape(1, -1))
    return out.reshape(bsz * seq, CONV_CH)


def _swa_kernel(sink_ref, q_ref, kvc_ref, kvp_ref, o_ref):
    i = pl.program_id(1)
    a = lax.broadcasted_iota(jnp.int32, (BLK, 2 * BLK), 0)
    c = lax.broadcasted_iota(jnp.int32, (BLK, 2 * BLK), 1)
    dist = BLK + a - c
    mask = (dist >= 0) & (dist < SWA_WINDOW) & ((c >= BLK) | (i > 0))
    distf = dist.astype(F32)
    rep = SWA_HEADS // SWA_KV_HEADS
    for s_idx in range(q_ref.shape[0]):
        q = q_ref[s_idx]
        kvc = kvc_ref[s_idx]
        kvp = kvp_ref[s_idx]
        for h in range(SWA_HEADS):
            g = h // rep
            k = jnp.concatenate([kvp[:, g * HEAD_DIM:(g + 1) * HEAD_DIM],
                                 kvc[:, g * HEAD_DIM:(g + 1) * HEAD_DIM]], axis=0)
            v = jnp.concatenate([kvp[:, BLK + g * HEAD_DIM:BLK + (g + 1) * HEAD_DIM],
                                 kvc[:, BLK + g * HEAD_DIM:BLK + (g + 1) * HEAD_DIM]], axis=0)
            s = _dot_nt(q[:, h * HEAD_DIM:(h + 1) * HEAD_DIM], k) * SCALE - SLOPES_SWA[h] * distf
            s = jnp.where(mask, s, NEG_INF)
            sink = sink_ref[h]
            m = jnp.maximum(jnp.max(s, axis=-1, keepdims=True), sink)
            e = jnp.where(mask, jnp.exp(s - m), 0.0)
            den = jnp.sum(e, axis=-1, keepdims=True) + jnp.exp(sink - m)
            p = e / jnp.maximum(den, 1e-30)
            o_ref[s_idx, :, h * HEAD_DIM:(h + 1) * HEAD_DIM] = _dot(p.astype(BF16), v).astype(o_ref.dtype)


def _swa(ua, sink, bsz, seq):
    nblk = seq // BLK
    nb = _seqs_per_step(bsz)
    u3 = ua.reshape(bsz, seq, UA)
    out = pl.pallas_call(
        _swa_kernel,
        grid=(bsz // nb, nblk),
        in_specs=[pl.BlockSpec(memory_space=pltpu.SMEM),
                  pl.BlockSpec((nb, BLK, 256), lambda bb, i: (bb, i, C_SWA // 256)),
                  pl.BlockSpec((nb, BLK, 256), lambda bb, i: (bb, i, C_SWA // 256 + 1)),
                  pl.BlockSpec((nb, BLK, 256), lambda bb, i: (bb, jnp.maximum(i - 1, 0), C_SWA // 256 + 1))],
        out_specs=pl.BlockSpec((nb, BLK, 256), lambda bb, i: (bb, i, 0)),
        out_shape=jax.ShapeDtypeStruct((bsz, seq, SWA_HEADS * HEAD_DIM), BF16),
        compiler_params=_params("parallel", "parallel"),
        name="swa",
    )(sink, u3, u3, u3)
    return out.reshape(bsz * seq, SWA_HEADS * HEAD_DIM)


def _ret_kernel(qk_ref, v_ref, g_ref, gng_ref, gnb_ref, o_ref, state_ref):
    n = pl.program_id(1)

    @pl.when(n == 0)
    def _():
        state_ref[...] = jnp.zeros_like(state_ref)

    a = lax.broadcasted_iota(jnp.int32, (BLK, BLK), 0)
    c = lax.broadcasted_iota(jnp.int32, (BLK, BLK), 1)
    diff = (a - c).astype(F32)
    idx = lax.broadcasted_iota(jnp.int32, (BLK, 1), 0).astype(F32)
    hq = RET_HEADS * RET_QK
    for s in range(qk_ref.shape[0]):
        qk = qk_ref[s]
        vv = v_ref[s]
        gate = g_ref[s].astype(F32)
        for h in range(RET_HEADS):
            lg = RET_LOG_GAMMA[h]
            q = qk[:, h * RET_QK:(h + 1) * RET_QK]
            k = qk[:, hq + h * RET_QK:hq + (h + 1) * RET_QK].astype(F32) * (RET_QK ** -0.5)
            v = vv[:, h * RET_V:(h + 1) * RET_V]
            dmat = jnp.where(diff >= 0, jnp.exp(lg * jnp.maximum(diff, 0.0)), 0.0)
            inner = _dot_nt(q, k.astype(BF16)) * dmat
            o_inner = _dot(inner.astype(BF16), v)
            zeta = jnp.exp(lg * (BLK - 1 - idx))
            xi = jnp.exp(lg * (idx + 1.0))
            kv = lax.dot_general((k * zeta).astype(BF16), v, TN_DIMS, preferred_element_type=F32)
            prev = state_ref[s * RET_HEADS + h]
            o = o_inner + _dot(q, prev.astype(BF16)) * xi
            state_ref[s * RET_HEADS + h] = math.exp(lg * BLK) * prev + kv
            mu = jnp.mean(o, axis=-1, keepdims=True)
            cen = o - mu
            var = jnp.mean(cen * cen, axis=-1, keepdims=True)
            sl = slice(h * RET_V, (h + 1) * RET_V)
            on = cen * lax.rsqrt(var + LN_EPS) * gng_ref[:, sl] + gnb_ref[:, sl]
            o_ref[s, :, sl] = (_silu(gate[:, sl]) * on).astype(o_ref.dtype)


def _retention(ua, gn_g, gn_b, bsz, seq):
    nc = seq // BLK
    nb = _seqs_per_step(bsz)
    u3 = ua.reshape(bsz, seq, UA)
    vec = pl.BlockSpec((1, RET_HEADS * RET_V), lambda bb, i: (0, 0))
    out = pl.pallas_call(
        _ret_kernel,
        grid=(bsz // nb, nc),
        in_specs=[pl.BlockSpec((nb, BLK, 512), lambda bb, i: (bb, i, C_RETQK // 512)),
                  pl.BlockSpec((nb, BLK, 512), lambda bb, i: (bb, i, C_RETV // 512)),
                  pl.BlockSpec((nb, BLK, 512), lambda bb, i: (bb, i, C_RETG // 512)),
                  vec, vec],
        out_specs=pl.BlockSpec((nb, BLK, 512), lambda bb, i: (bb, i, 0)),
        out_shape=jax.ShapeDtypeStruct((bsz, seq, RET_HEADS * RET_V), BF16),
        scratch_shapes=[pltpu.VMEM((nb * RET_HEADS, RET_QK, RET_V), F32)],
        compiler_params=_params("parallel", "arbitrary"),
        name="retention",
    )(u3, u3, u3, gn_g.reshape(1, -1), gn_b.reshape(1, -1))
    return out.reshape(bsz * seq, RET_HEADS * RET_V)


def _compress_kernel(t_ref, pek_ref, w1k_ref, w2k_ref, pev_ref, w1v_ref, w2v_ref, ok_ref, ov_ref):
    st = NSA_CMP_STRIDE
    ng = t_ref.shape[0] // st
    branches = ((pek_ref, w1k_ref, w2k_ref, ok_ref), (pev_ref, w1v_ref, w2v_ref, ov_ref))
    first = [jnp.zeros((ng, NSA_CMP_HIDDEN), F32) for _ in branches]
    second = [jnp.zeros((ng, NSA_CMP_HIDDEN), F32) for _ in branches]
    for r in range(st):
        rows = t_ref[pl.ds(r, ng, stride=st), :]
        for n, (pe_ref, w1_ref, _, _) in enumerate(branches):
            v = rows[:, n * HEAD_DIM:(n + 1) * HEAD_DIM]
            lo = (v + pe_ref[r:r + 1, :]).astype(BF16)
            hi = (v + pe_ref[st + r:st + r + 1, :]).astype(BF16)
            first[n] = first[n] + _dot(lo, w1_ref[r * HEAD_DIM:(r + 1) * HEAD_DIM, :])
            second[n] = second[n] + _dot(hi, w1_ref[(st + r) * HEAD_DIM:(st + r + 1) * HEAD_DIM, :])
    for n, (_, _, w2_ref, o_ref) in enumerate(branches):
        hid = first[n] + pltpu.roll(second[n], ng - 1, axis=0)
        act = 0.5 * hid * (1.0 + lax.erf(hid * (2.0 ** -0.5)))
        o_ref[0] = _dot(act.astype(BF16), w2_ref[...]).astype(o_ref.dtype)


def _compress(kcvc, pe_k, w1_k, w2_k, pe_v, w1_v, w2_v, bsz, seq):
    ng = seq // NSA_CMP_STRIDE
    pes = pl.BlockSpec((NSA_CMP_LEN, HEAD_DIM), lambda bb: (0, 0))
    w1s = pl.BlockSpec((NSA_CMP_LEN * HEAD_DIM, NSA_CMP_HIDDEN), lambda bb: (0, 0))
    w2s = pl.BlockSpec((NSA_CMP_HIDDEN, HEAD_DIM), lambda bb: (0, 0))
    osp = pl.BlockSpec((1, ng, HEAD_DIM), lambda bb: (bb, 0, 0))
    return pl.pallas_call(
        _compress_kernel,
        grid=(bsz,),
        in_specs=[pl.BlockSpec((seq, 128), lambda bb: (bb, 0)), pes, w1s, w2s, pes, w1s, w2s],
        out_specs=[osp, osp],
        out_shape=[jax.ShapeDtypeStruct((bsz, ng, HEAD_DIM), BF16)] * 2,
        compiler_params=_params("parallel"),
        name="nsa_compress",
    )(kcvc, pe_k, w1_k, w2_k, pe_v, w1_v, w2_v)


NSA_KC = 1024
NSA_WSPAN = NSA_WINDOW + BLK


NSA_PAD = 128
Q_LANES = NSA_HEADS * BLK


def _nsa_query_select():
    sel = np.zeros((NSA_HEADS * HEAD_DIM, NSA_HEADS * NSA_PAD), np.float32)
    for h in range(NSA_HEADS):
        for dd in range(HEAD_DIM):
            sel[h * HEAD_DIM + dd, h * NSA_PAD + dd] = SCALE
    return jnp.asarray(sel, BF16)


def _softmax_down(s):
    m = jnp.maximum(jnp.max(s, axis=0, keepdims=True), -1e29)
    e = jnp.exp(s - m)
    return e / jnp.maximum(jnp.sum(e, axis=0, keepdims=True), 1e-30)


def _nsa_kernel(q_ref, qsel_ref, kcmp_ref, vcmp_ref, ksvs_ref, kwvw_ref, gate_ref, o_ref,
                ksa_ref, kwa_ref, kca_ref, vst_ref, vwt_ref, vct_ref, ext_ref, bsel_ref):
    i = pl.program_id(1)
    seq = ksvs_ref.shape[0]
    ncmp = kcmp_ref.shape[1]
    nslc = seq // NSA_SLC_LEN
    lane = lax.broadcasted_iota(jnp.int32, (1, NSA_PAD), 1)

    @pl.when(i == 0)
    def _():
        def aug(k, pos):
            kf = jnp.concatenate([k.astype(F32), jnp.zeros((k.shape[0], NSA_PAD - HEAD_DIM), F32)], axis=1)
            hi = (pos // 128).astype(F32)
            lo = (pos % 128).astype(F32)
            return (kf + jnp.where(lane == HEAD_DIM, hi, jnp.where(lane == HEAD_DIM + 1, lo, 0.0))).astype(BF16)

        def value_t(v):
            vf = jnp.concatenate([v.astype(F32), jnp.zeros((v.shape[0], NSA_PAD - HEAD_DIM), F32)], axis=1)
            return vf.T.astype(BF16)

        kpos = lax.broadcasted_iota(jnp.int32, (seq, 1), 0)
        ksvs = ksvs_ref[...]
        kwvw = kwvw_ref[...]
        ksa_ref[...] = aug(ksvs[:, :HEAD_DIM], kpos)
        kwa_ref[...] = aug(kwvw[:, :HEAD_DIM], kpos)
        cend = lax.broadcasted_iota(jnp.int32, (ncmp, 1), 0) * NSA_CMP_STRIDE + (NSA_CMP_LEN - 1)
        kca_ref[...] = aug(kcmp_ref[0], cend)
        vst_ref[...] = value_t(ksvs[:, HEAD_DIM:])
        vwt_ref[...] = value_t(kwvw[:, HEAD_DIM:])
        vct_ref[...] = value_t(vcmp_ref[0])
        ext_ref[...] = jnp.where(kpos // NSA_SLC_LEN == lax.broadcasted_iota(jnp.int32, (seq, NSA_PAD), 1),
                                 1.0, 0.0).astype(BF16)

    t0 = i * BLK
    tq = t0 + lax.broadcasted_iota(jnp.int32, (1, BLK), 1)
    qw = _dot(q_ref[...], qsel_ref[...])
    qaug = []
    for h in range(NSA_HEADS):
        coef = jnp.where(lane == HEAD_DIM, 128.0 * SLOPES_NSA[h], jnp.where(lane == HEAD_DIM + 1, SLOPES_NSA[h], 0.0))
        qaug.append((qw[:, h * NSA_PAD:(h + 1) * NSA_PAD] + coef).astype(BF16))
    qaug = jnp.concatenate(qaug, axis=0)

    def tile4(b):
        return jnp.concatenate([b] * NSA_HEADS, axis=1)

    cend = lax.broadcasted_iota(jnp.int32, (ncmp, 1), 0) * NSA_CMP_STRIDE + (NSA_CMP_LEN - 1)
    p_cmp = _softmax_down(_dot_nt(kca_ref[...], qaug) + tile4(jnp.where(cend <= tq, 0.0, NEG_INF)))
    o_cmp = _dot(vct_ref[...], p_cmp.astype(BF16))
    psum = p_cmp[:, 0:BLK]
    for h in range(1, NSA_HEADS):
        psum = psum + p_cmp[:, h * BLK:(h + 1) * BLK]

    sst = lax.broadcasted_iota(jnp.int32, (nslc, ncmp), 0) * NSA_SLC_LEN
    cst = lax.broadcasted_iota(jnp.int32, (nslc, ncmp), 1) * NSA_CMP_STRIDE
    overlap = jnp.where((cst < sst + NSA_SLC_LEN) & (cst + NSA_CMP_LEN > sst), 1.0, 0.0).astype(BF16)
    p_hi = psum.astype(BF16)
    p_lo = (psum - p_hi.astype(F32)).astype(BF16)
    imp = _dot(overlap, p_hi) + _dot(overlap, p_lo)
    jj = lax.broadcasted_iota(jnp.int32, (nslc, BLK), 0)
    cur = tq // NSA_SLC_LEN
    forced = (jj == 0) | (jj == cur) | (jj == cur - 1)
    imp = jnp.where(forced, NSA_FORCE, jnp.where(jj > cur, -1.0, imp))

    sel = jnp.zeros((nslc, BLK), F32)
    for _ in range(min(NSA_TOPN, nslc)):
        mx = jnp.max(imp, axis=0, keepdims=True)
        first = jnp.min(jnp.where(imp == mx, jj, nslc), axis=0, keepdims=True)
        hit = jj == first
        sel = jnp.where(hit, 1.0, sel)
        imp = jnp.where(hit, -3e38, imp)
    sel = jnp.concatenate([sel, jnp.zeros((NSA_PAD - nslc, BLK), F32)], axis=0).astype(BF16)
    bsel_ref[...] = (_dot(ext_ref[...], sel) - 1.0) * 1e30

    def slc_step(c, carry):
        m, l, acc = carry
        start = pl.multiple_of(c * NSA_KC, NSA_KC)
        kpos = start + lax.broadcasted_iota(jnp.int32, (NSA_KC, 1), 0)
        bias = bsel_ref[pl.ds(start, NSA_KC), :] + jnp.where(kpos <= tq, 0.0, NEG_INF)
        s = _dot_nt(ksa_ref[pl.ds(start, NSA_KC), :], qaug) + tile4(bias)
        m_new = jnp.maximum(m, jnp.max(s, axis=0, keepdims=True))
        alpha = jnp.exp(m - m_new)
        p = jnp.exp(s - m_new)
        l_new = alpha * l + jnp.sum(p, axis=0, keepdims=True)
        acc_new = alpha * acc + _dot(vst_ref[:, pl.ds(start, NSA_KC)], p.astype(BF16))
        return m_new, l_new, acc_new

    init = (jnp.full((1, Q_LANES), -1e29, F32), jnp.zeros((1, Q_LANES), F32), jnp.zeros((NSA_PAD, Q_LANES), F32))
    _, l_slc, acc_slc = lax.fori_loop(0, (t0 + BLK - 1) // NSA_KC + 1, slc_step, init)
    o_slc = acc_slc / jnp.maximum(l_slc, 1e-30)

    wspan = min(NSA_WSPAN, seq)
    wstart = pl.multiple_of(jnp.maximum(t0 + BLK - wspan, 0), BLK)
    wdist = tq - (wstart + lax.broadcasted_iota(jnp.int32, (wspan, 1), 0))
    wbias = jnp.where((wdist >= 0) & (wdist < NSA_WINDOW), 0.0, NEG_INF)
    p_win = _softmax_down(_dot_nt(kwa_ref[pl.ds(wstart, wspan), :], qaug) + tile4(wbias))
    o_win = _dot(vwt_ref[:, pl.ds(wstart, wspan)], p_win.astype(BF16))

    gts = jax.nn.sigmoid(gate_ref[...].astype(F32)).T
    outs = []
    for h in range(NSA_HEADS):
        cols = slice(h * BLK, (h + 1) * BLK)
        outs.append(gts[h:h + 1, :] * o_cmp[:HEAD_DIM, cols]
                    + gts[NSA_HEADS + h:NSA_HEADS + h + 1, :] * o_slc[:HEAD_DIM, cols]
                    + gts[2 * NSA_HEADS + h:2 * NSA_HEADS + h + 1, :] * o_win[:HEAD_DIM, cols])
    o_ref[...] = jnp.concatenate(outs, axis=0).T.astype(o_ref.dtype)


def _nsa(ua, kcmp, vcmp, bsz, seq):
    t = bsz * seq
    nq = seq // BLK
    ncmp = kcmp.shape[1]
    qsel = _nsa_query_select()
    staged_k = pltpu.VMEM((seq, NSA_PAD), BF16)
    staged_v = pltpu.VMEM((NSA_PAD, seq), BF16)
    return pl.pallas_call(
        _nsa_kernel,
        grid=(bsz, nq),
        in_specs=[pl.BlockSpec((BLK, 256), lambda bb, i: (bb * nq + i, C_NSAQ // 256)),
                  pl.BlockSpec(qsel.shape, lambda bb, i: (0, 0)),
                  pl.BlockSpec((1, ncmp, HEAD_DIM), lambda bb, i: (bb, 0, 0)),
                  pl.BlockSpec((1, ncmp, HEAD_DIM), lambda bb, i: (bb, 0, 0)),
                  pl.BlockSpec((seq, 128), lambda bb, i: (bb, C_NSAS // 128)),
                  pl.BlockSpec((seq, 128), lambda bb, i: (bb, C_NSAW // 128)),
                  pl.BlockSpec((BLK, 128), lambda bb, i: (bb * nq + i, C_NSAG // 128))],
        out_specs=pl.BlockSpec((BLK, 256), lambda bb, i: (bb * nq + i, 0)),
        out_shape=jax.ShapeDtypeStruct((t, NSA_HEADS * HEAD_DIM), BF16),
        scratch_shapes=[staged_k, staged_k, pltpu.VMEM((ncmp, NSA_PAD), BF16),
                        staged_v, staged_v, pltpu.VMEM((NSA_PAD, ncmp), BF16),
                        staged_k, pltpu.VMEM((seq, BLK), F32)],
        compiler_params=_params("parallel", "arbitrary"),
        name="nsa_attention",
    )(ua, qsel, kcmp, vcmp, ua, ua, ua)


MERGE_TM = 512


def _merge_kernel(x_ref, conv_ref, swa_ref, ret_ref, nsa_ref, wm_ref, pc_ref, ps_ref, pr_ref, pn_ref,
                  wo_ref, g_ref, b_ref, o_ref, op_ref):
    x = x_ref[...]
    xb = x.astype(BF16)
    merged = None
    for n, (a_ref, p_ref) in enumerate(((conv_ref, pc_ref), (swa_ref, ps_ref), (ret_ref, pr_ref), (nsa_ref, pn_ref))):
        gate = jax.nn.sigmoid(_dot(xb, wm_ref[:, n * D_MODEL:(n + 1) * D_MODEL]))
        term = gate * _dot(a_ref[...], p_ref[...])
        merged = term if merged is None else merged + term
    mix = _dot(merged.astype(BF16), wo_ref[...])
    y = _layer_norm(ALPHA * x + mix, g_ref[...], b_ref[...])
    o_ref[...] = y
    op_ref[...] = y.astype(BF16)


def _merge(x2d, conv_a, swa_o, ret_o, nsa_o, w_m, p_conv, p_swa, p_ret, p_nsa, w_out, g, b):
    t = x2d.shape[0]
    tm = min(MERGE_TM, t)

    def rows(width):
        return pl.BlockSpec((tm, width), lambda i: (i, 0))

    def whole(arr):
        return pl.BlockSpec(arr.shape, lambda i: (0, 0), pipeline_mode=pl.Buffered(1))

    g2, b2 = g.reshape(1, -1), b.reshape(1, -1)
    return pl.pallas_call(
        _merge_kernel,
        grid=(t // tm,),
        in_specs=[rows(D_MODEL), rows(256), rows(256), rows(512), rows(256),
                  whole(w_m), whole(p_conv), whole(p_swa), whole(p_ret), whole(p_nsa), whole(w_out),
                  whole(g2), whole(b2)],
        out_specs=[rows(D_MODEL), rows(D_MODEL)],
        out_shape=[jax.ShapeDtypeStruct((t, D_MODEL), F32), jax.ShapeDtypeStruct((t, D_MODEL), BF16)],
        compiler_params=_params("parallel"),
        name="merge_outproj_ln",
    )(x2d, conv_a, swa_o, ret_o, nsa_o, w_m, p_conv, p_swa, p_ret, p_nsa, w_out, g2, b2)


MOE_TT = 256
RUN_ALIGN = 16
MOE_LR = 2560
MOE_CHUNK = 256
RUN_PIECES = (256, 128, 64, 32, 16)
ROW_W = D_MODEL + 128
assert MOE_LR >= TOP_K * MOE_TT + N_EXPERTS * (RUN_ALIGN - 1) and MOE_LR % MOE_CHUNK == 0


def _first_argmax(v, iota, size):
    m = jnp.max(v, axis=0, keepdims=True)
    idx = jnp.min(jnp.where(v == m, iota, size), axis=0, keepdims=True)
    return m, idx


def _router_kernel(x_ref, w_ref, b_ref, lp_ref, lpt_ref, gate_ref, tab_ref, cnt_ref, carry_ref):
    i = pl.program_id(0)
    tr = x_ref.shape[0]

    @pl.when(i == 0)
    def _():
        carry_ref[...] = jnp.zeros_like(carry_ref)

    x = x_ref[...]
    w = w_ref[...]
    xh = x.astype(BF16)
    xl = (x - xh.astype(F32)).astype(BF16)
    wh = w.astype(BF16)
    wl = (w - wh.astype(F32)).astype(BF16)
    logits = _dot_nt(wh, xh) + _dot_nt(wh, xl) + _dot_nt(wl, xh)
    scores = jax.nn.sigmoid(logits)
    biased = scores + b_ref[...]

    iota_g = lax.broadcasted_iota(jnp.int32, (GROUP_SIZE, tr), 0)
    grp = []
    for g in range(N_GROUPS):
        vg = biased[g * GROUP_SIZE:(g + 1) * GROUP_SIZE, :]
        m1, i1 = _first_argmax(vg, iota_g, GROUP_SIZE)
        m2 = jnp.max(jnp.where(iota_g == i1, -jnp.inf, vg), axis=0, keepdims=True)
        grp.append(m1 + m2)
    work = jnp.concatenate(grp, axis=0)
    iota_n = lax.broadcasted_iota(jnp.int32, (N_GROUPS, tr), 0)
    gsel = jnp.zeros((N_GROUPS, tr), F32)
    for _ in range(TOPK_GROUPS):
        _, gi = _first_argmax(work, iota_n, N_GROUPS)
        hit = iota_n == gi
        gsel = jnp.where(hit, 1.0, gsel)
        work = jnp.where(hit, -jnp.inf, work)
    emask = jnp.concatenate([jnp.broadcast_to(gsel[g:g + 1, :], (GROUP_SIZE, tr)) for g in range(N_GROUPS)], axis=0)
    work = jnp.where(emask > 0.5, biased, NEG_INF)

    iota_e = lax.broadcasted_iota(jnp.int32, (N_EXPERTS, tr), 0)
    hits, gates = [], []
    chosen = jnp.zeros((N_EXPERTS, tr), F32)
    for _ in range(TOP_K):
        _, ei = _first_argmax(work, iota_e, N_EXPERTS)
        hit = iota_e == ei
        hits.append(hit)
        gates.append(jnp.sum(jnp.where(hit, scores, 0.0), axis=0, keepdims=True))
        chosen = jnp.where(hit, 1.0, chosen)
        work = jnp.where(hit, -jnp.inf, work)
    gsum = gates[0]
    for gk in gates[1:]:
        gsum = gsum + gk
    gate_ref[...] = jnp.concatenate([gk / gsum * ROUTE_SCALE for gk in gates] + [jnp.zeros((8 - TOP_K, tr), F32)], axis=0)

    r = lax.broadcasted_iota(jnp.int32, (tr, tr), 0)
    c = lax.broadcasted_iota(jnp.int32, (tr, tr), 1)
    rank = _dot(chosen.astype(BF16), jnp.where(r < c, 1.0, 0.0).astype(BF16))
    run_len = jnp.ceil(jnp.sum(chosen, axis=1, keepdims=True) * (1.0 / RUN_ALIGN)) * RUN_ALIGN
    er = lax.broadcasted_iota(jnp.int32, (N_EXPERTS, N_EXPERTS), 0)
    ec = lax.broadcasted_iota(jnp.int32, (N_EXPERTS, N_EXPERTS), 1)
    units = jnp.broadcast_to(run_len * (1.0 / RUN_ALIGN), (N_EXPERTS, 128)).astype(BF16)
    run_off = _dot(jnp.where(ec < er, 1.0, 0.0).astype(BF16), units)[:, :1] * RUN_ALIGN
    local = [jnp.sum(jnp.where(h, run_off + rank, 0.0), axis=0, keepdims=True) for h in hits]
    lp = jnp.concatenate(local + [jnp.zeros((8 - TOP_K, tr), F32)], axis=0)
    lp_ref[...] = lp.astype(jnp.int32)

    lp128 = jnp.concatenate([lp, jnp.zeros((128 - 8, tr), F32)], axis=0)
    hi = jnp.floor(lp128 * (1.0 / 64.0))
    lo = lp128 - 64.0 * hi
    eye = jnp.where(r == c, 1.0, 0.0).astype(BF16)
    lpt_ref[...] = 64.0 * _dot_nt(eye, hi.astype(BF16)) + _dot_nt(eye, lo.astype(BF16))

    def as_row(col):
        row = jnp.sum(jnp.where(er == ec, jnp.broadcast_to(col, (N_EXPERTS, N_EXPERTS)), 0.0), axis=0, keepdims=True)
        return jnp.concatenate([row, jnp.zeros((1, 128 - N_EXPERTS), F32)], axis=1)

    total = jnp.sum(run_len, axis=0, keepdims=True)
    n_large = jnp.sum(jnp.where(run_len >= float(LARGE_PIECES[0]), 1.0, 0.0), axis=0, keepdims=True)
    lane_t = lax.broadcasted_iota(jnp.int32, (1, 128), 1)
    total_row = jnp.where(lane_t == 0, total, jnp.where(lane_t == 1, n_large, 0.0))
    tab = jnp.concatenate([as_row(run_len), as_row(run_off), as_row(carry_ref[...]), total_row,
                           jnp.zeros((4, 128), F32)], axis=0)
    tab_ref[...] = tab.astype(jnp.int32)
    carry_ref[...] = carry_ref[...] + run_len
    cnt_ref[...] = jnp.broadcast_to(carry_ref[...], cnt_ref.shape)


def _router(x1, router_w, router_b):
    t = x1.shape[0]
    tr = MOE_TT
    nt = t // tr
    col = pl.BlockSpec((8, tr), lambda i: (0, i))
    return pl.pallas_call(
        _router_kernel,
        grid=(nt,),
        in_specs=[pl.BlockSpec((tr, D_MODEL), lambda i: (i, 0)),
                  pl.BlockSpec((N_EXPERTS, D_MODEL), lambda i: (0, 0)),
                  pl.BlockSpec((N_EXPERTS, 1), lambda i: (0, 0))],
        out_specs=[col, pl.BlockSpec((tr, 128), lambda i: (i, 0)), col,
                   pl.BlockSpec((None, 8, 128), lambda i: (i, 0, 0)),
                   pl.BlockSpec((N_EXPERTS, 128), lambda i: (0, 0))],
        out_shape=[jax.ShapeDtypeStruct((8, t), jnp.int32), jax.ShapeDtypeStruct((t, 128), F32),
                   jax.ShapeDtypeStruct((8, t), F32), jax.ShapeDtypeStruct((nt, 8, 128), jnp.int32),
                   jax.ShapeDtypeStruct((N_EXPERTS, 128), F32)],
        scratch_shapes=[pltpu.VMEM((N_EXPERTS, 1), F32)],
        compiler_params=_params("arbitrary"),
        name="moe_router",
    )(x1, router_w, router_b.reshape(-1, 1))


SMALL_PIECES = (16, 32)
LARGE_PIECES = (64, 128, 256)
WAIT_PIECES = (2048, 1024, 512, 256, 128, 64, 32, 16)
assert sum(WAIT_PIECES) >= MOE_LR and sum(SMALL_PIECES + LARGE_PIECES) >= MOE_TT


def _start_run_copies(pstart_ref, tab_ref, make_copy):
    def start_pieces(e, sizes, skip_mask):
        length = tab_ref[0, e]
        local = tab_ref[1, e]
        glob = pstart_ref[e] + tab_ref[2, e]
        done = length & skip_mask
        for size in sizes:
            piece = length & size

            @pl.when(piece != 0)
            def _():
                make_copy(pl.multiple_of(local + done, RUN_ALIGN), pl.multiple_of(glob + done, RUN_ALIGN),
                          size).start()

            done = done + piece

    def small(e, carry):
        start_pieces(e, SMALL_PIECES, 0)
        return carry

    def large(e, carry):
        start_pieces(e, LARGE_PIECES, sum(SMALL_PIECES))
        return carry

    lax.fori_loop(0, N_EXPERTS, small, 0)

    @pl.when(tab_ref[3, 1] > 0)
    def _():
        lax.fori_loop(0, N_EXPERTS, large, 0)


def _wait_run_copies(tab_ref, make_copy):
    total = tab_ref[3, 0]
    for size in WAIT_PIECES:
        @pl.when((total & size) != 0)
        def _():
            make_copy(0, 0, size).wait()


def _dispatch_kernel(pstart_ref, tail_ref, taillen_ref, nused_ref, tab_ref, lp_ref, gate_ref, x_ref, o_hbm,
                     buf_ref, sem):
    i = pl.program_id(0)

    def make_copy(local, glob, size):
        return pltpu.make_async_copy(buf_ref.at[pl.ds(local, size)], o_hbm.at[pl.ds(glob, size)], sem)

    @pl.when(i == 0)
    def _():
        buf_ref[0:MOE_BLK, :] = jnp.zeros((MOE_BLK, ROW_W), BF16)
        for action in ("start", "wait"):
            def per_expert(e, carry, action=action):
                done = 0
                for size in RUN_PIECES:
                    piece = taillen_ref[e] & size

                    @pl.when(piece != 0)
                    def _():
                        cp = make_copy(0, pl.multiple_of(tail_ref[e] + done, RUN_ALIGN), size)
                        cp.start() if action == "start" else cp.wait()

                    done = done + piece
                return carry
            lax.fori_loop(0, N_EXPERTS, per_expert, 0)

            def per_block(r, carry, action=action):
                cp = make_copy(0, pl.multiple_of(r * MOE_BLK, MOE_BLK), MOE_BLK)
                cp.start() if action == "start" else cp.wait()
                return carry
            lax.fori_loop(nused_ref[0], o_hbm.shape[0] // MOE_BLK, per_block, 0)

    lp = lp_ref[...]
    gates = gate_ref[...]
    xb = x_ref[...]
    glane = lax.broadcasted_iota(jnp.int32, (1, ROW_W - D_MODEL), 1)
    for c in range(MOE_LR // MOE_CHUNK):
        rows = c * MOE_CHUNK + lax.broadcasted_iota(jnp.int32, (MOE_CHUNK, MOE_TT), 0)
        member = rows == lp[0:1, :]
        weight = jnp.where(member, gates[0:1, :], 0.0)
        for k in range(1, TOP_K):
            hit = rows == lp[k:k + 1, :]
            member = member | hit
            weight = jnp.where(hit, gates[k:k + 1, :], weight)
        picked = _dot(jnp.where(member, 1.0, 0.0).astype(BF16), xb)
        row_gate = jnp.sum(weight, axis=1, keepdims=True)
        sl = slice(c * MOE_CHUNK, (c + 1) * MOE_CHUNK)
        buf_ref[sl, :D_MODEL] = picked.astype(BF16)
        g_hi = row_gate.astype(BF16).astype(F32)
        buf_ref[sl, D_MODEL:] = jnp.where(glane == 0, g_hi, jnp.where(glane == 1, row_gate - g_hi, 0.0)).astype(BF16)

    _start_run_copies(pstart_ref, tab_ref, make_copy)
    _wait_run_copies(tab_ref, make_copy)


def _dispatch(pstart, tail_start, tail_len, n_used, tab, lp, gates, x1b, rows):
    t = x1b.shape[0]
    tt = MOE_TT
    grid_spec = pltpu.PrefetchScalarGridSpec(
        num_scalar_prefetch=4,
        grid=(t // tt,),
        in_specs=[pl.BlockSpec((None, 8, 128), lambda i, *_: (i, 0, 0), memory_space=pltpu.SMEM),
                  pl.BlockSpec((8, tt), lambda i, *_: (0, i)),
                  pl.BlockSpec((8, tt), lambda i, *_: (0, i)),
                  pl.BlockSpec((tt, D_MODEL), lambda i, *_: (i, 0))],
        out_specs=pl.BlockSpec(memory_space=pl.ANY),
        scratch_shapes=[pltpu.VMEM((MOE_LR, ROW_W), BF16), pltpu.SemaphoreType.DMA],
    )
    return pl.pallas_call(
        _dispatch_kernel,
        grid_spec=grid_spec,
        out_shape=jax.ShapeDtypeStruct((rows, ROW_W), BF16),
        compiler_params=_params("arbitrary"),
        name="moe_dispatch",
    )(pstart, tail_start, tail_len, n_used, tab, lp, gates, x1b)


def _expert_kernel(be_ref, nu_ref, x_ref, wg_ref, wu_ref, wd_ref, o_ref, wgb_ref, wub_ref, wdb_ref):
    r = pl.program_id(0)

    @pl.when((r == 0) | (be_ref[r] != be_ref[jnp.maximum(r - 1, 0)]))
    def _():
        wgb_ref[...] = wg_ref[...].astype(BF16)
        wub_ref[...] = wu_ref[...].astype(BF16)
        wdb_ref[...] = wd_ref[...].astype(BF16)

    @pl.when(r < nu_ref[0])
    def _():
        xb = x_ref[:, :D_MODEL]
        gate_terms = x_ref[:, D_MODEL:D_MODEL + 2].astype(F32)
        row_gate = gate_terms[:, 0:1] + gate_terms[:, 1:2]
        h = _silu(_dot(xb, wgb_ref[...])) * _dot(xb, wub_ref[...])
        o_ref[...] = (_dot(h.astype(BF16), wdb_ref[...]) * row_gate).astype(o_ref.dtype)

    @pl.when(r >= nu_ref[0])
    def _():
        o_ref[...] = jnp.zeros_like(o_ref)


def _experts(blk_exp, n_used, xs, wg, wu, wd, layer):
    rows = xs.shape[0]
    nblk = rows // MOE_BLK
    grid_spec = pltpu.PrefetchScalarGridSpec(
        num_scalar_prefetch=2,
        grid=(nblk,),
        in_specs=[pl.BlockSpec((MOE_BLK, ROW_W), lambda r, be, nu: (jnp.minimum(r, jnp.maximum(nu[0] - 1, 0)), 0)),
                  pl.BlockSpec((None, None, D_MODEL, D_EXPERT), lambda r, be, nu: (layer, be[r], 0, 0)),
                  pl.BlockSpec((None, None, D_MODEL, D_EXPERT), lambda r, be, nu: (layer, be[r], 0, 0)),
                  pl.BlockSpec((None, None, D_EXPERT, D_MODEL), lambda r, be, nu: (layer, be[r], 0, 0))],
        out_specs=pl.BlockSpec((MOE_BLK, D_MODEL), lambda r, be, nu: (r, 0)),
        scratch_shapes=[pltpu.VMEM((D_MODEL, D_EXPERT), BF16), pltpu.VMEM((D_MODEL, D_EXPERT), BF16),
                        pltpu.VMEM((D_EXPERT, D_MODEL), BF16)],
    )
    return pl.pallas_call(
        _expert_kernel,
        grid_spec=grid_spec,
        out_shape=jax.ShapeDtypeStruct((rows, D_MODEL), BF16),
        compiler_params=_params("arbitrary"),
        name="moe_experts",
    )(blk_exp, n_used, xs, wg, wu, wd)


def _combine_kernel(pstart_ref, tab_ref, lpt_ref, x_ref, y_hbm, swg_ref, swu_ref, swd_ref, g_ref, b_ref, o_ref,
                    buf_ref, sem):
    i = pl.program_id(0)

    @pl.when(i == 0)
    def _():
        buf_ref[...] = jnp.zeros_like(buf_ref)

    def make_copy(local, glob, size):
        return pltpu.make_async_copy(y_hbm.at[pl.ds(glob, size)], buf_ref.at[pl.ds(local, size)], sem)

    _start_run_copies(pstart_ref, tab_ref, make_copy)

    x = x_ref[...]
    xb = x.astype(BF16)
    h = _silu(_dot(xb, swg_ref[...])) * _dot(xb, swu_ref[...])
    acc = ALPHA * x + _dot(h.astype(BF16), swd_ref[...])

    _wait_run_copies(tab_ref, make_copy)

    lpt = lpt_ref[...].astype(jnp.int32)
    for c in range(MOE_LR // MOE_CHUNK):
        cols = c * MOE_CHUNK + lax.broadcasted_iota(jnp.int32, (MOE_TT, MOE_CHUNK), 1)
        member = cols == lpt[:, 0:1]
        for k in range(1, TOP_K):
            member = member | (cols == lpt[:, k:k + 1])
        sel = jnp.where(member, 1.0, 0.0).astype(BF16)
        acc = acc + _dot(sel, buf_ref[c * MOE_CHUNK:(c + 1) * MOE_CHUNK, :])
    o_ref[...] = _layer_norm(acc, g_ref[...], b_ref[...])


def _combine(pstart, tab, lpt, x1, y_rows, swg, swu, swd, g, b):
    t = x1.shape[0]
    tt = MOE_TT

    def whole(arr):
        return pl.BlockSpec(arr.shape, lambda i, ps: (0, 0))

    g2, b2 = g.reshape(1, -1), b.reshape(1, -1)
    grid_spec = pltpu.PrefetchScalarGridSpec(
        num_scalar_prefetch=1,
        grid=(t // tt,),
        in_specs=[pl.BlockSpec((None, 8, 128), lambda i, ps: (i, 0, 0), memory_space=pltpu.SMEM),
                  pl.BlockSpec((tt, 128), lambda i, ps: (i, 0)),
                  pl.BlockSpec((tt, D_MODEL), lambda i, ps: (i, 0)),
                  pl.BlockSpec(memory_space=pl.ANY),
                  whole(swg), whole(swu), whole(swd), whole(g2), whole(b2)],
        out_specs=pl.BlockSpec((tt, D_MODEL), lambda i, ps: (i, 0)),
        scratch_shapes=[pltpu.VMEM((MOE_LR, D_MODEL), BF16), pltpu.SemaphoreType.DMA],
    )
    return pl.pallas_call(
        _combine_kernel,
        grid_spec=grid_spec,
        out_shape=jax.ShapeDtypeStruct((t, D_MODEL), F32),
        compiler_params=_params("arbitrary"),
        name="moe_combine_shared_ln",
    )(pstart, tab, lpt, x1, y_rows, swg, swu, swd, g2, b2)


def _moe(x1, x1b, layer, router_w, router_b, wg, wu, wd, swg, swu, swd, g, b):
    t = x1.shape[0]
    lp, lpt, gates, tab, cnt = _router(x1, router_w, router_b)
    rows_e = cnt[:, 0].astype(jnp.int32)
    padded = (rows_e + MOE_BLK - 1) // MOE_BLK * MOE_BLK
    pend = jnp.cumsum(padded)
    pstart = pend - padded
    max_rows = t * TOP_K + (t // MOE_TT) * N_EXPERTS * (RUN_ALIGN - 1) + N_EXPERTS * (MOE_BLK - 1)
    n_blocks = -(-max_rows // MOE_BLK)
    blk_first = jnp.arange(n_blocks, dtype=jnp.int32) * MOE_BLK
    blk_exp = jnp.minimum(jnp.sum((pend[None, :] <= blk_first[:, None]).astype(jnp.int32), axis=1), N_EXPERTS - 1)
    n_used = (pend[-1:] // MOE_BLK).astype(jnp.int32)
    xs = _dispatch(pstart, pstart + rows_e, padded - rows_e, n_used, tab, lp, gates, x1b, n_blocks * MOE_BLK)
    y_rows = _experts(blk_exp, n_used, xs, wg, wu, wd, layer)
    return _combine(pstart, tab, lpt, x1, y_rows, swg, swu, swd, g, b)


def _layer(x2d, bsz, seq, layer, w_in, moe_wg, moe_wu, moe_wd,
           conv_dw, conv_db, conv_ln_g, conv_ln_b, swa_sink, ret_gn_g, ret_gn_b,
           nsa_pe_k, nsa_w1_k, nsa_w2_k, nsa_pe_v, nsa_w1_v, nsa_w2_v, p_conv, p_swa, p_ret, p_nsa, w_out,
           ln1_g, ln1_b, router_w, router_b, sh_wg, sh_wu, sh_wd, ln2_g, ln2_b):
    bf = lambda a: a.astype(BF16)
    w_a = _mix_weights(w_in, layer)
    w_m = _merge_gate_weights(w_in, layer)
    ua, kcvc = _inproj(x2d, w_a)
    conv_a = _conv(ua, conv_dw, conv_db, conv_ln_g, conv_ln_b, bsz, seq)
    swa_o = _swa(ua, swa_sink, bsz, seq)
    ret_o = _retention(ua, ret_gn_g, ret_gn_b, bsz, seq)
    kcmp, vcmp = _compress(kcvc, nsa_pe_k, bf(nsa_w1_k), bf(nsa_w2_k), nsa_pe_v, bf(nsa_w1_v), bf(nsa_w2_v), bsz, seq)
    nsa_o = _nsa(ua, kcmp, vcmp, bsz, seq)
    x1, x1b = _merge(x2d, conv_a, swa_o, ret_o, nsa_o, w_m, bf(p_conv), bf(p_swa), bf(p_ret), bf(p_nsa),
                     bf(w_out), ln1_g, ln1_b)
    return _moe(x1, x1b, layer, router_w, router_b, moe_wg, moe_wu, moe_wd,
                bf(sh_wg), bf(sh_wu), bf(sh_wd), ln2_g, ln2_b)


def kernel(x, w_in, conv_dw, conv_db, conv_ln_g, conv_ln_b, swa_sink, ret_gn_g, ret_gn_b, nsa_pe_k, nsa_w1_k, nsa_w2_k, nsa_pe_v, nsa_w1_v, nsa_w2_v, p_conv, p_swa, p_ret, p_nsa, w_out, ln1_g, ln1_b, router_w, router_b, moe_wg, moe_wu, moe_wd, sh_wg, sh_wu, sh_wd, ln2_g, ln2_b):
    bsz, seq, d = x.shape
    params = (conv_dw, conv_db, conv_ln_g, conv_ln_b, swa_sink, ret_gn_g, ret_gn_b,
              nsa_pe_k, nsa_w1_k, nsa_w2_k, nsa_pe_v, nsa_w1_v, nsa_w2_v, p_conv, p_swa, p_ret, p_nsa, w_out,
              ln1_g, ln1_b, router_w, router_b, sh_wg, sh_wu, sh_wd, ln2_g, ln2_b)
    x2d = x.reshape(bsz * seq, d)
    for l in range(w_in.shape[0]):
        x2d = _layer(x2d, bsz, seq, l, w_in, moe_wg, moe_wu, moe_wd, *[p[l] for p in params])
    return x2d.reshape(bsz, seq, d)
```

```python
import functools
import math

import jax
import jax.numpy as jnp
import numpy as np
from jax import lax
from jax.experimental import pallas as pl
from jax.experimental.pallas import tpu as pltpu

F32 = jnp.float32
BF16 = jnp.bfloat16

D_MODEL = 1024
DEPTH = 2
HEAD_DIM = 64
BLK = 128
CONV_CH = 256
CONV_K = 31
SWA_HEADS = 4
SWA_KV_HEADS = 2
SWA_WINDOW = 128
RET_HEADS = 4
RET_QK = 64
RET_V = 128
NSA_HEADS = 4
NSA_CMP_LEN = 32
NSA_CMP_STRIDE = 16
NSA_CMP_HIDDEN = 256
NSA_SLC_LEN = 64
NSA_TOPN = 8
NSA_WINDOW = 512
NSA_FORCE = 1e4
N_EXPERTS = 64
TOP_K = 6
N_GROUPS = 8
GROUP_SIZE = N_EXPERTS // N_GROUPS
TOPK_GROUPS = 4
D_EXPERT = 256
ROUTE_SCALE = 2.5
MOE_BLK = 512
N_BRANCH = 4
ALPHA = (2 * DEPTH) ** 0.25
LN_EPS = 1e-5
NEG_INF = -1e30
SCALE = HEAD_DIM ** -0.5

C_CONV = 0
C_SWA = 512
C_RETQK = 1024
C_RETV = 1536
C_RETG = 2048
C_NSAQ = 2560
C_NSAC = 2816
C_NSAS = 2944
C_NSAW = 3072
C_NSAG = 3200
N_MIX_COLS = 3212
UA = 3328

ALIBI = [2.0 ** (-8.0 * (i + 1) / 8) for i in range(8)]
SLOPES_SWA = ALIBI[:4]
SLOPES_NSA = ALIBI[4:]
RET_LOG_GAMMA = [math.log1p(-(2.0 ** (-5.0 - h))) for h in range(RET_HEADS)]

VMEM_LIMIT = 48 * 1024 * 1024
NT_DIMS = (((1,), (1,)), ((), ()))
TN_DIMS = (((0,), (0,)), ((), ()))


def _params(*sem):
    return pltpu.CompilerParams(dimension_semantics=sem, vmem_limit_bytes=VMEM_LIMIT)


def _dot(a, b):
    return jnp.dot(a, b, preferred_element_type=F32)


def _dot_nt(a, b):
    return lax.dot_general(a, b, NT_DIMS, preferred_element_type=F32)


def _layer_norm(v, g, b):
    mu = jnp.mean(v, axis=-1, keepdims=True)
    c = v - mu
    var = jnp.mean(c * c, axis=-1, keepdims=True)
    return c * lax.rsqrt(var + LN_EPS) * g + b


def _silu(v):
    return v * jax.nn.sigmoid(v)


CAST_TM = 256
LANES = 128


def _cast_kernel(x_ref, o_ref):
    o_ref[...] = x_ref[...].astype(o_ref.dtype)


def _mix_weights(w_in, layer):
    return pl.pallas_call(
        _cast_kernel,
        grid=(D_MODEL // CAST_TM,),
        in_specs=[pl.BlockSpec((None, CAST_TM, UA), lambda i: (layer, i, 0))],
        out_specs=pl.BlockSpec((CAST_TM, UA), lambda i: (i, 0)),
        out_shape=jax.ShapeDtypeStruct((D_MODEL, UA), BF16),
        compiler_params=_params("parallel"),
        name="cast_mix_weights",
    )(w_in)


def _shifted_cast_kernel(a_ref, b_ref, o_ref):
    shift = N_MIX_COLS % LANES
    both = jnp.concatenate([a_ref[...], b_ref[...]], axis=1)
    o_ref[...] = both[:, shift:shift + LANES].astype(o_ref.dtype)


def _merge_gate_weights(w_in, layer):
    first = N_MIX_COLS // LANES
    width = N_BRANCH * D_MODEL
    return pl.pallas_call(
        _shifted_cast_kernel,
        grid=(width // LANES,),
        in_specs=[pl.BlockSpec((None, D_MODEL, LANES), lambda j: (layer, 0, first + j)),
                  pl.BlockSpec((None, D_MODEL, LANES), lambda j: (layer, 0, first + j + 1))],
        out_specs=pl.BlockSpec((D_MODEL, LANES), lambda j: (0, j)),
        out_shape=jax.ShapeDtypeStruct((D_MODEL, width), BF16),
        compiler_params=_params("parallel"),
        name="cast_merge_gate_weights",
    )(w_in, w_in)


INPROJ_TM = 512


def _inproj_kernel(x_ref, w_ref, o_ref, oc_ref):
    xb = x_ref[...].astype(BF16)
    for c in range(0, UA, 256):
        o_ref[:, c:c + 256] = _dot(xb, w_ref[:, c:c + 256]).astype(o_ref.dtype)
    oc_ref[...] = _dot(xb, w_ref[:, C_NSAC:C_NSAC + 128])


def _inproj(x2d, w_a):
    t = x2d.shape[0]
    return pl.pallas_call(
        _inproj_kernel,
        grid=(t // INPROJ_TM,),
        in_specs=[pl.BlockSpec((INPROJ_TM, D_MODEL), lambda i: (i, 0)),
                  pl.BlockSpec((D_MODEL, UA), lambda i: (0, 0))],
        out_specs=[pl.BlockSpec((INPROJ_TM, UA), lambda i: (i, 0)),
                   pl.BlockSpec((INPROJ_TM, 128), lambda i: (i, 0))],
        out_shape=[jax.ShapeDtypeStruct((t, UA), BF16), jax.ShapeDtypeStruct((t, 128), F32)],
        compiler_params=_params("parallel"),
        name="inproj",
    )(x2d, w_a)


SEQS_PER_STEP = 2


def _seqs_per_step(bsz):
    return SEQS_PER_STEP if bsz % SEQS_PER_STEP == 0 else 1


CONV_TS = 256
CONV_HALO = 32


def _conv_kernel(cur_ref, halo_ref, dw_ref, db_ref, g_ref, b_ref, o_ref, hc_ref):
    i = pl.program_id(1)

    def glu(u):
        u = u.astype(F32)
        return u[:, :CONV_CH] * jax.nn.sigmoid(u[:, CONV_CH:])

    for s in range(cur_ref.shape[0]):
        hc_ref[s, 0:CONV_HALO, :] = jnp.where(i > 0, glu(halo_ref[s]), 0.0)
        hc_ref[s, CONV_HALO:CONV_HALO + CONV_TS, :] = glu(cur_ref[s])
        acc = jnp.broadcast_to(db_ref[...], (CONV_TS, CONV_CH))
        off = CONV_HALO - (CONV_K - 1)
        for j in range(CONV_K):
            acc = acc + hc_ref[s, off + j:off + j + CONV_TS, :] * dw_ref[j:j + 1, :]
        o_ref[s] = _silu(_layer_norm(acc, g_ref[...], b_ref[...])).astype(o_ref.dtype)


def _conv(ua, dw, db, g, b, bsz, seq):
    ns = seq // CONV_TS
    per = CONV_TS // CONV_HALO
    nb = _seqs_per_step(bsz)
    u3 = ua.reshape(bsz, seq, UA)
    dwp = jnp.concatenate([dw, jnp.zeros((32 - CONV_K, CONV_CH), F32)], axis=0)
    vec = pl.BlockSpec((1, CONV_CH), lambda bb, i: (0, 0))
    out = pl.pallas_call(
        _conv_kernel,
        grid=(bsz // nb, ns),
        in_specs=[pl.BlockSpec((nb, CONV_TS, 512), lambda bb, i: (bb, i, C_CONV // 512)),
                  pl.BlockSpec((nb, CONV_HALO, 512), lambda bb, i: (bb, jnp.maximum(i * per - 1, 0), C_CONV // 512)),
                  pl.BlockSpec((32, CONV_CH), lambda bb, i: (0, 0)), vec, vec, vec],
        out_specs=pl.BlockSpec((nb, CONV_TS, CONV_CH), lambda bb, i: (bb, i, 0)),
        out_shape=jax.ShapeDtypeStruct((bsz, seq, CONV_CH), BF16),
        scratch_shapes=[pltpu.VMEM((nb, CONV_HALO + CONV_TS, CONV_CH), F32)],
        compiler_params=_params("parallel", "parallel"),
        name="conformer_conv",
    )(u3, u3, dwp, db.reshape(1, -1), g.reshape(1, -1), b.reshape(1, -1))
    return out.reshape(bsz * seq, CONV_CH)


def _swa_kernel(sink_ref, q_ref, kvc_ref, kvp_ref, o_ref):
    i = pl.program_id(1)
    a = lax.broadcasted_iota(jnp.int32, (BLK, 2 * BLK), 0)
    c = lax.broadcasted_iota(jnp.int32, (BLK, 2 * BLK), 1)
    dist = BLK + a - c
    mask = (dist >= 0) & (dist < SWA_WINDOW) & ((c >= BLK) | (i > 0))
    distf = dist.astype(F32)
    rep = SWA_HEADS // SWA_KV_HEADS
    for s_idx in range(q_ref.shape[0]):
        q = q_ref[s_idx]
        kvc = kvc_ref[s_idx]
        kvp = kvp_ref[s_idx]
        for h in range(SWA_HEADS):
            g = h // rep
            k = jnp.concatenate([kvp[:, g * HEAD_DIM:(g + 1) * HEAD_DIM],
                                 kvc[:, g * HEAD_DIM:(g + 1) * HEAD_DIM]], axis=0)
            v = jnp.concatenate([kvp[:, BLK + g * HEAD_DIM:BLK + (g + 1) * HEAD_DIM],
                                 kvc[:, BLK + g * HEAD_DIM:BLK + (g + 1) * HEAD_DIM]], axis=0)
            s = _dot_nt(q[:, h * HEAD_DIM:(h + 1) * HEAD_DIM], k) * SCALE - SLOPES_SWA[h] * distf
            s = jnp.where(mask, s, NEG_INF)
            sink = sink_ref[h]
            m = jnp.maximum(jnp.max(s, axis=-1, keepdims=True), sink)
            e = jnp.where(mask, jnp.exp(s - m), 0.0)
            den = jnp.sum(e, axis=-1, keepdims=True) + jnp.exp(sink - m)
            p = e / jnp.maximum(den, 1e-30)
            o_ref[s_idx, :, h * HEAD_DIM:(h + 1) * HEAD_DIM] = _dot(p.astype(BF16), v).astype(o_ref.dtype)


def _swa(ua, sink, bsz, seq):
    nblk = seq // BLK
    nb = _seqs_per_step(bsz)
    u3 = ua.reshape(bsz, seq, UA)
    out = pl.pallas_call(
        _swa_kernel,
        grid=(bsz // nb, nblk),
        in_specs=[pl.BlockSpec(memory_space=pltpu.SMEM),
                  pl.BlockSpec((nb, BLK, 256), lambda bb, i: (bb, i, C_SWA // 256)),
                  pl.BlockSpec((nb, BLK, 256), lambda bb, i: (bb, i, C_SWA // 256 + 1)),
                  pl.BlockSpec((nb, BLK, 256), lambda bb, i: (bb, jnp.maximum(i - 1, 0), C_SWA // 256 + 1))],
        out_specs=pl.BlockSpec((nb, BLK, 256), lambda bb, i: (bb, i, 0)),
        out_shape=jax.ShapeDtypeStruct((bsz, seq, SWA_HEADS * HEAD_DIM), BF16),
        compiler_params=_params("parallel", "parallel"),
        name="swa",
    )(sink, u3, u3, u3)
    return out.reshape(bsz * seq, SWA_HEADS * HEAD_DIM)


def _ret_kernel(qk_ref, v_ref, g_ref, gng_ref, gnb_ref, o_ref, state_ref):
    n = pl.program_id(1)

    @pl.when(n == 0)
    def _():
        state_ref[...] = jnp.zeros_like(state_ref)

    a = lax.broadcasted_iota(jnp.int32, (BLK, BLK), 0)
    c = lax.broadcasted_iota(jnp.int32, (BLK, BLK), 1)
    diff = (a - c).astype(F32)
    idx = lax.broadcasted_iota(jnp.int32, (BLK, 1), 0).astype(F32)
    hq = RET_HEADS * RET_QK
    for s in range(qk_ref.shape[0]):
        qk = qk_ref[s]
        vv = v_ref[s]
        gate = g_ref[s].astype(F32)
        for h in range(RET_HEADS):
            lg = RET_LOG_GAMMA[h]
            q = qk[:, h * RET_QK:(h + 1) * RET_QK]
            k = qk[:, hq + h * RET_QK:hq + (h + 1) * RET_QK].astype(F32) * (RET_QK ** -0.5)
            v = vv[:, h * RET_V:(h + 1) * RET_V]
            dmat = jnp.where(diff >= 0, jnp.exp(lg * jnp.maximum(diff, 0.0)), 0.0)
            inner = _dot_nt(q, k.astype(BF16)) * dmat
            o_inner = _dot(inner.astype(BF16), v)
            zeta = jnp.exp(lg * (BLK - 1 - idx))
            xi = jnp.exp(lg * (idx + 1.0))
            kv = lax.dot_general((k * zeta).astype(BF16), v, TN_DIMS, preferred_element_type=F32)
            prev = state_ref[s * RET_HEADS + h]
            o = o_inner + _dot(q, prev.astype(BF16)) * xi
            state_ref[s * RET_HEADS + h] = math.exp(lg * BLK) * prev + kv
            mu = jnp.mean(o, axis=-1, keepdims=True)
            cen = o - mu
            var = jnp.mean(cen * cen, axis=-1, keepdims=True)
            sl = slice(h * RET_V, (h + 1) * RET_V)
            on = cen * lax.rsqrt(var + LN_EPS) * gng_ref[:, sl] + gnb_ref[:, sl]
            o_ref[s, :, sl] = (_silu(gate[:, sl]) * on).astype(o_ref.dtype)


def _retention(ua, gn_g, gn_b, bsz, seq):
    nc = seq // BLK
    nb = _seqs_per_step(bsz)
    u3 = ua.reshape(bsz, seq, UA)
    vec = pl.BlockSpec((1, RET_HEADS * RET_V), lambda bb, i: (0, 0))
    out = pl.pallas_call(
        _ret_kernel,
        grid=(bsz // nb, nc),
        in_specs=[pl.BlockSpec((nb, BLK, 512), lambda bb, i: (bb, i, C_RETQK // 512)),
                  pl.BlockSpec((nb, BLK, 512), lambda bb, i: (bb, i, C_RETV // 512)),
                  pl.BlockSpec((nb, BLK, 512), lambda bb, i: (bb, i, C_RETG // 512)),
                  vec, vec],
        out_specs=pl.BlockSpec((nb, BLK, 512), lambda bb, i: (bb, i, 0)),
        out_shape=jax.ShapeDtypeStruct((bsz, seq, RET_HEADS * RET_V), BF16),
        scratch_shapes=[pltpu.VMEM((nb * RET_HEADS, RET_QK, RET_V), F32)],
        compiler_params=_params("parallel", "arbitrary"),
        name="retention",
    )(u3, u3, u3, gn_g.reshape(1, -1), gn_b.reshape(1, -1))
    return out.reshape(bsz * seq, RET_HEADS * RET_V)


def _compress_kernel(t_ref, pek_ref, w1k_ref, w2k_ref, pev_ref, w1v_ref, w2v_ref, ok_ref, ov_ref):
    st = NSA_CMP_STRIDE
    ng = t_ref.shape[0] // st
    branches = ((pek_ref, w1k_ref, w2k_ref, ok_ref), (pev_ref, w1v_ref, w2v_ref, ov_ref))
    first = [jnp.zeros((ng, NSA_CMP_HIDDEN), F32) for _ in branches]
    second = [jnp.zeros((ng, NSA_CMP_HIDDEN), F32) for _ in branches]
    for r in range(st):
        rows = t_ref[pl.ds(r, ng, stride=st), :]
        for n, (pe_ref, w1_ref, _, _) in enumerate(branches):
            v = rows[:, n * HEAD_DIM:(n + 1) * HEAD_DIM]
            lo = (v + pe_ref[r:r + 1, :]).astype(BF16)
            hi = (v + pe_ref[st + r:st + r + 1, :]).astype(BF16)
            first[n] = first[n] + _dot(lo, w1_ref[r * HEAD_DIM:(r + 1) * HEAD_DIM, :])
            second[n] = second[n] + _dot(hi, w1_ref[(st + r) * HEAD_DIM:(st + r + 1) * HEAD_DIM, :])
    for n, (_, _, w2_ref, o_ref) in enumerate(branches):
        hid = first[n] + pltpu.roll(second[n], ng - 1, axis=0)
        act = 0.5 * hid * (1.0 + lax.erf(hid * (2.0 ** -0.5)))
        o_ref[0] = _dot(act.astype(BF16), w2_ref[...]).astype(o_ref.dtype)


def _compress(kcvc, pe_k, w1_k, w2_k, pe_v, w1_v, w2_v, bsz, seq):
    ng = seq // NSA_CMP_STRIDE
    pes = pl.BlockSpec((NSA_CMP_LEN, HEAD_DIM), lambda bb: (0, 0))
    w1s = pl.BlockSpec((NSA_CMP_LEN * HEAD_DIM, NSA_CMP_HIDDEN), lambda bb: (0, 0))
    w2s = pl.BlockSpec((NSA_CMP_HIDDEN, HEAD_DIM), lambda bb: (0, 0))
    osp = pl.BlockSpec((1, ng, HEAD_DIM), lambda bb: (bb, 0, 0))
    return pl.pallas_call(
        _compress_kernel,
        grid=(bsz,),
        in_specs=[pl.BlockSpec((seq, 128), lambda bb: (bb, 0)), pes, w1s, w2s, pes, w1s, w2s],
        out_specs=[osp, osp],
        out_shape=[jax.ShapeDtypeStruct((bsz, ng, HEAD_DIM), BF16)] * 2,
        compiler_params=_params("parallel"),
        name="nsa_compress",
    )(kcvc, pe_k, w1_k, w2_k, pe_v, w1_v, w2_v)


NSA_KC = 1024
NSA_WSPAN = NSA_WINDOW + BLK


NSA_PAD = 128
Q_LANES = NSA_HEADS * BLK


def _nsa_query_select():
    sel = np.zeros((NSA_HEADS * HEAD_DIM, NSA_HEADS * NSA_PAD), np.float32)
    for h in range(NSA_HEADS):
        for dd in range(HEAD_DIM):
            sel[h * HEAD_DIM + dd, h * NSA_PAD + dd] = SCALE
    return jnp.asarray(sel, BF16)


def _softmax_down(s):
    m = jnp.maximum(jnp.max(s, axis=0, keepdims=True), -1e29)
    e = jnp.exp(s - m)
    return e / jnp.maximum(jnp.sum(e, axis=0, keepdims=True), 1e-30)


def _nsa_kernel(q_ref, qsel_ref, kcmp_ref, vcmp_ref, ksvs_ref, kwvw_ref, gate_ref, o_ref,
                ksa_ref, kwa_ref, kca_ref, vst_ref, vwt_ref, vct_ref, ext_ref, bsel_ref):
    i = pl.program_id(1)
    seq = ksvs_ref.shape[0]
    ncmp = kcmp_ref.shape[1]
    nslc = seq // NSA_SLC_LEN
    lane = lax.broadcasted_iota(jnp.int32, (1, NSA_PAD), 1)

    @pl.when(i == 0)
    def _():
        def aug(k, pos):
            kf = jnp.concatenate([k.astype(F32), jnp.zeros((k.shape[0], NSA_PAD - HEAD_DIM), F32)], axis=1)
            hi = (pos // 128).astype(F32)
            lo = (pos % 128).astype(F32)
            return (kf + jnp.where(lane == HEAD_DIM, hi, jnp.where(lane == HEAD_DIM + 1, lo, 0.0))).astype(BF16)

        def value_t(v):
            vf = jnp.concatenate([v.astype(F32), jnp.zeros((v.shape[0], NSA_PAD - HEAD_DIM), F32)], axis=1)
            return vf.T.astype(BF16)

        kpos = lax.broadcasted_iota(jnp.int32, (seq, 1), 0)
        ksvs = ksvs_ref[...]
        kwvw = kwvw_ref[...]
        ksa_ref[...] = aug(ksvs[:, :HEAD_DIM], kpos)
        kwa_ref[...] = aug(kwvw[:, :HEAD_DIM], kpos)
        cend = lax.broadcasted_iota(jnp.int32, (ncmp, 1), 0) * NSA_CMP_STRIDE + (NSA_CMP_LEN - 1)
        kca_ref[...] = aug(kcmp_ref[0], cend)
        vst_ref[...] = value_t(ksvs[:, HEAD_DIM:])
        vwt_ref[...] = value_t(kwvw[:, HEAD_DIM:])
        vct_ref[...] = value_t(vcmp_ref[0])
        ext_ref[...] = jnp.where(kpos // NSA_SLC_LEN == lax.broadcasted_iota(jnp.int32, (seq, NSA_PAD), 1),
                                 1.0, 0.0).astype(BF16)

    t0 = i * BLK
    tq = t0 + lax.broadcasted_iota(jnp.int32, (1, BLK), 1)
    qw = _dot(q_ref[...], qsel_ref[...])
    qaug = []
    for h in range(NSA_HEADS):
        coef = jnp.where(lane == HEAD_DIM, 128.0 * SLOPES_NSA[h], jnp.where(lane == HEAD_DIM + 1, SLOPES_NSA[h], 0.0))
        qaug.append((qw[:, h * NSA_PAD:(h + 1) * NSA_PAD] + coef).astype(BF16))
    qaug = jnp.concatenate(qaug, axis=0)

    def tile4(b):
        return jnp.concatenate([b] * NSA_HEADS, axis=1)

    cend = lax.broadcasted_iota(jnp.int32, (ncmp, 1), 0) * NSA_CMP_STRIDE + (NSA_CMP_LEN - 1)
    p_cmp = _softmax_down(_dot_nt(kca_ref[...], qaug) + tile4(jnp.where(cend <= tq, 0.0, NEG_INF)))
    o_cmp = _dot(vct_ref[...], p_cmp.astype(BF16))
    psum = p_cmp[:, 0:BLK]
    for h in range(1, NSA_HEADS):
        psum = psum + p_cmp[:, h * BLK:(h + 1) * BLK]

    sst = lax.broadcasted_iota(jnp.int32, (nslc, ncmp), 0) * NSA_SLC_LEN
    cst = lax.broadcasted_iota(jnp.int32, (nslc, ncmp), 1) * NSA_CMP_STRIDE
    overlap = jnp.where((cst < sst + NSA_SLC_LEN) & (cst + NSA_CMP_LEN > sst), 1.0, 0.0).astype(BF16)
    p_hi = psum.astype(BF16)
    p_lo = (psum - p_hi.astype(F32)).astype(BF16)
    imp = _dot(overlap, p_hi) + _dot(overlap, p_lo)
    jj = lax.broadcasted_iota(jnp.int32, (nslc, BLK), 0)
    cur = tq // NSA_SLC_LEN
    forced = (jj == 0) | (jj == cur) | (jj == cur - 1)
    imp = jnp.where(forced, NSA_FORCE, jnp.where(jj > cur, -1.0, imp))

    sel = jnp.zeros((nslc, BLK), F32)
    for _ in range(min(NSA_TOPN, nslc)):
        mx = jnp.max(imp, axis=0, keepdims=True)
        first = jnp.min(jnp.where(imp == mx, jj, nslc), axis=0, keepdims=True)
        hit = jj == first
        sel = jnp.where(hit, 1.0, sel)
        imp = jnp.where(hit, -3e38, imp)
    sel = jnp.concatenate([sel, jnp.zeros((NSA_PAD - nslc, BLK), F32)], axis=0).astype(BF16)
    bsel_ref[...] = (_dot(ext_ref[...], sel) - 1.0) * 1e30

    def slc_step(c, carry):
        m, l, acc = carry
        start = pl.multiple_of(c * NSA_KC, NSA_KC)
        kpos = start + lax.broadcasted_iota(jnp.int32, (NSA_KC, 1), 0)
        bias = bsel_ref[pl.ds(start, NSA_KC), :] + jnp.where(kpos <= tq, 0.0, NEG_INF)
        s = _dot_nt(ksa_ref[pl.ds(start, NSA_KC), :], qaug) + tile4(bias)
        m_new = jnp.maximum(m, jnp.max(s, axis=0, keepdims=True))
        alpha = jnp.exp(m - m_new)
        p = jnp.exp(s - m_new)
        l_new = alpha * l + jnp.sum(p, axis=0, keepdims=True)
        acc_new = alpha * acc + _dot(vst_ref[:, pl.ds(start, NSA_KC)], p.astype(BF16))
        return m_new, l_new, acc_new

    init = (jnp.full((1, Q_LANES), -1e29, F32), jnp.zeros((1, Q_LANES), F32), jnp.zeros((NSA_PAD, Q_LANES), F32))
    _, l_slc, acc_slc = lax.fori_loop(0, (t0 + BLK - 1) // NSA_KC + 1, slc_step, init)
    o_slc = acc_slc / jnp.maximum(l_slc, 1e-30)

    wspan = min(NSA_WSPAN, seq)
    wstart = pl.multiple_of(jnp.maximum(t0 + BLK - wspan, 0), BLK)
    wdist = tq - (wstart + lax.broadcasted_iota(jnp.int32, (wspan, 1), 0))
    wbias = jnp.where((wdist >= 0) & (wdist < NSA_WINDOW), 0.0, NEG_INF)
    p_win = _softmax_down(_dot_nt(kwa_ref[pl.ds(wstart, wspan), :], qaug) + tile4(wbias))
    o_win = _dot(vwt_ref[:, pl.ds(wstart, wspan)], p_win.astype(BF16))

    gts = jax.nn.sigmoid(gate_ref[...].astype(F32)).T
    outs = []
    for h in range(NSA_HEADS):
        cols = slice(h * BLK, (h + 1) * BLK)
        outs.append(gts[h:h + 1, :] * o_cmp[:HEAD_DIM, cols]
                    + gts[NSA_HEADS + h:NSA_HEADS + h + 1, :] * o_slc[:HEAD_DIM, cols]
                    + gts[2 * NSA_HEADS + h:2 * NSA_HEADS + h + 1, :] * o_win[:HEAD_DIM, cols])
    o_ref[...] = jnp.concatenate(outs, axis=0).T.astype(o_ref.dtype)


def _nsa(ua, kcmp, vcmp, bsz, seq):
    t = bsz * seq
    nq = seq // BLK
    ncmp = kcmp.shape[1]
    qsel = _nsa_query_select()
    staged_k = pltpu.VMEM((seq, NSA_PAD), BF16)
    staged_v = pltpu.VMEM((NSA_PAD, seq), BF16)
    return pl.pallas_call(
        _nsa_kernel,
        grid=(bsz, nq),
        in_specs=[pl.BlockSpec((BLK, 256), lambda bb, i: (bb * nq + i, C_NSAQ // 256)),
                  pl.BlockSpec(qsel.shape, lambda bb, i: (0, 0)),
                  pl.BlockSpec((1, ncmp, HEAD_DIM), lambda bb, i: (bb, 0, 0)),
                  pl.BlockSpec((1, ncmp, HEAD_DIM), lambda bb, i: (bb, 0, 0)),
                  pl.BlockSpec((seq, 128), lambda bb, i: (bb, C_NSAS // 128)),
                  pl.BlockSpec((seq, 128), lambda bb, i: (bb, C_NSAW // 128)),
                  pl.BlockSpec((BLK, 128), lambda bb, i: (bb * nq + i, C_NSAG // 128))],
        out_specs=pl.BlockSpec((BLK, 256), lambda bb, i: (bb * nq + i, 0)),
        out_shape=jax.ShapeDtypeStruct((t, NSA_HEADS * HEAD_DIM), BF16),
        scratch_shapes=[staged_k, staged_k, pltpu.VMEM((ncmp, NSA_PAD), BF16),
                        staged_v, staged_v, pltpu.VMEM((NSA_PAD, ncmp), BF16),
                        staged_k, pltpu.VMEM((seq, BLK), F32)],
        compiler_params=_params("parallel", "arbitrary"),
        name="nsa_attention",
    )(ua, qsel, kcmp, vcmp, ua, ua, ua)


MERGE_TM = 512


def _merge_kernel(x_ref, conv_ref, swa_ref, ret_ref, nsa_ref, wm_ref, pc_ref, ps_ref, pr_ref, pn_ref,
                  wo_ref, g_ref, b_ref, o_ref, op_ref):
    x = x_ref[...]
    xb = x.astype(BF16)
    merged = None
    for n, (a_ref, p_ref) in enumerate(((conv_ref, pc_ref), (swa_ref, ps_ref), (ret_ref, pr_ref), (nsa_ref, pn_ref))):
        gate = jax.nn.sigmoid(_dot(xb, wm_ref[:, n * D_MODEL:(n + 1) * D_MODEL]))
        term = gate * _dot(a_ref[...], p_ref[...])
        merged = term if merged is None else merged + term
    mix = _dot(merged.astype(BF16), wo_ref[...])
    y = _layer_norm(ALPHA * x + mix, g_ref[...], b_ref[...])
    o_ref[...] = y
    op_ref[...] = y.astype(BF16)


def _merge(x2d, conv_a, swa_o, ret_o, nsa_o, w_m, p_conv, p_swa, p_ret, p_nsa, w_out, g, b):
    t = x2d.shape[0]
    tm = min(MERGE_TM, t)

    def rows(width):
        return pl.BlockSpec((tm, width), lambda i: (i, 0))

    def whole(arr):
        return pl.BlockSpec(arr.shape, lambda i: (0, 0), pipeline_mode=pl.Buffered(1))

    g2, b2 = g.reshape(1, -1), b.reshape(1, -1)
    return pl.pallas_call(
        _merge_kernel,
        grid=(t // tm,),
        in_specs=[rows(D_MODEL), rows(256), rows(256), rows(512), rows(256),
                  whole(w_m), whole(p_conv), whole(p_swa), whole(p_ret), whole(p_nsa), whole(w_out),
                  whole(g2), whole(b2)],
        out_specs=[rows(D_MODEL), rows(D_MODEL)],
        out_shape=[jax.ShapeDtypeStruct((t, D_MODEL), F32), jax.ShapeDtypeStruct((t, D_MODEL), BF16)],
        compiler_params=_params("parallel"),
        name="merge_outproj_ln",
    )(x2d, conv_a, swa_o, ret_o, nsa_o, w_m, p_conv, p_swa, p_ret, p_nsa, w_out, g2, b2)


MOE_TT = 256
RUN_ALIGN = 16
MOE_LR = 2560
MOE_CHUNK = 256
RUN_PIECES = (256, 128, 64, 32, 16)
ROW_W = D_MODEL + 128
assert MOE_LR >= TOP_K * MOE_TT + N_EXPERTS * (RUN_ALIGN - 1) and MOE_LR % MOE_CHUNK == 0


def _first_argmax(v, iota, size):
    m = jnp.max(v, axis=0, keepdims=True)
    idx = jnp.min(jnp.where(v == m, iota, size), axis=0, keepdims=True)
    return m, idx


def _router_kernel(x_ref, w_ref, b_ref, lp_ref, lpt_ref, gate_ref, tab_ref, cnt_ref, carry_ref):
    i = pl.program_id(0)
    tr = x_ref.shape[0]

    @pl.when(i == 0)
    def _():
        carry_ref[...] = jnp.zeros_like(carry_ref)

    x = x_ref[...]
    w = w_ref[...]
    xh = x.astype(BF16)
    xl = (x - xh.astype(F32)).astype(BF16)
    wh = w.astype(BF16)
    wl = (w - wh.astype(F32)).astype(BF16)
    logits = _dot_nt(wh, xh) + _dot_nt(wh, xl) + _dot_nt(wl, xh)
    scores = jax.nn.sigmoid(logits)
    biased = scores + b_ref[...]

    iota_g = lax.broadcasted_iota(jnp.int32, (GROUP_SIZE, tr), 0)
    grp = []
    for g in range(N_GROUPS):
        vg = biased[g * GROUP_SIZE:(g + 1) * GROUP_SIZE, :]
        m1, i1 = _first_argmax(vg, iota_g, GROUP_SIZE)
        m2 = jnp.max(jnp.where(iota_g == i1, -jnp.inf, vg), axis=0, keepdims=True)
        grp.append(m1 + m2)
    work = jnp.concatenate(grp, axis=0)
    iota_n = lax.broadcasted_iota(jnp.int32, (N_GROUPS, tr), 0)
    gsel = jnp.zeros((N_GROUPS, tr), F32)
    for _ in range(TOPK_GROUPS):
        _, gi = _first_argmax(work, iota_n, N_GROUPS)
        hit = iota_n == gi
        gsel = jnp.where(hit, 1.0, gsel)
        work = jnp.where(hit, -jnp.inf, work)
    emask = jnp.concatenate([jnp.broadcast_to(gsel[g:g + 1, :], (GROUP_SIZE, tr)) for g in range(N_GROUPS)], axis=0)
    work = jnp.where(emask > 0.5, biased, NEG_INF)

    iota_e = lax.broadcasted_iota(jnp.int32, (N_EXPERTS, tr), 0)
    hits, gates = [], []
    chosen = jnp.zeros((N_EXPERTS, tr), F32)
    for _ in range(TOP_K):
        _, ei = _first_argmax(work, iota_e, N_EXPERTS)
        hit = iota_e == ei
        hits.append(hit)
        gates.append(jnp.sum(jnp.where(hit, scores, 0.0), axis=0, keepdims=True))
        chosen = jnp.where(hit, 1.0, chosen)
        work = jnp.where(hit, -jnp.inf, work)
    gsum = gates[0]
    for gk in gates[1:]:
        gsum = gsum + gk
    gate_ref[...] = jnp.concatenate([gk / gsum * ROUTE_SCALE for gk in gates] + [jnp.zeros((8 - TOP_K, tr), F32)], axis=0)

    r = lax.broadcasted_iota(jnp.int32, (tr, tr), 0)
    c = lax.broadcasted_iota(jnp.int32, (tr, tr), 1)
    rank = _dot(chosen.astype(BF16), jnp.where(r < c, 1.0, 0.0).astype(BF16))
    run_len = jnp.ceil(jnp.sum(chosen, axis=1, keepdims=True) * (1.0 / RUN_ALIGN)) * RUN_ALIGN
    er = lax.broadcasted_iota(jnp.int32, (N_EXPERTS, N_EXPERTS), 0)
    ec = lax.broadcasted_iota(jnp.int32, (N_EXPERTS, N_EXPERTS), 1)
    units = jnp.broadcast_to(run_len * (1.0 / RUN_ALIGN), (N_EXPERTS, 128)).astype(BF16)
    run_off = _dot(jnp.where(ec < er, 1.0, 0.0).astype(BF16), units)[:, :1] * RUN_ALIGN
    local = [jnp.sum(jnp.where(h, run_off + rank, 0.0), axis=0, keepdims=True) for h in hits]
    lp = jnp.concatenate(local + [jnp.zeros((8 - TOP_K, tr), F32)], axis=0)
    lp_ref[...] = lp.astype(jnp.int32)

    lp128 = jnp.concatenate([lp, jnp.zeros((128 - 8, tr), F32)], axis=0)
    hi = jnp.floor(lp128 * (1.0 / 64.0))
    lo = lp128 - 64.0 * hi
    eye = jnp.where(r == c, 1.0, 0.0).astype(BF16)
    lpt_ref[...] = 64.0 * _dot_nt(eye, hi.astype(BF16)) + _dot_nt(eye, lo.astype(BF16))

    def as_row(col):
        row = jnp.sum(jnp.where(er == ec, jnp.broadcast_to(col, (N_EXPERTS, N_EXPERTS)), 0.0), axis=0, keepdims=True)
        return jnp.concatenate([row, jnp.zeros((1, 128 - N_EXPERTS), F32)], axis=1)

    total = jnp.sum(run_len, axis=0, keepdims=True)
    n_large = jnp.sum(jnp.where(run_len >= float(LARGE_PIECES[0]), 1.0, 0.0), axis=0, keepdims=True)
    lane_t = lax.broadcasted_iota(jnp.int32, (1, 128), 1)
    total_row = jnp.where(lane_t == 0, total, jnp.where(lane_t == 1, n_large, 0.0))
    tab = jnp.concatenate([as_row(run_len), as_row(run_off), as_row(carry_ref[...]), total_row,
                           jnp.zeros((4, 128), F32)], axis=0)
    tab_ref[...] = tab.astype(jnp.int32)
    carry_ref[...] = carry_ref[...] + run_len
    cnt_ref[...] = jnp.broadcast_to(carry_ref[...], cnt_ref.shape)


def _router(x1, router_w, router_b):
    t = x1.shape[0]
    tr = MOE_TT
    nt = t // tr
    col = pl.BlockSpec((8, tr), lambda i: (0, i))
    return pl.pallas_call(
        _router_kernel,
        grid=(nt,),
        in_specs=[pl.BlockSpec((tr, D_MODEL), lambda i: (i, 0)),
                  pl.BlockSpec((N_EXPERTS, D_MODEL), lambda i: (0, 0)),
                  pl.BlockSpec((N_EXPERTS, 1), lambda i: (0, 0))],
        out_specs=[col, pl.BlockSpec((tr, 128), lambda i: (i, 0)), col,
                   pl.BlockSpec((None, 8, 128), lambda i: (i, 0, 0)),
                   pl.BlockSpec((N_EXPERTS, 128), lambda i: (0, 0))],
        out_shape=[jax.ShapeDtypeStruct((8, t), jnp.int32), jax.ShapeDtypeStruct((t, 128), F32),
                   jax.ShapeDtypeStruct((8, t), F32), jax.ShapeDtypeStruct((nt, 8, 128), jnp.int32),
                   jax.ShapeDtypeStruct((N_EXPERTS, 128), F32)],
        scratch_shapes=[pltpu.VMEM((N_EXPERTS, 1), F32)],
        compiler_params=_params("arbitrary"),
        name="moe_router",
    )(x1, router_w, router_b.reshape(-1, 1))


SMALL_PIECES = (16, 32)
LARGE_PIECES = (64, 128, 256)
WAIT_PIECES = (2048, 1024, 512, 256, 128, 64, 32, 16)
assert sum(WAIT_PIECES) >= MOE_LR and sum(SMALL_PIECES + LARGE_PIECES) >= MOE_TT


def _start_run_copies(pstart_ref, tab_ref, make_copy):
    def start_pieces(e, sizes, skip_mask):
        length = tab_ref[0, e]
        local = tab_ref[1, e]
        glob = pstart_ref[e] + tab_ref[2, e]
        done = length & skip_mask
        for size in sizes:
            piece = length & size

            @pl.when(piece != 0)
            def _():
                make_copy(pl.multiple_of(local + done, RUN_ALIGN), pl.multiple_of(glob + done, RUN_ALIGN),
                          size).start()

            done = done + piece

    def small(e, carry):
        start_pieces(e, SMALL_PIECES, 0)
        return carry

    def large(e, carry):
        start_pieces(e, LARGE_PIECES, sum(SMALL_PIECES))
        return carry

    lax.fori_loop(0, N_EXPERTS, small, 0)

    @pl.when(tab_ref[3, 1] > 0)
    def _():
        lax.fori_loop(0, N_EXPERTS, large, 0)


def _wait_run_copies(tab_ref, make_copy):
    total = tab_ref[3, 0]
    for size in WAIT_PIECES:
        @pl.when((total & size) != 0)
        def _():
            make_copy(0, 0, size).wait()


def _dispatch_kernel(pstart_ref, tail_ref, taillen_ref, nused_ref, tab_ref, lp_ref, gate_ref, x_ref, o_hbm,
                     buf_ref, sem):
    i = pl.program_id(0)

    def make_copy(local, glob, size):
        return pltpu.make_async_copy(buf_ref.at[pl.ds(local, size)], o_hbm.at[pl.ds(glob, size)], sem)

    @pl.when(i == 0)
    def _():
        buf_ref[0:MOE_BLK, :] = jnp.zeros((MOE_BLK, ROW_W), BF16)
        for action in ("start", "wait"):
            def per_expert(e, carry, action=action):
                done = 0
                for size in RUN_PIECES:
                    piece = taillen_ref[e] & size

                    @pl.when(piece != 0)
                    def _():
                        cp = make_copy(0, pl.multiple_of(tail_ref[e] + done, RUN_ALIGN), size)
                        cp.start() if action == "start" else cp.wait()

                    done = done + piece
                return carry
            lax.fori_loop(0, N_EXPERTS, per_expert, 0)

            def per_block(r, carry, action=action):
                cp = make_copy(0, pl.multiple_of(r * MOE_BLK, MOE_BLK), MOE_BLK)
                cp.start() if action == "start" else cp.wait()
                return carry
            lax.fori_loop(nused_ref[0], o_hbm.shape[0] // MOE_BLK, per_block, 0)

    xb = x_ref[...]
    lp = lp_ref[...].astype(F32)
    gates = gate_ref[...]
    g_hi = gates.astype(BF16)
    g_lo = (gates - g_hi.astype(F32)).astype(BF16)
    rows = lax.broadcasted_iota(jnp.int32, (MOE_CHUNK, MOE_TT), 0).astype(F32).astype(BF16)
    ones_col = jnp.ones((MOE_TT, ROW_W - D_MODEL), BF16)
    glane = lax.broadcasted_iota(jnp.int32, (1, ROW_W - D_MODEL), 1)
    one, zero = jnp.ones((), BF16), jnp.zeros((), BF16)
    for c in range(MOE_LR // MOE_CHUNK):
        local = (lp - float(c * MOE_CHUNK)).astype(BF16)
        hit = rows == local[0:1, :]
        member = hit
        w_hi = jnp.where(hit, g_hi[0:1, :], zero)
        w_lo = jnp.where(hit, g_lo[0:1, :], zero)
        for k in range(1, TOP_K):
            hit = rows == local[k:k + 1, :]
            member = member | hit
            w_hi = jnp.where(hit, g_hi[k:k + 1, :], w_hi)
            w_lo = jnp.where(hit, g_lo[k:k + 1, :], w_lo)
        picked = _dot(jnp.where(member, one, zero), xb)
        gate_hi = _dot(w_hi, ones_col)
        gate_lo = _dot(w_lo, ones_col)
        sl = slice(c * MOE_CHUNK, (c + 1) * MOE_CHUNK)
        buf_ref[sl, :D_MODEL] = picked.astype(BF16)
        buf_ref[sl, D_MODEL:] = jnp.where(glane == 0, gate_hi, jnp.where(glane == 1, gate_lo, 0.0)).astype(BF16)

    _start_run_copies(pstart_ref, tab_ref, make_copy)
    _wait_run_copies(tab_ref, make_copy)


def _dispatch(pstart, tail_start, tail_len, n_used, tab, lp, gates, x1b, rows):
    t = x1b.shape[0]
    tt = MOE_TT
    grid_spec = pltpu.PrefetchScalarGridSpec(
        num_scalar_prefetch=4,
        grid=(t // tt,),
        in_specs=[pl.BlockSpec((None, 8, 128), lambda i, *_: (i, 0, 0), memory_space=pltpu.SMEM),
                  pl.BlockSpec((8, tt), lambda i, *_: (0, i)),
                  pl.BlockSpec((8, tt), lambda i, *_: (0, i)),
                  pl.BlockSpec((tt, D_MODEL), lambda i, *_: (i, 0))],
        out_specs=pl.BlockSpec(memory_space=pl.ANY),
        scratch_shapes=[pltpu.VMEM((MOE_LR, ROW_W), BF16), pltpu.SemaphoreType.DMA],
    )
    return pl.pallas_call(
        _dispatch_kernel,
        grid_spec=grid_spec,
        out_shape=jax.ShapeDtypeStruct((rows, ROW_W), BF16),
        compiler_params=_params("arbitrary"),
        name="moe_dispatch",
    )(pstart, tail_start, tail_len, n_used, tab, lp, gates, x1b)


def _expert_kernel(be_ref, nu_ref, x_ref, wg_ref, wu_ref, wd_ref, o_ref, wgb_ref, wub_ref, wdb_ref):
    r = pl.program_id(0)

    @pl.when((r == 0) | (be_ref[r] != be_ref[jnp.maximum(r - 1, 0)]))
    def _():
        wgb_ref[...] = wg_ref[...].astype(BF16)
        wub_ref[...] = wu_ref[...].astype(BF16)
        wdb_ref[...] = wd_ref[...].astype(BF16)

    @pl.when(r < nu_ref[0])
    def _():
        xb = x_ref[:, :D_MODEL]
        gate_terms = x_ref[:, D_MODEL:D_MODEL + 2].astype(F32)
        row_gate = gate_terms[:, 0:1] + gate_terms[:, 1:2]
        h = _silu(_dot(xb, wgb_ref[...])) * _dot(xb, wub_ref[...])
        o_ref[...] = (_dot(h.astype(BF16), wdb_ref[...]) * row_gate).astype(o_ref.dtype)

    @pl.when(r >= nu_ref[0])
    def _():
        o_ref[...] = jnp.zeros_like(o_ref)


def _experts(blk_exp, n_used, xs, wg, wu, wd, layer):
    rows = xs.shape[0]
    nblk = rows // MOE_BLK
    grid_spec = pltpu.PrefetchScalarGridSpec(
        num_scalar_prefetch=2,
        grid=(nblk,),
        in_specs=[pl.BlockSpec((MOE_BLK, ROW_W), lambda r, be, nu: (jnp.minimum(r, jnp.maximum(nu[0] - 1, 0)), 0)),
                  pl.BlockSpec((None, None, D_MODEL, D_EXPERT), lambda r, be, nu: (layer, be[r], 0, 0)),
                  pl.BlockSpec((None, None, D_MODEL, D_EXPERT), lambda r, be, nu: (layer, be[r], 0, 0)),
                  pl.BlockSpec((None, None, D_EXPERT, D_MODEL), lambda r, be, nu: (layer, be[r], 0, 0))],
        out_specs=pl.BlockSpec((MOE_BLK, D_MODEL), lambda r, be, nu: (r, 0)),
        scratch_shapes=[pltpu.VMEM((D_MODEL, D_EXPERT), BF16), pltpu.VMEM((D_MODEL, D_EXPERT), BF16),
                        pltpu.VMEM((D_EXPERT, D_MODEL), BF16)],
    )
    return pl.pallas_call(
        _expert_kernel,
        grid_spec=grid_spec,
        out_shape=jax.ShapeDtypeStruct((rows, D_MODEL), BF16),
        compiler_params=_params("arbitrary"),
        name="moe_experts",
    )(blk_exp, n_used, xs, wg, wu, wd)


def _combine_kernel(pstart_ref, tab_ref, lpt_ref, x_ref, y_hbm, swg_ref, swu_ref, swd_ref, g_ref, b_ref, o_ref,
                    buf_ref, sem):
    i = pl.program_id(0)

    @pl.when(i == 0)
    def _():
        buf_ref[...] = jnp.zeros_like(buf_ref)

    def make_copy(local, glob, size):
        return pltpu.make_async_copy(y_hbm.at[pl.ds(glob, size)], buf_ref.at[pl.ds(local, size)], sem)

    _start_run_copies(pstart_ref, tab_ref, make_copy)

    x = x_ref[...]
    xb = x.astype(BF16)
    h = _silu(_dot(xb, swg_ref[...])) * _dot(xb, swu_ref[...])
    acc = ALPHA * x + _dot(h.astype(BF16), swd_ref[...])

    _wait_run_copies(tab_ref, make_copy)

    lpt = lpt_ref[...]
    cols = lax.broadcasted_iota(jnp.int32, (MOE_TT, MOE_CHUNK), 1).astype(F32).astype(BF16)
    one, zero = jnp.ones((), BF16), jnp.zeros((), BF16)
    for c in range(MOE_LR // MOE_CHUNK):
        local = (lpt - float(c * MOE_CHUNK)).astype(BF16)
        member = cols == local[:, 0:1]
        for k in range(1, TOP_K):
            member = member | (cols == local[:, k:k + 1])
        acc = acc + _dot(jnp.where(member, one, zero), buf_ref[c * MOE_CHUNK:(c + 1) * MOE_CHUNK, :])
    o_ref[...] = _layer_norm(acc, g_ref[...], b_ref[...])


def _combine(pstart, tab, lpt, x1, y_rows, swg, swu, swd, g, b):
    t = x1.shape[0]
    tt = MOE_TT

    def whole(arr):
        return pl.BlockSpec(arr.shape, lambda i, ps: (0, 0))

    g2, b2 = g.reshape(1, -1), b.reshape(1, -1)
    grid_spec = pltpu.PrefetchScalarGridSpec(
        num_scalar_prefetch=1,
        grid=(t // tt,),
        in_specs=[pl.BlockSpec((None, 8, 128), lambda i, ps: (i, 0, 0), memory_space=pltpu.SMEM),
                  pl.BlockSpec((tt, 128), lambda i, ps: (i, 0)),
                  pl.BlockSpec((tt, D_MODEL), lambda i, ps: (i, 0)),
                  pl.BlockSpec(memory_space=pl.ANY),
                  whole(swg), whole(swu), whole(swd), whole(g2), whole(b2)],
        out_specs=pl.BlockSpec((tt, D_MODEL), lambda i, ps: (i, 0)),
        scratch_shapes=[pltpu.VMEM((MOE_LR, D_MODEL), BF16), pltpu.SemaphoreType.DMA],
    )
    return pl.pallas_call(
        _combine_kernel,
        grid_spec=grid_spec,
        out_shape=jax.ShapeDtypeStruct((t, D_MODEL), F32),
        compiler_params=_params("arbitrary"),
        name="moe_combine_shared_ln",
    )(pstart, tab, lpt, x1, y_rows, swg, swu, swd, g2, b2)


def _moe(x1, x1b, layer, router_w, router_b, wg, wu, wd, swg, swu, swd, g, b):
    t = x1.shape[0]
    lp, lpt, gates, tab, cnt = _router(x1, router_w, router_b)
    rows_e = cnt[:, 0].astype(jnp.int32)
    padded = (rows_e + MOE_BLK - 1) // MOE_BLK * MOE_BLK
    pend = jnp.cumsum(padded)
    pstart = pend - padded
    max_rows = t * TOP_K + (t // MOE_TT) * N_EXPERTS * (RUN_ALIGN - 1) + N_EXPERTS * (MOE_BLK - 1)
    n_blocks = -(-max_rows // MOE_BLK)
    blk_first = jnp.arange(n_blocks, dtype=jnp.int32) * MOE_BLK
    blk_exp = jnp.minimum(jnp.sum((pend[None, :] <= blk_first[:, None]).astype(jnp.int32), axis=1), N_EXPERTS - 1)
    n_used = (pend[-1:] // MOE_BLK).astype(jnp.int32)
    xs = _dispatch(pstart, pstart + rows_e, padded - rows_e, n_used, tab, lp, gates, x1b, n_blocks * MOE_BLK)
    y_rows = _experts(blk_exp, n_used, xs, wg, wu, wd, layer)
    return _combine(pstart, tab, lpt, x1, y_rows, swg, swu, swd, g, b)


def _layer(x2d, bsz, seq, layer, w_in, moe_wg, moe_wu, moe_wd,
           conv_dw, conv_db, conv_ln_g, conv_ln_b, swa_sink, ret_gn_g, ret_gn_b,
           nsa_pe_k, nsa_w1_k, nsa_w2_k, nsa_pe_v, nsa_w1_v, nsa_w2_v, p_conv, p_swa, p_ret, p_nsa, w_out,
           ln1_g, ln1_b, router_w, router_b, sh_wg, sh_wu, sh_wd, ln2_g, ln2_b):
    bf = lambda a: a.astype(BF16)
    w_a = _mix_weights(w_in, layer)
    w_m = _merge_gate_weights(w_in, layer)
    ua, kcvc = _inproj(x2d, w_a)
    conv_a = _conv(ua, conv_dw, conv_db, conv_ln_g, conv_ln_b, bsz, seq)
    swa_o = _swa(ua, swa_sink, bsz, seq)
    ret_o = _retention(ua, ret_gn_g, ret_gn_b, bsz, seq)
    kcmp, vcmp = _compress(kcvc, nsa_pe_k, bf(nsa_w1_k), bf(nsa_w2_k), nsa_pe_v, bf(nsa_w1_v), bf(nsa_w2_v), bsz, seq)
    nsa_o = _nsa(ua, kcmp, vcmp, bsz, seq)
    x1, x1b = _merge(x2d, conv_a, swa_o, ret_o, nsa_o, w_m, bf(p_conv), bf(p_swa), bf(p_ret), bf(p_nsa),
                     bf(w_out), ln1_g, ln1_b)
    return _moe(x1, x1b, layer, router_w, router_b, moe_wg, moe_wu, moe_wd,
                bf(sh_wg), bf(sh_wu), bf(sh_wd), ln2_g, ln2_b)


def kernel(x, w_in, conv_dw, conv_db, conv_ln_g, conv_ln_b, swa_sink, ret_gn_g, ret_gn_b, nsa_pe_k, nsa_w1_k, nsa_w2_k, nsa_pe_v, nsa_w1_v, nsa_w2_v, p_conv, p_swa, p_ret, p_nsa, w_out, ln1_g, ln1_b, router_w, router_b, moe_wg, moe_wu, moe_wd, sh_wg, sh_wu, sh_wd, ln2_g, ln2_b):
    bsz, seq, d = x.shape
    params = (conv_dw, conv_db, conv_ln_g, conv_ln_b, swa_sink, ret_gn_g, ret_gn_b,
              nsa_pe_k, nsa_w1_k, nsa_w2_k, nsa_pe_v, nsa_w1_v, nsa_w2_v, p_conv, p_swa, p_ret, p_nsa, w_out,
              ln1_g, ln1_b, router_w, router_b, sh_wg, sh_wu, sh_wd, ln2_g, ln2_b)
    x2d = x.reshape(bsz * seq, d)
    for l in range(w_in.shape[0]):
        x2d = _layer(x2d, bsz, seq, l, w_in, moe_wg, moe_wu, moe_wd, *[p[l] for p in params])
    return x2d.reshape(bsz, seq, d)
```

```python
import math

import jax
import jax.numpy as jnp
import numpy as np
from jax import lax
from jax.experimental import pallas as pl
from jax.experimental.pallas import tpu as pltpu

F32 = jnp.float32
BF16 = jnp.bfloat16

D_MODEL = 1024
DEPTH = 2
HEAD_DIM = 64
BLK = 128
CONV_CH = 256
CONV_K = 31
SWA_HEADS = 4
SWA_KV_HEADS = 2
SWA_WINDOW = 128
RET_HEADS = 4
RET_QK = 64
RET_V = 128
NSA_HEADS = 4
NSA_CMP_LEN = 32
NSA_CMP_STRIDE = 16
NSA_CMP_HIDDEN = 256
NSA_SLC_LEN = 64
NSA_TOPN = 8
NSA_WINDOW = 512
NSA_FORCE = 1e4
N_EXPERTS = 64
TOP_K = 6
N_GROUPS = 8
GROUP_SIZE = N_EXPERTS // N_GROUPS
TOPK_GROUPS = 4
D_EXPERT = 256
ROUTE_SCALE = 2.5
MOE_BLK = 512
N_BRANCH = 4
ALPHA = (2 * DEPTH) ** 0.25
LN_EPS = 1e-5
NEG_INF = -1e30
SCALE = HEAD_DIM ** -0.5

C_CONV = 0
C_SWA = 512
C_RETQK = 1024
C_RETV = 1536
C_RETG = 2048
C_NSAQ = 2560
C_NSAC = 2816
C_NSAS = 2944
C_NSAW = 3072
C_NSAG = 3200
N_MIX_COLS = 3212
UA = 3328

ALIBI = [2.0 ** (-8.0 * (i + 1) / 8) for i in range(8)]
SLOPES_SWA = ALIBI[:4]
SLOPES_NSA = ALIBI[4:]
RET_LOG_GAMMA = [math.log1p(-(2.0 ** (-5.0 - h))) for h in range(RET_HEADS)]

VMEM_LIMIT = 48 * 1024 * 1024
NT_DIMS = (((1,), (1,)), ((), ()))
TN_DIMS = (((0,), (0,)), ((), ()))


def _params(*sem):
    return pltpu.CompilerParams(dimension_semantics=sem, vmem_limit_bytes=VMEM_LIMIT)


def _dot(a, b):
    return jnp.dot(a, b, preferred_element_type=F32)


def _dot_nt(a, b):
    return lax.dot_general(a, b, NT_DIMS, preferred_element_type=F32)


def _layer_norm(v, g, b):
    mu = jnp.mean(v, axis=-1, keepdims=True)
    c = v - mu
    var = jnp.mean(c * c, axis=-1, keepdims=True)
    return c * lax.rsqrt(var + LN_EPS) * g + b


def _silu(v):
    return v * jax.nn.sigmoid(v)


CAST_TM = 256
LANES = 128


def _cast_kernel(x_ref, o_ref):
    o_ref[...] = x_ref[...].astype(o_ref.dtype)


def _mix_weights(w_in, layer):
    return pl.pallas_call(
        _cast_kernel,
        grid=(D_MODEL // CAST_TM,),
        in_specs=[pl.BlockSpec((None, CAST_TM, UA), lambda i: (layer, i, 0))],
        out_specs=pl.BlockSpec((CAST_TM, UA), lambda i: (i, 0)),
        out_shape=jax.ShapeDtypeStruct((D_MODEL, UA), BF16),
        compiler_params=_params("parallel"),
        name="cast_mix_weights",
    )(w_in)


def _shifted_cast_kernel(a_ref, b_ref, o_ref):
    shift = N_MIX_COLS % LANES
    both = jnp.concatenate([a_ref[...], b_ref[...]], axis=1)
    o_ref[...] = both[:, shift:shift + LANES].astype(o_ref.dtype)


def _merge_gate_weights(w_in, layer):
    first = N_MIX_COLS // LANES
    width = N_BRANCH * D_MODEL
    return pl.pallas_call(
        _shifted_cast_kernel,
        grid=(width // LANES,),
        in_specs=[pl.BlockSpec((None, D_MODEL, LANES), lambda j: (layer, 0, first + j)),
                  pl.BlockSpec((None, D_MODEL, LANES), lambda j: (layer, 0, first + j + 1))],
        out_specs=pl.BlockSpec((D_MODEL, LANES), lambda j: (0, j)),
        out_shape=jax.ShapeDtypeStruct((D_MODEL, width), BF16),
        compiler_params=_params("parallel"),
        name="cast_merge_gate_weights",
    )(w_in, w_in)


INPROJ_TM = 512


def _inproj_kernel(x_ref, w_ref, o_ref, oc_ref):
    xb = x_ref[...].astype(BF16)
    for c in range(0, UA, 256):
        o_ref[:, c:c + 256] = _dot(xb, w_ref[:, c:c + 256]).astype(o_ref.dtype)
    oc_ref[...] = _dot(xb, w_ref[:, C_NSAC:C_NSAC + 128])


def _inproj(x2d, w_a):
    t = x2d.shape[0]
    return pl.pallas_call(
        _inproj_kernel,
        grid=(t // INPROJ_TM,),
        in_specs=[pl.BlockSpec((INPROJ_TM, D_MODEL), lambda i: (i, 0)),
                  pl.BlockSpec((D_MODEL, UA), lambda i: (0, 0))],
        out_specs=[pl.BlockSpec((INPROJ_TM, UA), lambda i: (i, 0)),
                   pl.BlockSpec((INPROJ_TM, 128), lambda i: (i, 0))],
        out_shape=[jax.ShapeDtypeStruct((t, UA), BF16), jax.ShapeDtypeStruct((t, 128), F32)],
        compiler_params=_params("parallel"),
        name="inproj",
    )(x2d, w_a)


SEQS_PER_STEP = 2


def _seqs_per_step(bsz):
    return SEQS_PER_STEP if bsz % SEQS_PER_STEP == 0 else 1


CONV_TS = 256
CONV_HALO = 32


def _conv_kernel(cur_ref, halo_ref, dw_ref, db_ref, g_ref, b_ref, o_ref, hc_ref):
    i = pl.program_id(1)

    def glu(u):
        u = u.astype(F32)
        return u[:, :CONV_CH] * jax.nn.sigmoid(u[:, CONV_CH:])

    for s in range(cur_ref.shape[0]):
        hc_ref[s, 0:CONV_HALO, :] = jnp.where(i > 0, glu(halo_ref[s]), 0.0)
        hc_ref[s, CONV_HALO:CONV_HALO + CONV_TS, :] = glu(cur_ref[s])
        acc = jnp.broadcast_to(db_ref[...], (CONV_TS, CONV_CH))
        off = CONV_HALO - (CONV_K - 1)
        for j in range(CONV_K):
            acc = acc + hc_ref[s, off + j:off + j + CONV_TS, :] * dw_ref[j:j + 1, :]
        o_ref[s] = _silu(_layer_norm(acc, g_ref[...], b_ref[...])).astype(o_ref.dtype)


def _conv(ua, dw, db, g, b, bsz, seq):
    ns = seq // CONV_TS
    per = CONV_TS // CONV_HALO
    nb = _seqs_per_step(bsz)
    u3 = ua.reshape(bsz, seq, UA)
    dwp = jnp.concatenate([dw, jnp.zeros((32 - CONV_K, CONV_CH), F32)], axis=0)
    vec = pl.BlockSpec((1, CONV_CH), lambda bb, i: (0, 0))
    out = pl.pallas_call(
        _conv_kernel,
        grid=(bsz // nb, ns),
        in_specs=[pl.BlockSpec((nb, CONV_TS, 512), lambda bb, i: (bb, i, C_CONV // 512)),
                  pl.BlockSpec((nb, CONV_HALO, 512), lambda bb, i: (bb, jnp.maximum(i * per - 1, 0), C_CONV // 512)),
                  pl.BlockSpec((32, CONV_CH), lambda bb, i: (0, 0)), vec, vec, vec],
        out_specs=pl.BlockSpec((nb, CONV_TS, CONV_CH), lambda bb, i: (bb, i, 0)),
        out_shape=jax.ShapeDtypeStruct((bsz, seq, CONV_CH), BF16),
        scratch_shapes=[pltpu.VMEM((nb, CONV_HALO + CONV_TS, CONV_CH), F32)],
        compiler_params=_params("parallel", "parallel"),
        name="conformer_conv",
    )(u3, u3, dwp, db.reshape(1, -1), g.reshape(1, -1), b.reshape(1, -1))
    return out.reshape(bsz * seq, CONV_CH)


def _swa_kernel(sink_ref, q_ref, kvc_ref, kvp_ref, o_ref):
    i = pl.program_id(1)
    a = lax.broadcasted_iota(jnp.int32, (BLK, 2 * BLK), 0)
    c = lax.broadcasted_iota(jnp.int32, (BLK, 2 * BLK), 1)
    dist = BLK + a - c
    mask = (dist >= 0) & (dist < SWA_WINDOW) & ((c >= BLK) | (i > 0))
    distf = dist.astype(F32)
    rep = SWA_HEADS // SWA_KV_HEADS
    for s_idx in range(q_ref.shape[0]):
        q = q_ref[s_idx]
        kvc = kvc_ref[s_idx]
        kvp = kvp_ref[s_idx]
        for h in range(SWA_HEADS):
            g = h // rep
            k = jnp.concatenate([kvp[:, g * HEAD_DIM:(g + 1) * HEAD_DIM],
                                 kvc[:, g * HEAD_DIM:(g + 1) * HEAD_DIM]], axis=0)
            v = jnp.concatenate([kvp[:, BLK + g * HEAD_DIM:BLK + (g + 1) * HEAD_DIM],
                                 kvc[:, BLK + g * HEAD_DIM:BLK + (g + 1) * HEAD_DIM]], axis=0)
            s = _dot_nt(q[:, h * HEAD_DIM:(h + 1) * HEAD_DIM], k) * SCALE - SLOPES_SWA[h] * distf
            s = jnp.where(mask, s, NEG_INF)
            sink = sink_ref[h]
            m = jnp.maximum(jnp.max(s, axis=-1, keepdims=True), sink)
            e = jnp.where(mask, jnp.exp(s - m), 0.0)
            den = jnp.sum(e, axis=-1, keepdims=True) + jnp.exp(sink - m)
            p = e / jnp.maximum(den, 1e-30)
            o_ref[s_idx, :, h * HEAD_DIM:(h + 1) * HEAD_DIM] = _dot(p.astype(BF16), v).astype(o_ref.dtype)


def _swa(ua, sink, bsz, seq):
    nblk = seq // BLK
    nb = _seqs_per_step(bsz)
    u3 = ua.reshape(bsz, seq, UA)
    out = pl.pallas_call(
        _swa_kernel,
        grid=(bsz // nb, nblk),
        in_specs=[pl.BlockSpec(memory_space=pltpu.SMEM),
                  pl.BlockSpec((nb, BLK, 256), lambda bb, i: (bb, i, C_SWA // 256)),
                  pl.BlockSpec((nb, BLK, 256), lambda bb, i: (bb, i, C_SWA // 256 + 1)),
                  pl.BlockSpec((nb, BLK, 256), lambda bb, i: (bb, jnp.maximum(i - 1, 0), C_SWA // 256 + 1))],
        out_specs=pl.BlockSpec((nb, BLK, 256), lambda bb, i: (bb, i, 0)),
        out_shape=jax.ShapeDtypeStruct((bsz, seq, SWA_HEADS * HEAD_DIM), BF16),
        compiler_params=_params("parallel", "parallel"),
        name="swa",
    )(sink, u3, u3, u3)
    return out.reshape(bsz * seq, SWA_HEADS * HEAD_DIM)


def _ret_kernel(qk_ref, v_ref, g_ref, gng_ref, gnb_ref, o_ref, state_ref):
    n = pl.program_id(1)

    @pl.when(n == 0)
    def _():
        state_ref[...] = jnp.zeros_like(state_ref)

    a = lax.broadcasted_iota(jnp.int32, (BLK, BLK), 0)
    c = lax.broadcasted_iota(jnp.int32, (BLK, BLK), 1)
    diff = (a - c).astype(F32)
    idx = lax.broadcasted_iota(jnp.int32, (BLK, 1), 0).astype(F32)
    hq = RET_HEADS * RET_QK
    for s in range(qk_ref.shape[0]):
        qk = qk_ref[s]
        vv = v_ref[s]
        gate = g_ref[s].astype(F32)
        for h in range(RET_HEADS):
            lg = RET_LOG_GAMMA[h]
            q = qk[:, h * RET_QK:(h + 1) * RET_QK]
            k = qk[:, hq + h * RET_QK:hq + (h + 1) * RET_QK].astype(F32) * (RET_QK ** -0.5)
            v = vv[:, h * RET_V:(h + 1) * RET_V]
            dmat = jnp.where(diff >= 0, jnp.exp(lg * jnp.maximum(diff, 0.0)), 0.0)
            inner = _dot_nt(q, k.astype(BF16)) * dmat
            o_inner = _dot(inner.astype(BF16), v)
            zeta = jnp.exp(lg * (BLK - 1 - idx))
            xi = jnp.exp(lg * (idx + 1.0))
            kv = lax.dot_general((k * zeta).astype(BF16), v, TN_DIMS, preferred_element_type=F32)
            prev = state_ref[s * RET_HEADS + h]
            o = o_inner + _dot(q, prev.astype(BF16)) * xi
            state_ref[s * RET_HEADS + h] = math.exp(lg * BLK) * prev + kv
            mu = jnp.mean(o, axis=-1, keepdims=True)
            cen = o - mu
            var = jnp.mean(cen * cen, axis=-1, keepdims=True)
            sl = slice(h * RET_V, (h + 1) * RET_V)
            on = cen * lax.rsqrt(var + LN_EPS) * gng_ref[:, sl] + gnb_ref[:, sl]
            o_ref[s, :, sl] = (_silu(gate[:, sl]) * on).astype(o_ref.dtype)


def _retention(ua, gn_g, gn_b, bsz, seq):
    nc = seq // BLK
    nb = _seqs_per_step(bsz)
    u3 = ua.reshape(bsz, seq, UA)
    vec = pl.BlockSpec((1, RET_HEADS * RET_V), lambda bb, i: (0, 0))
    out = pl.pallas_call(
        _ret_kernel,
        grid=(bsz // nb, nc),
        in_specs=[pl.BlockSpec((nb, BLK, 512), lambda bb, i: (bb, i, C_RETQK // 512)),
                  pl.BlockSpec((nb, BLK, 512), lambda bb, i: (bb, i, C_RETV // 512)),
                  pl.BlockSpec((nb, BLK, 512), lambda bb, i: (bb, i, C_RETG // 512)),
                  vec, vec],
        out_specs=pl.BlockSpec((nb, BLK, 512), lambda bb, i: (bb, i, 0)),
        out_shape=jax.ShapeDtypeStruct((bsz, seq, RET_HEADS * RET_V), BF16),
        scratch_shapes=[pltpu.VMEM((nb * RET_HEADS, RET_QK, RET_V), F32)],
        compiler_params=_params("parallel", "arbitrary"),
        name="retention",
    )(u3, u3, u3, gn_g.reshape(1, -1), gn_b.reshape(1, -1))
    return out.reshape(bsz * seq, RET_HEADS * RET_V)


def _compress_kernel(t_ref, pek_ref, w1k_ref, w2k_ref, pev_ref, w1v_ref, w2v_ref, ok_ref, ov_ref):
    st = NSA_CMP_STRIDE
    ng = t_ref.shape[0] // st
    branches = ((pek_ref, w1k_ref, w2k_ref, ok_ref), (pev_ref, w1v_ref, w2v_ref, ov_ref))
    first = [jnp.zeros((ng, NSA_CMP_HIDDEN), F32) for _ in branches]
    second = [jnp.zeros((ng, NSA_CMP_HIDDEN), F32) for _ in branches]
    for r in range(st):
        rows = t_ref[pl.ds(r, ng, stride=st), :]
        for n, (pe_ref, w1_ref, _, _) in enumerate(branches):
            v = rows[:, n * HEAD_DIM:(n + 1) * HEAD_DIM]
            lo = (v + pe_ref[r:r + 1, :]).astype(BF16)
            hi = (v + pe_ref[st + r:st + r + 1, :]).astype(BF16)
            first[n] = first[n] + _dot(lo, w1_ref[r * HEAD_DIM:(r + 1) * HEAD_DIM, :])
            second[n] = second[n] + _dot(hi, w1_ref[(st + r) * HEAD_DIM:(st + r + 1) * HEAD_DIM, :])
    for n, (_, _, w2_ref, o_ref) in enumerate(branches):
        hid = first[n] + pltpu.roll(second[n], ng - 1, axis=0)
        act = 0.5 * hid * (1.0 + lax.erf(hid * (2.0 ** -0.5)))
        o_ref[0] = _dot(act.astype(BF16), w2_ref[...]).astype(o_ref.dtype)


def _compress(kcvc, pe_k, w1_k, w2_k, pe_v, w1_v, w2_v, bsz, seq):
    ng = seq // NSA_CMP_STRIDE
    pes = pl.BlockSpec((NSA_CMP_LEN, HEAD_DIM), lambda bb: (0, 0))
    w1s = pl.BlockSpec((NSA_CMP_LEN * HEAD_DIM, NSA_CMP_HIDDEN), lambda bb: (0, 0))
    w2s = pl.BlockSpec((NSA_CMP_HIDDEN, HEAD_DIM), lambda bb: (0, 0))
    osp = pl.BlockSpec((1, ng, HEAD_DIM), lambda bb: (bb, 0, 0))
    return pl.pallas_call(
        _compress_kernel,
        grid=(bsz,),
        in_specs=[pl.BlockSpec((seq, 128), lambda bb: (bb, 0)), pes, w1s, w2s, pes, w1s, w2s],
        out_specs=[osp, osp],
        out_shape=[jax.ShapeDtypeStruct((bsz, ng, HEAD_DIM), BF16)] * 2,
        compiler_params=_params("parallel"),
        name="nsa_compress",
    )(kcvc, pe_k, w1_k, w2_k, pe_v, w1_v, w2_v)


NSA_KC = 1024
NSA_WSPAN = NSA_WINDOW + BLK


NSA_PAD = 128
Q_LANES = NSA_HEADS * BLK


def _nsa_query_select():
    sel = np.zeros((NSA_HEADS * HEAD_DIM, NSA_HEADS * NSA_PAD), np.float32)
    for h in range(NSA_HEADS):
        for dd in range(HEAD_DIM):
            sel[h * HEAD_DIM + dd, h * NSA_PAD + dd] = SCALE
    return jnp.asarray(sel, BF16)


def _softmax_down(s):
    m = jnp.maximum(jnp.max(s, axis=0, keepdims=True), -1e29)
    e = jnp.exp(s - m)
    return e / jnp.maximum(jnp.sum(e, axis=0, keepdims=True), 1e-30)


def _nsa_kernel(q_ref, qsel_ref, kcmp_ref, vcmp_ref, ksvs_ref, kwvw_ref, gate_ref, o_ref,
                ksa_ref, kwa_ref, kca_ref, vst_ref, vwt_ref, vct_ref, ext_ref):
    i = pl.program_id(1)
    seq = ksvs_ref.shape[0]
    ncmp = kcmp_ref.shape[1]
    nslc = seq // NSA_SLC_LEN
    lane = lax.broadcasted_iota(jnp.int32, (1, NSA_PAD), 1)

    @pl.when(i == 0)
    def _():
        def aug(k, pos):
            kf = jnp.concatenate([k.astype(F32), jnp.zeros((k.shape[0], NSA_PAD - HEAD_DIM), F32)], axis=1)
            hi = (pos // 128).astype(F32)
            lo = (pos % 128).astype(F32)
            return (kf + jnp.where(lane == HEAD_DIM, hi, jnp.where(lane == HEAD_DIM + 1, lo, 0.0))).astype(BF16)

        def value_t(v):
            vf = jnp.concatenate([v.astype(F32), jnp.zeros((v.shape[0], NSA_PAD - HEAD_DIM), F32)], axis=1)
            return vf.T.astype(BF16)

        kpos = lax.broadcasted_iota(jnp.int32, (seq, 1), 0)
        ksvs = ksvs_ref[...]
        kwvw = kwvw_ref[...]
        ksa_ref[...] = aug(ksvs[:, :HEAD_DIM], kpos)
        kwa_ref[...] = aug(kwvw[:, :HEAD_DIM], kpos)
        cend = lax.broadcasted_iota(jnp.int32, (ncmp, 1), 0) * NSA_CMP_STRIDE + (NSA_CMP_LEN - 1)
        kca_ref[...] = aug(kcmp_ref[0], cend)
        vst_ref[...] = value_t(ksvs[:, HEAD_DIM:])
        vwt_ref[...] = value_t(kwvw[:, HEAD_DIM:])
        vct_ref[...] = value_t(vcmp_ref[0])
        ext_ref[...] = jnp.where(kpos // NSA_SLC_LEN == lax.broadcasted_iota(jnp.int32, (seq, NSA_PAD), 1),
                                 1.0, 0.0).astype(BF16)

    t0 = i * BLK
    tq = t0 + lax.broadcasted_iota(jnp.int32, (1, BLK), 1)
    qw = _dot(q_ref[...], qsel_ref[...])
    qaug = []
    for h in range(NSA_HEADS):
        coef = jnp.where(lane == HEAD_DIM, 128.0 * SLOPES_NSA[h], jnp.where(lane == HEAD_DIM + 1, SLOPES_NSA[h], 0.0))
        qaug.append((qw[:, h * NSA_PAD:(h + 1) * NSA_PAD] + coef).astype(BF16))
    qaug = jnp.concatenate(qaug, axis=0)

    def tile4(b):
        return jnp.concatenate([b] * NSA_HEADS, axis=1)

    cend = lax.broadcasted_iota(jnp.int32, (ncmp, 1), 0) * NSA_CMP_STRIDE + (NSA_CMP_LEN - 1)
    p_cmp = _softmax_down(_dot_nt(kca_ref[...], qaug) + tile4(jnp.where(cend <= tq, 0.0, NEG_INF)))
    o_cmp = _dot(vct_ref[...], p_cmp.astype(BF16))
    psum = p_cmp[:, 0:BLK]
    for h in range(1, NSA_HEADS):
        psum = psum + p_cmp[:, h * BLK:(h + 1) * BLK]

    sst = lax.broadcasted_iota(jnp.int32, (nslc, ncmp), 0) * NSA_SLC_LEN
    cst = lax.broadcasted_iota(jnp.int32, (nslc, ncmp), 1) * NSA_CMP_STRIDE
    overlap = jnp.where((cst < sst + NSA_SLC_LEN) & (cst + NSA_CMP_LEN > sst), 1.0, 0.0).astype(BF16)
    p_hi = psum.astype(BF16)
    p_lo = (psum - p_hi.astype(F32)).astype(BF16)
    imp = _dot(overlap, p_hi) + _dot(overlap, p_lo)
    jj = lax.broadcasted_iota(jnp.int32, (nslc, BLK), 0)
    cur = tq // NSA_SLC_LEN
    forced = (jj == 0) | (jj == cur) | (jj == cur - 1)
    imp = jnp.where(forced, NSA_FORCE, jnp.where(jj > cur, -1.0, imp))

    sel = jnp.zeros((nslc, BLK), F32)
    for _ in range(min(NSA_TOPN, nslc)):
        mx = jnp.max(imp, axis=0, keepdims=True)
        first = jnp.min(jnp.where(imp == mx, jj, nslc), axis=0, keepdims=True)
        hit = jj == first
        sel = jnp.where(hit, 1.0, sel)
        imp = jnp.where(hit, -3e38, imp)
    sel = jnp.concatenate([sel, jnp.zeros((NSA_PAD - nslc, BLK), F32)], axis=0).astype(BF16)

    def slc_step(c, carry):
        m, l, acc = carry
        start = pl.multiple_of(c * NSA_KC, NSA_KC)
        kpos = start + lax.broadcasted_iota(jnp.int32, (NSA_KC, 1), 0)
        bias = (_dot(ext_ref[pl.ds(start, NSA_KC), :], sel) - 1.0) * 1e30 + jnp.where(kpos <= tq, 0.0, NEG_INF)
        s = _dot_nt(ksa_ref[pl.ds(start, NSA_KC), :], qaug) + tile4(bias)
        m_new = jnp.maximum(m, jnp.max(s, axis=0, keepdims=True))
        alpha = jnp.exp(m - m_new)
        p = jnp.exp(s - m_new)
        l_new = alpha * l + jnp.sum(p, axis=0, keepdims=True)
        acc_new = alpha * acc + _dot(vst_ref[:, pl.ds(start, NSA_KC)], p.astype(BF16))
        return m_new, l_new, acc_new

    init = (jnp.full((1, Q_LANES), -1e29, F32), jnp.zeros((1, Q_LANES), F32), jnp.zeros((NSA_PAD, Q_LANES), F32))
    _, l_slc, acc_slc = lax.fori_loop(0, (t0 + BLK - 1) // NSA_KC + 1, slc_step, init)
    o_slc = acc_slc / jnp.maximum(l_slc, 1e-30)

    wspan = min(NSA_WSPAN, seq)
    wstart = pl.multiple_of(jnp.maximum(t0 + BLK - wspan, 0), BLK)
    wdist = tq - (wstart + lax.broadcasted_iota(jnp.int32, (wspan, 1), 0))
    wbias = jnp.where((wdist >= 0) & (wdist < NSA_WINDOW), 0.0, NEG_INF)
    p_win = _softmax_down(_dot_nt(kwa_ref[pl.ds(wstart, wspan), :], qaug) + tile4(wbias))
    o_win = _dot(vwt_ref[:, pl.ds(wstart, wspan)], p_win.astype(BF16))

    gts = jax.nn.sigmoid(gate_ref[...].astype(F32)).T
    outs = []
    for h in range(NSA_HEADS):
        cols = slice(h * BLK, (h + 1) * BLK)
        outs.append(gts[h:h + 1, :] * o_cmp[:HEAD_DIM, cols]
                    + gts[NSA_HEADS + h:NSA_HEADS + h + 1, :] * o_slc[:HEAD_DIM, cols]
                    + gts[2 * NSA_HEADS + h:2 * NSA_HEADS + h + 1, :] * o_win[:HEAD_DIM, cols])
    o_ref[...] = jnp.concatenate(outs, axis=0).T.astype(o_ref.dtype)


def _nsa(ua, kcmp, vcmp, bsz, seq):
    t = bsz * seq
    nq = seq // BLK
    ncmp = kcmp.shape[1]
    qsel = _nsa_query_select()
    staged_k = pltpu.VMEM((seq, NSA_PAD), BF16)
    staged_v = pltpu.VMEM((NSA_PAD, seq), BF16)
    return pl.pallas_call(
        _nsa_kernel,
        grid=(bsz, nq),
        in_specs=[pl.BlockSpec((BLK, 256), lambda bb, i: (bb * nq + i, C_NSAQ // 256)),
                  pl.BlockSpec(qsel.shape, lambda bb, i: (0, 0)),
                  pl.BlockSpec((1, ncmp, HEAD_DIM), lambda bb, i: (bb, 0, 0)),
                  pl.BlockSpec((1, ncmp, HEAD_DIM), lambda bb, i: (bb, 0, 0)),
                  pl.BlockSpec((seq, 128), lambda bb, i: (bb, C_NSAS // 128)),
                  pl.BlockSpec((seq, 128), lambda bb, i: (bb, C_NSAW // 128)),
                  pl.BlockSpec((BLK, 128), lambda bb, i: (bb * nq + i, C_NSAG // 128))],
        out_specs=pl.BlockSpec((BLK, 256), lambda bb, i: (bb * nq + i, 0)),
        out_shape=jax.ShapeDtypeStruct((t, NSA_HEADS * HEAD_DIM), BF16),
        scratch_shapes=[staged_k, staged_k, pltpu.VMEM((ncmp, NSA_PAD), BF16),
                        staged_v, staged_v, pltpu.VMEM((NSA_PAD, ncmp), BF16),
                        staged_k],
        compiler_params=_params("parallel", "arbitrary"),
        name="nsa_attention",
    )(ua, qsel, kcmp, vcmp, ua, ua, ua)


MERGE_TM = 512


def _merge_kernel(x_ref, conv_ref, swa_ref, ret_ref, nsa_ref, wm_ref, pc_ref, ps_ref, pr_ref, pn_ref,
                  wo_ref, g_ref, b_ref, o_ref, op_ref):
    x = x_ref[...]
    xb = x.astype(BF16)
    merged = None
    for n, (a_ref, p_ref) in enumerate(((conv_ref, pc_ref), (swa_ref, ps_ref), (ret_ref, pr_ref), (nsa_ref, pn_ref))):
        gate = jax.nn.sigmoid(_dot(xb, wm_ref[:, n * D_MODEL:(n + 1) * D_MODEL]))
        term = gate * _dot(a_ref[...], p_ref[...])
        merged = term if merged is None else merged + term
    mix = _dot(merged.astype(BF16), wo_ref[...])
    y = _layer_norm(ALPHA * x + mix, g_ref[...], b_ref[...])
    o_ref[...] = y
    op_ref[...] = y.astype(BF16)


def _merge(x2d, conv_a, swa_o, ret_o, nsa_o, w_m, p_conv, p_swa, p_ret, p_nsa, w_out, g, b):
    t = x2d.shape[0]
    tm = min(MERGE_TM, t)

    def rows(width):
        return pl.BlockSpec((tm, width), lambda i: (i, 0))

    def whole(arr):
        return pl.BlockSpec(arr.shape, lambda i: (0, 0), pipeline_mode=pl.Buffered(1))

    g2, b2 = g.reshape(1, -1), b.reshape(1, -1)
    return pl.pallas_call(
        _merge_kernel,
        grid=(t // tm,),
        in_specs=[rows(D_MODEL), rows(256), rows(256), rows(512), rows(256),
                  whole(w_m), whole(p_conv), whole(p_swa), whole(p_ret), whole(p_nsa), whole(w_out),
                  whole(g2), whole(b2)],
        out_specs=[rows(D_MODEL), rows(D_MODEL)],
        out_shape=[jax.ShapeDtypeStruct((t, D_MODEL), F32), jax.ShapeDtypeStruct((t, D_MODEL), BF16)],
        compiler_params=_params("parallel"),
        name="merge_outproj_ln",
    )(x2d, conv_a, swa_o, ret_o, nsa_o, w_m, p_conv, p_swa, p_ret, p_nsa, w_out, g2, b2)


MOE_TT = 256
RUN_ALIGN = 16
MOE_LR = 2560
MOE_CHUNK = 256
RUN_PIECES = (256, 128, 64, 32, 16)
ROW_W = D_MODEL + 128
assert MOE_LR >= TOP_K * MOE_TT + N_EXPERTS * (RUN_ALIGN - 1) and MOE_LR % MOE_CHUNK == 0


def _first_argmax(v, iota, size):
    m = jnp.max(v, axis=0, keepdims=True)
    idx = jnp.min(jnp.where(v == m, iota, size), axis=0, keepdims=True)
    return m, idx


def _router_kernel(x_ref, w_ref, b_ref, lp_ref, lpt_ref, gate_ref, tab_ref, cnt_ref, carry_ref):
    i = pl.program_id(0)
    tr = x_ref.shape[0]

    @pl.when(i == 0)
    def _():
        carry_ref[...] = jnp.zeros_like(carry_ref)

    x = x_ref[...]
    w = w_ref[...]
    xh = x.astype(BF16)
    xl = (x - xh.astype(F32)).astype(BF16)
    wh = w.astype(BF16)
    wl = (w - wh.astype(F32)).astype(BF16)
    logits = _dot_nt(wh, xh) + _dot_nt(wh, xl) + _dot_nt(wl, xh)
    scores = jax.nn.sigmoid(logits)
    biased = scores + b_ref[...]

    iota_g = lax.broadcasted_iota(jnp.int32, (GROUP_SIZE, tr), 0)
    grp = []
    for g in range(N_GROUPS):
        vg = biased[g * GROUP_SIZE:(g + 1) * GROUP_SIZE, :]
        m1, i1 = _first_argmax(vg, iota_g, GROUP_SIZE)
        m2 = jnp.max(jnp.where(iota_g == i1, -jnp.inf, vg), axis=0, keepdims=True)
        grp.append(m1 + m2)
    work = jnp.concatenate(grp, axis=0)
    iota_n = lax.broadcasted_iota(jnp.int32, (N_GROUPS, tr), 0)
    gsel = jnp.zeros((N_GROUPS, tr), F32)
    for _ in range(TOPK_GROUPS):
        _, gi = _first_argmax(work, iota_n, N_GROUPS)
        hit = iota_n == gi
        gsel = jnp.where(hit, 1.0, gsel)
        work = jnp.where(hit, -jnp.inf, work)
    emask = jnp.concatenate([jnp.broadcast_to(gsel[g:g + 1, :], (GROUP_SIZE, tr)) for g in range(N_GROUPS)], axis=0)
    work = jnp.where(emask > 0.5, biased, NEG_INF)

    iota_e = lax.broadcasted_iota(jnp.int32, (N_EXPERTS, tr), 0)
    hits, gates = [], []
    chosen = jnp.zeros((N_EXPERTS, tr), F32)
    for _ in range(TOP_K):
        _, ei = _first_argmax(work, iota_e, N_EXPERTS)
        hit = iota_e == ei
        hits.append(hit)
        gates.append(jnp.sum(jnp.where(hit, scores, 0.0), axis=0, keepdims=True))
        chosen = jnp.where(hit, 1.0, chosen)
        work = jnp.where(hit, -jnp.inf, work)
    gsum = gates[0]
    for gk in gates[1:]:
        gsum = gsum + gk
    gate_ref[...] = jnp.concatenate([gk / gsum * ROUTE_SCALE for gk in gates] + [jnp.zeros((8 - TOP_K, tr), F32)], axis=0)

    r = lax.broadcasted_iota(jnp.int32, (tr, tr), 0)
    c = lax.broadcasted_iota(jnp.int32, (tr, tr), 1)
    rank = _dot(chosen.astype(BF16), jnp.where(r < c, 1.0, 0.0).astype(BF16))
    run_len = jnp.ceil(jnp.sum(chosen, axis=1, keepdims=True) * (1.0 / RUN_ALIGN)) * RUN_ALIGN
    er = lax.broadcasted_iota(jnp.int32, (N_EXPERTS, N_EXPERTS), 0)
    ec = lax.broadcasted_iota(jnp.int32, (N_EXPERTS, N_EXPERTS), 1)
    units = jnp.broadcast_to(run_len * (1.0 / RUN_ALIGN), (N_EXPERTS, 128)).astype(BF16)
    run_off = _dot(jnp.where(ec < er, 1.0, 0.0).astype(BF16), units)[:, :1] * RUN_ALIGN
    local = [jnp.sum(jnp.where(h, run_off + rank, 0.0), axis=0, keepdims=True) for h in hits]
    lp = jnp.concatenate(local + [jnp.zeros((8 - TOP_K, tr), F32)], axis=0)
    lp_ref[...] = lp.astype(jnp.int32)

    lp128 = jnp.concatenate([lp, jnp.zeros((128 - 8, tr), F32)], axis=0)
    hi = jnp.floor(lp128 * (1.0 / 64.0))
    lo = lp128 - 64.0 * hi
    eye = jnp.where(r == c, 1.0, 0.0).astype(BF16)
    lpt_ref[...] = 64.0 * _dot_nt(eye, hi.astype(BF16)) + _dot_nt(eye, lo.astype(BF16))

    def as_row(col):
        row = jnp.sum(jnp.where(er == ec, jnp.broadcast_to(col, (N_EXPERTS, N_EXPERTS)), 0.0), axis=0, keepdims=True)
        return jnp.concatenate([row, jnp.zeros((1, 128 - N_EXPERTS), F32)], axis=1)

    total = jnp.sum(run_len, axis=0, keepdims=True)
    n_large = jnp.sum(jnp.where(run_len >= float(LARGE_PIECES[0]), 1.0, 0.0), axis=0, keepdims=True)
    lane_t = lax.broadcasted_iota(jnp.int32, (1, 128), 1)
    total_row = jnp.where(lane_t == 0, total, jnp.where(lane_t == 1, n_large, 0.0))
    tab = jnp.concatenate([as_row(run_len), as_row(run_off), as_row(carry_ref[...]), total_row,
                           jnp.zeros((4, 128), F32)], axis=0)
    tab_ref[...] = tab.astype(jnp.int32)
    carry_ref[...] = carry_ref[...] + run_len
    cnt_ref[...] = jnp.broadcast_to(carry_ref[...], cnt_ref.shape)


def _router(x1, router_w, router_b):
    t = x1.shape[0]
    tr = MOE_TT
    nt = t // tr
    col = pl.BlockSpec((8, tr), lambda i: (0, i))
    return pl.pallas_call(
        _router_kernel,
        grid=(nt,),
        in_specs=[pl.BlockSpec((tr, D_MODEL), lambda i: (i, 0)),
                  pl.BlockSpec((N_EXPERTS, D_MODEL), lambda i: (0, 0)),
                  pl.BlockSpec((N_EXPERTS, 1), lambda i: (0, 0))],
        out_specs=[col, pl.BlockSpec((tr, 128), lambda i: (i, 0)), col,
                   pl.BlockSpec((None, 8, 128), lambda i: (i, 0, 0)),
                   pl.BlockSpec((N_EXPERTS, 128), lambda i: (0, 0))],
        out_shape=[jax.ShapeDtypeStruct((8, t), jnp.int32), jax.ShapeDtypeStruct((t, 128), F32),
                   jax.ShapeDtypeStruct((8, t), F32), jax.ShapeDtypeStruct((nt, 8, 128), jnp.int32),
                   jax.ShapeDtypeStruct((N_EXPERTS, 128), F32)],
        scratch_shapes=[pltpu.VMEM((N_EXPERTS, 1), F32)],
        compiler_params=_params("arbitrary"),
        name="moe_router",
    )(x1, router_w, router_b.reshape(-1, 1))


SMALL_PIECES = (16, 32)
LARGE_PIECES = (64, 128, 256)
WAIT_PIECES = (2048, 1024, 512, 256, 128, 64, 32, 16)
assert sum(WAIT_PIECES) >= MOE_LR and sum(SMALL_PIECES + LARGE_PIECES) >= MOE_TT


def _start_run_copies(pstart_ref, tab_ref, make_copy):
    def start_pieces(e, sizes, skip_mask):
        length = tab_ref[0, e]
        local = tab_ref[1, e]
        glob = pstart_ref[e] + tab_ref[2, e]
        done = length & skip_mask
        for size in sizes:
            piece = length & size

            @pl.when(piece != 0)
            def _():
                make_copy(pl.multiple_of(local + done, RUN_ALIGN), pl.multiple_of(glob + done, RUN_ALIGN),
                          size).start()

            done = done + piece

    def small(e, carry):
        start_pieces(e, SMALL_PIECES, 0)
        return carry

    def large(e, carry):
        start_pieces(e, LARGE_PIECES, sum(SMALL_PIECES))
        return carry

    lax.fori_loop(0, N_EXPERTS, small, 0)

    @pl.when(tab_ref[3, 1] > 0)
    def _():
        lax.fori_loop(0, N_EXPERTS, large, 0)


def _wait_run_copies(tab_ref, make_copy):
    total = tab_ref[3, 0]
    for size in WAIT_PIECES:
        @pl.when((total & size) != 0)
        def _():
            make_copy(0, 0, size).wait()


def _dispatch_kernel(pstart_ref, tail_ref, taillen_ref, nused_ref, tab_ref, tabp_ref, lp_ref, gate_ref, x_ref, o_hbm,
                     buf_ref, sem):
    i = pl.program_id(0)
    slot = i % 2

    def copy_from(s):
        def make_copy(local, glob, size):
            return pltpu.make_async_copy(buf_ref.at[s, pl.ds(local, size)], o_hbm.at[pl.ds(glob, size)], sem.at[s])
        return make_copy

    make_copy = copy_from(slot)

    @pl.when(i == 0)
    def _():
        buf_ref[0, 0:MOE_BLK, :] = jnp.zeros((MOE_BLK, ROW_W), BF16)
        for action in ("start", "wait"):
            def per_expert(e, carry, action=action):
                done = 0
                for size in RUN_PIECES:
                    piece = taillen_ref[e] & size

                    @pl.when(piece != 0)
                    def _():
                        cp = make_copy(0, pl.multiple_of(tail_ref[e] + done, RUN_ALIGN), size)
                        cp.start() if action == "start" else cp.wait()

                    done = done + piece
                return carry
            lax.fori_loop(0, N_EXPERTS, per_expert, 0)

            def per_block(r, carry, action=action):
                cp = make_copy(0, pl.multiple_of(r * MOE_BLK, MOE_BLK), MOE_BLK)
                cp.start() if action == "start" else cp.wait()
                return carry
            lax.fori_loop(nused_ref[0], o_hbm.shape[0] // MOE_BLK, per_block, 0)

    xb = x_ref[...]
    lp = lp_ref[...].astype(F32)
    gates = gate_ref[...]
    g_hi = gates.astype(BF16)
    g_lo = (gates - g_hi.astype(F32)).astype(BF16)
    rows = lax.broadcasted_iota(jnp.int32, (MOE_CHUNK, MOE_TT), 0).astype(F32).astype(BF16)
    ones_col = jnp.ones((MOE_TT, ROW_W - D_MODEL), BF16)
    glane = lax.broadcasted_iota(jnp.int32, (1, ROW_W - D_MODEL), 1)
    one, zero = jnp.ones((), BF16), jnp.zeros((), BF16)
    for c in range(MOE_LR // MOE_CHUNK):
        local = (lp - float(c * MOE_CHUNK)).astype(BF16)
        hit = rows == local[0:1, :]
        member = hit
        w_hi = jnp.where(hit, g_hi[0:1, :], zero)
        w_lo = jnp.where(hit, g_lo[0:1, :], zero)
        for k in range(1, TOP_K):
            hit = rows == local[k:k + 1, :]
            member = member | hit
            w_hi = jnp.where(hit, g_hi[k:k + 1, :], w_hi)
            w_lo = jnp.where(hit, g_lo[k:k + 1, :], w_lo)
        picked = _dot(jnp.where(member, one, zero), xb)
        gate_hi = _dot(w_hi, ones_col)
        gate_lo = _dot(w_lo, ones_col)
        sl = slice(c * MOE_CHUNK, (c + 1) * MOE_CHUNK)
        buf_ref[slot, sl, :D_MODEL] = picked.astype(BF16)
        buf_ref[slot, sl, D_MODEL:] = jnp.where(glane == 0, gate_hi, jnp.where(glane == 1, gate_lo, 0.0)).astype(BF16)

    _start_run_copies(pstart_ref, tab_ref, make_copy)

    @pl.when(i > 0)
    def _():
        _wait_run_copies(tabp_ref, copy_from(1 - slot))

    @pl.when(i == pl.num_programs(0) - 1)
    def _():
        _wait_run_copies(tab_ref, make_copy)


def _dispatch(pstart, tail_start, tail_len, n_used, tab, lp, gates, x1b, rows):
    t = x1b.shape[0]
    tt = MOE_TT
    grid_spec = pltpu.PrefetchScalarGridSpec(
        num_scalar_prefetch=4,
        grid=(t // tt,),
        in_specs=[pl.BlockSpec((None, 8, 128), lambda i, *_: (i, 0, 0), memory_space=pltpu.SMEM),
                  pl.BlockSpec((None, 8, 128), lambda i, *_: (jnp.maximum(i - 1, 0), 0, 0), memory_space=pltpu.SMEM),
                  pl.BlockSpec((8, tt), lambda i, *_: (0, i)),
                  pl.BlockSpec((8, tt), lambda i, *_: (0, i)),
                  pl.BlockSpec((tt, D_MODEL), lambda i, *_: (i, 0))],
        out_specs=pl.BlockSpec(memory_space=pl.ANY),
        scratch_shapes=[pltpu.VMEM((2, MOE_LR, ROW_W), BF16), pltpu.SemaphoreType.DMA((2,))],
    )
    return pl.pallas_call(
        _dispatch_kernel,
        grid_spec=grid_spec,
        out_shape=jax.ShapeDtypeStruct((rows, ROW_W), BF16),
        compiler_params=_params("arbitrary"),
        name="moe_dispatch",
    )(pstart, tail_start, tail_len, n_used, tab, tab, lp, gates, x1b)


def _expert_kernel(be_ref, nu_ref, x_ref, wg_ref, wu_ref, wd_ref, o_ref, wgb_ref, wub_ref, wdb_ref):
    r = pl.program_id(0)

    @pl.when((r == 0) | (be_ref[r] != be_ref[jnp.maximum(r - 1, 0)]))
    def _():
        wgb_ref[...] = wg_ref[...].astype(BF16)
        wub_ref[...] = wu_ref[...].astype(BF16)
        wdb_ref[...] = wd_ref[...].astype(BF16)

    @pl.when(r < nu_ref[0])
    def _():
        xb = x_ref[:, :D_MODEL]
        gate_terms = x_ref[:, D_MODEL:D_MODEL + 2].astype(F32)
        row_gate = gate_terms[:, 0:1] + gate_terms[:, 1:2]
        h = _silu(_dot(xb, wgb_ref[...])) * _dot(xb, wub_ref[...])
        o_ref[...] = (_dot(h.astype(BF16), wdb_ref[...]) * row_gate).astype(o_ref.dtype)

    @pl.when(r >= nu_ref[0])
    def _():
        o_ref[...] = jnp.zeros_like(o_ref)


def _experts(blk_exp, n_used, xs, wg, wu, wd, layer):
    rows = xs.shape[0]
    nblk = rows // MOE_BLK
    grid_spec = pltpu.PrefetchScalarGridSpec(
        num_scalar_prefetch=2,
        grid=(nblk,),
        in_specs=[pl.BlockSpec((MOE_BLK, ROW_W), lambda r, be, nu: (jnp.minimum(r, jnp.maximum(nu[0] - 1, 0)), 0)),
                  pl.BlockSpec((None, None, D_MODEL, D_EXPERT), lambda r, be, nu: (layer, be[r], 0, 0)),
                  pl.BlockSpec((None, None, D_MODEL, D_EXPERT), lambda r, be, nu: (layer, be[r], 0, 0)),
                  pl.BlockSpec((None, None, D_EXPERT, D_MODEL), lambda r, be, nu: (layer, be[r], 0, 0))],
        out_specs=pl.BlockSpec((MOE_BLK, D_MODEL), lambda r, be, nu: (r, 0)),
        scratch_shapes=[pltpu.VMEM((D_MODEL, D_EXPERT), BF16), pltpu.VMEM((D_MODEL, D_EXPERT), BF16),
                        pltpu.VMEM((D_EXPERT, D_MODEL), BF16)],
    )
    return pl.pallas_call(
        _expert_kernel,
        grid_spec=grid_spec,
        out_shape=jax.ShapeDtypeStruct((rows, D_MODEL), BF16),
        compiler_params=_params("arbitrary"),
        name="moe_experts",
    )(blk_exp, n_used, xs, wg, wu, wd)


def _combine_kernel(pstart_ref, tab_ref, lpt_ref, x_ref, y_hbm, swg_ref, swu_ref, swd_ref, g_ref, b_ref, o_ref,
                    buf_ref, sem):
    i = pl.program_id(0)

    @pl.when(i == 0)
    def _():
        buf_ref[...] = jnp.zeros_like(buf_ref)

    def make_copy(local, glob, size):
        return pltpu.make_async_copy(y_hbm.at[pl.ds(glob, size)], buf_ref.at[pl.ds(local, size)], sem)

    _start_run_copies(pstart_ref, tab_ref, make_copy)

    x = x_ref[...]
    xb = x.astype(BF16)
    h = _silu(_dot(xb, swg_ref[...])) * _dot(xb, swu_ref[...])
    acc = ALPHA * x + _dot(h.astype(BF16), swd_ref[...])

    _wait_run_copies(tab_ref, make_copy)

    lpt = lpt_ref[...]
    cols = lax.broadcasted_iota(jnp.int32, (MOE_TT, MOE_CHUNK), 1).astype(F32).astype(BF16)
    one, zero = jnp.ones((), BF16), jnp.zeros((), BF16)
    for c in range(MOE_LR // MOE_CHUNK):
        local = (lpt - float(c * MOE_CHUNK)).astype(BF16)
        member = cols == local[:, 0:1]
        for k in range(1, TOP_K):
            member = member | (cols == local[:, k:k + 1])
        acc = acc + _dot(jnp.where(member, one, zero), buf_ref[c * MOE_CHUNK:(c + 1) * MOE_CHUNK, :])
    o_ref[...] = _layer_norm(acc, g_ref[...], b_ref[...])


def _combine(pstart, tab, lpt, x1, y_rows, swg, swu, swd, g, b):
    t = x1.shape[0]
    tt = MOE_TT

    def whole(arr):
        return pl.BlockSpec(arr.shape, lambda i, ps: (0, 0))

    g2, b2 = g.reshape(1, -1), b.reshape(1, -1)
    grid_spec = pltpu.PrefetchScalarGridSpec(
        num_scalar_prefetch=1,
        grid=(t // tt,),
        in_specs=[pl.BlockSpec((None, 8, 128), lambda i, ps: (i, 0, 0), memory_space=pltpu.SMEM),
                  pl.BlockSpec((tt, 128), lambda i, ps: (i, 0)),
                  pl.BlockSpec((tt, D_MODEL), lambda i, ps: (i, 0)),
                  pl.BlockSpec(memory_space=pl.ANY),
                  whole(swg), whole(swu), whole(swd), whole(g2), whole(b2)],
        out_specs=pl.BlockSpec((tt, D_MODEL), lambda i, ps: (i, 0)),
        scratch_shapes=[pltpu.VMEM((MOE_LR, D_MODEL), BF16), pltpu.SemaphoreType.DMA],
    )
    return pl.pallas_call(
        _combine_kernel,
        grid_spec=grid_spec,
        out_shape=jax.ShapeDtypeStruct((t, D_MODEL), F32),
        compiler_params=_params("arbitrary"),
        name="moe_combine_shared_ln",
    )(pstart, tab, lpt, x1, y_rows, swg, swu, swd, g2, b2)


def _moe(x1, x1b, layer, router_w, router_b, wg, wu, wd, swg, swu, swd, g, b):
    t = x1.shape[0]
    lp, lpt, gates, tab, cnt = _router(x1, router_w, router_b)
    rows_e = cnt[:, 0].astype(jnp.int32)
    padded = (rows_e + MOE_BLK - 1) // MOE_BLK * MOE_BLK
    pend = jnp.cumsum(padded)
    pstart = pend - padded
    max_rows = t * TOP_K + (t // MOE_TT) * N_EXPERTS * (RUN_ALIGN - 1) + N_EXPERTS * (MOE_BLK - 1)
    n_blocks = -(-max_rows // MOE_BLK)
    blk_first = jnp.arange(n_blocks, dtype=jnp.int32) * MOE_BLK
    blk_exp = jnp.minimum(jnp.sum((pend[None, :] <= blk_first[:, None]).astype(jnp.int32), axis=1), N_EXPERTS - 1)
    n_used = (pend[-1:] // MOE_BLK).astype(jnp.int32)
    xs = _dispatch(pstart, pstart + rows_e, padded - rows_e, n_used, tab, lp, gates, x1b, n_blocks * MOE_BLK)
    y_rows = _experts(blk_exp, n_used, xs, wg, wu, wd, layer)
    return _combine(pstart, tab, lpt, x1, y_rows, swg, swu, swd, g, b)


def _layer(x2d, bsz, seq, layer, w_in, moe_wg, moe_wu, moe_wd,
           conv_dw, conv_db, conv_ln_g, conv_ln_b, swa_sink, ret_gn_g, ret_gn_b,
           nsa_pe_k, nsa_w1_k, nsa_w2_k, nsa_pe_v, nsa_w1_v, nsa_w2_v, p_conv, p_swa, p_ret, p_nsa, w_out,
           ln1_g, ln1_b, router_w, router_b, sh_wg, sh_wu, sh_wd, ln2_g, ln2_b):
    bf = lambda a: a.astype(BF16)
    w_a = _mix_weights(w_in, layer)
    w_m = _merge_gate_weights(w_in, layer)
    ua, kcvc = _inproj(x2d, w_a)
    conv_a = _conv(ua, conv_dw, conv_db, conv_ln_g, conv_ln_b, bsz, seq)
    swa_o = _swa(ua, swa_sink, bsz, seq)
    ret_o = _retention(ua, ret_gn_g, ret_gn_b, bsz, seq)
    kcmp, vcmp = _compress(kcvc, nsa_pe_k, bf(nsa_w1_k), bf(nsa_w2_k), nsa_pe_v, bf(nsa_w1_v), bf(nsa_w2_v), bsz, seq)
    nsa_o = _nsa(ua, kcmp, vcmp, bsz, seq)
    x1, x1b = _merge(x2d, conv_a, swa_o, ret_o, nsa_o, w_m, bf(p_conv), bf(p_swa), bf(p_ret), bf(p_nsa),
                     bf(w_out), ln1_g, ln1_b)
    return _moe(x1, x1b, layer, router_w, router_b, moe_wg, moe_wu, moe_wd,
                bf(sh_wg), bf(sh_wu), bf(sh_wd), ln2_g, ln2_b)


def kernel(x, w_in, conv_dw, conv_db, conv_ln_g, conv_ln_b, swa_sink, ret_gn_g, ret_gn_b, nsa_pe_k, nsa_w1_k, nsa_w2_k, nsa_pe_v, nsa_w1_v, nsa_w2_v, p_conv, p_swa, p_ret, p_nsa, w_out, ln1_g, ln1_b, router_w, router_b, moe_wg, moe_wu, moe_wd, sh_wg, sh_wu, sh_wd, ln2_g, ln2_b):
    bsz, seq, d = x.shape
    params = (conv_dw, conv_db, conv_ln_g, conv_ln_b, swa_sink, ret_gn_g, ret_gn_b,
              nsa_pe_k, nsa_w1_k, nsa_w2_k, nsa_pe_v, nsa_w1_v, nsa_w2_v, p_conv, p_swa, p_ret, p_nsa, w_out,
              ln1_g, ln1_b, router_w, router_b, sh_wg, sh_wu, sh_wd, ln2_g, ln2_b)
    x2d = x.reshape(bsz * seq, d)
    for l in range(w_in.shape[0]):
        x2d = _layer(x2d, bsz, seq, l, w_in, moe_wg, moe_wu, moe_wd, *[p[l] for p in params])
    return x2d.reshape(bsz, seq, d)
```
